```python
import math
import jax, jax.numpy as jnp
from jax import lax
import numpy as np

D_MODEL = 1024
BATCH = 8
SEQ = 4096
DEPTH = 1
DEC_BATCH = 1
DEC_SEQ = 16384
PAST_LEN = 128

N_HEADS = 8
QK_NOPE = 64
QK_ROPE = 32
V_DIM = 64
Q_RANK = 256
KV_RANK = 128
ATTN_W = N_HEADS * V_DIM
Q_BLOCK = 128
ROPE_THETA = 10000.0
SSM_W = D_MODEL // 2
SSM_GROUP = 16
SSM_GROUPS = SSM_W // SSM_GROUP
SSM_STATE = 64
N_DIR = 2
DT_MIN = 1e-3
DT_MAX = 1e-1
IN_W = Q_RANK + KV_RANK + QK_ROPE + SSM_W
MIX_W = ATTN_W + SSM_W
D_FF = 4 * D_MODEL
EPS = 1e-6

kernel_name = "hymba_s5_mla_encoder"


def rms_norm(x, g):
    xf = x.astype(jnp.float32)
    y = xf * lax.rsqrt(jnp.mean(xf * xf, axis=-1, keepdims=True) + EPS)
    return (y * g.astype(jnp.float32)).astype(x.dtype)


def rope_tables(length):
    pos = jnp.arange(length, dtype=jnp.float32)
    inv = ROPE_THETA ** (-jnp.arange(0, QK_ROPE, 2, dtype=jnp.float32) / QK_ROPE)
    ang = pos[:, None] * inv[None, :]
    return jnp.cos(ang), jnp.sin(ang)


def apply_rope(x, cos, sin):
    half = QK_ROPE // 2
    xf = x.astype(jnp.float32)
    x1, x2 = xf[..., :half], xf[..., half:]
    return jnp.concatenate([x1 * cos - x2 * sin, x2 * cos + x1 * sin], axis=-1).astype(x.dtype)


def mla_attention(c_q, c_kv, k_rope_raw, q_norm_g, w_uq, kv_norm_g, w_ukv):
    b, l, _ = c_q.shape
    q = (rms_norm(c_q, q_norm_g) @ w_uq).reshape(b, l, N_HEADS, QK_NOPE + QK_ROPE)
    kv = (rms_norm(c_kv, kv_norm_g) @ w_ukv).reshape(b, l, N_HEADS, QK_NOPE + V_DIM)
    k_nope, v = kv[..., :QK_NOPE], kv[..., QK_NOPE:]
    cos, sin = rope_tables(l)
    q_rope = apply_rope(q[..., QK_NOPE:], cos[:, None, :], sin[:, None, :])
    k_rope = apply_rope(k_rope_raw, cos, sin)
    q = jnp.concatenate([q[..., :QK_NOPE], q_rope], axis=-1)
    k = jnp.concatenate(
        [k_nope, jnp.broadcast_to(k_rope[:, :, None, :], (b, l, N_HEADS, QK_ROPE))], axis=-1)
    scale = 1.0 / math.sqrt(QK_NOPE + QK_ROPE)
    nb = l // Q_BLOCK
    q_blocks = q.reshape(b, nb, Q_BLOCK, N_HEADS, QK_NOPE + QK_ROPE).transpose(1, 0, 2, 3, 4)

    def attend(qb):
        s = jnp.einsum('bqhd,bkhd->bhqk', qb, k).astype(jnp.float32) * scale
        p = jax.nn.softmax(s, axis=-1).astype(v.dtype)
        return jnp.einsum('bhqk,bkhd->bqhd', p, v)

    o = lax.map(attend, q_blocks)
    return o.transpose(1, 0, 2, 3, 4).reshape(b, l, ATTN_W)


def _ssm_combine(left, right):
    a1, b1 = left
    a2, b2 = right
    return a1 * a2, a2 * b1 + b2


def s5_bidirectional_glu(u, lam_re, lam_im, log_dt, b_re, b_im, c_re, c_im, d_skip, w_glu, b_glu):
    bsz, l, _ = u.shape
    ug = u.astype(jnp.float32).reshape(bsz, l, SSM_GROUPS, SSM_GROUP)
    uc = ug.astype(jnp.complex64)
    y = jnp.zeros_like(ug)
    for d in range(N_DIR):
        lam = lax.complex(lam_re[d].astype(jnp.float32), lam_im[d].astype(jnp.float32))
        dt = jnp.exp(log_dt[d].astype(jnp.float32))[:, None]
        a_bar = jnp.exp(lam * dt)
        b_mat = lax.complex(b_re[d].astype(jnp.float32), b_im[d].astype(jnp.float32))
        c_mat = lax.complex(c_re[d].astype(jnp.float32), c_im[d].astype(jnp.float32))
        b_bar = ((a_bar - 1.0) / lam)[..., None] * b_mat
        bu = jnp.einsum('blgh,gnh->blgn', uc, b_bar)
        a_seq = jnp.broadcast_to(a_bar, bu.shape)
        _, states = lax.associative_scan(_ssm_combine, (a_seq, bu), axis=1, reverse=(d == 1))
        y = y + jnp.real(jnp.einsum('blgn,ghn->blgh', states, c_mat))
    y = y + d_skip.astype(jnp.float32).reshape(SSM_GROUPS, SSM_GROUP) * ug
    y = y.reshape(bsz, l, SSM_W).astype(u.dtype)
    g = jax.nn.gelu(y)
    return g * jax.nn.sigmoid(g @ w_glu + b_glu)


def encoder_layer(x, norm1_g, w_in, q_norm_g, w_uq, kv_norm_g, w_ukv,
                  lam_re, lam_im, log_dt, b_re, b_im, c_re, c_im, d_skip, w_glu, b_glu,
                  attn_out_g, ssm_out_g, w_out, norm2_g, w_mlp1, w_mlp2):
    h = rms_norm(x, norm1_g)
    proj = h @ w_in
    c_q, c_kv, k_rope, u = jnp.split(
        proj, [Q_RANK, Q_RANK + KV_RANK, Q_RANK + KV_RANK + QK_ROPE], axis=-1)
    a = mla_attention(c_q, c_kv, k_rope, q_norm_g, w_uq, kv_norm_g, w_ukv)
    s = s5_bidirectional_glu(u, lam_re, lam_im, log_dt, b_re, b_im, c_re, c_im,
                             d_skip, w_glu, b_glu)
    mixed = jnp.concatenate([rms_norm(a, attn_out_g), rms_norm(s, ssm_out_g)], axis=-1)
    x = x + mixed @ w_out
    h = rms_norm(x, norm2_g)
    x = x + jnp.square(jax.nn.relu(h @ w_mlp1)) @ w_mlp2
    return x


def trunk(x, norm1_g, w_in, q_norm_g, w_uq, kv_norm_g, w_ukv,
          lam_re, lam_im, log_dt, b_re, b_im, c_re, c_im, d_skip, w_glu, b_glu,
          attn_out_g, ssm_out_g, w_out, norm2_g, w_mlp1, w_mlp2, final_g):
    for i in range(DEPTH):
        x = encoder_layer(x, norm1_g[i], w_in[i], q_norm_g[i], w_uq[i], kv_norm_g[i], w_ukv[i],
                          lam_re[i], lam_im[i], log_dt[i], b_re[i], b_im[i], c_re[i], c_im[i],
                          d_skip[i], w_glu[i], b_glu[i], attn_out_g[i], ssm_out_g[i], w_out[i],
                          norm2_g[i], w_mlp1[i], w_mlp2[i])
    return rms_norm(x, final_g)


def setup_inputs(seed: int = 0) -> dict:
    key = jax.random.key(seed)
    ks = jax.random.split(key, 32)
    f32 = jnp.float32

    def nrm(k, shape, scale):
        return jax.random.normal(k, shape, f32) * scale

    def gain(k, shape):
        return 1.0 + 0.02 * jax.random.normal(k, shape, f32)

    G, N, H = SSM_GROUPS, SSM_STATE, SSM_GROUP
    lam_re = -0.5 + 0.01 * jax.random.normal(ks[8], (DEPTH, N_DIR, G, N), f32)
    lam_im = math.pi * jnp.arange(N, dtype=f32) + 0.01 * jax.random.normal(ks[9], (DEPTH, N_DIR, G, N), f32)
    log_dt = jax.random.uniform(ks[10], (DEPTH, N_DIR, G), f32,
                                minval=math.log(DT_MIN), maxval=math.log(DT_MAX))
    return {
        "x_prompt": jax.random.normal(ks[0], (BATCH, SEQ, D_MODEL), f32),
        "x_sample": jax.random.normal(ks[1], (DEC_BATCH, DEC_SEQ, D_MODEL), f32),
        "norm1_g": gain(ks[2], (DEPTH, D_MODEL)),
        "w_in": nrm(ks[3], (DEPTH, D_MODEL, IN_W), D_MODEL ** -0.5),
        "q_norm_g": gain(ks[4], (DEPTH, Q_RANK)),
        "w_uq": nrm(ks[5], (DEPTH, Q_RANK, N_HEADS * (QK_NOPE + QK_ROPE)), Q_RANK ** -0.5),
        "kv_norm_g": gain(ks[6], (DEPTH, KV_RANK)),
        "w_ukv": nrm(ks[7], (DEPTH, KV_RANK, N_HEADS * (QK_NOPE + V_DIM)), KV_RANK ** -0.5),
        "lam_re": lam_re,
        "lam_im": lam_im,
        "log_dt": log_dt,
        "b_re": nrm(ks[11], (DEPTH, N_DIR, G, N, H), (2.0 * H) ** -0.5),
        "b_im": nrm(ks[12], (DEPTH, N_DIR, G, N, H), (2.0 * H) ** -0.5),
        "c_re": nrm(ks[13], (DEPTH, N_DIR, G, H, N), (2.0 * N) ** -0.5),
        "c_im": nrm(ks[14], (DEPTH, N_DIR, G, H, N), (2.0 * N) ** -0.5),
        "d_skip": nrm(ks[15], (DEPTH, SSM_W), 1.0),
        "w_glu": nrm(ks[16], (DEPTH, SSM_W, SSM_W), SSM_W ** -0.5),
        "b_glu": nrm(ks[17], (DEPTH, SSM_W), 0.01),
        "attn_out_g": gain(ks[18], (DEPTH, ATTN_W)),
        "ssm_out_g": gain(ks[19], (DEPTH, SSM_W)),
        "w_out": nrm(ks[20], (DEPTH, MIX_W, D_MODEL), MIX_W ** -0.5),
        "norm2_g": gain(ks[21], (DEPTH, D_MODEL)),
        "w_mlp1": nrm(ks[22], (DEPTH, D_MODEL, D_FF), D_MODEL ** -0.5),
        "w_mlp2": nrm(ks[23], (DEPTH, D_FF, D_MODEL), D_FF ** -0.5),
        "final_g": gain(ks[24], (D_MODEL,)),
    }


def reference(x_prompt, x_sample, norm1_g, w_in, q_norm_g, w_uq, kv_norm_g, w_ukv,
              lam_re, lam_im, log_dt, b_re, b_im, c_re, c_im, d_skip, w_glu, b_glu,
              attn_out_g, ssm_out_g, w_out, norm2_g, w_mlp1, w_mlp2, final_g):
    y_prompt = trunk(x_prompt, norm1_g, w_in, q_norm_g, w_uq, kv_norm_g, w_ukv,
                     lam_re, lam_im, log_dt, b_re, b_im, c_re, c_im, d_skip, w_glu, b_glu,
                     attn_out_g, ssm_out_g, w_out, norm2_g, w_mlp1, w_mlp2, final_g)
    y_sample = trunk(x_sample, norm1_g, w_in, q_norm_g, w_uq, kv_norm_g, w_ukv,
                     lam_re, lam_im, log_dt, b_re, b_im, c_re, c_im, d_skip, w_glu, b_glu,
                     attn_out_g, ssm_out_g, w_out, norm2_g, w_mlp1, w_mlp2, final_g)
    return (y_prompt, y_sample)
```

```python
import functools
import math

import jax
import jax.numpy as jnp
import numpy as np
from jax import lax
from jax.experimental import pallas as pl
from jax.experimental.pallas import tpu as pltpu

F32 = jnp.float32
BF16 = jnp.bfloat16

D_MODEL = 1024
N_HEADS = 8
QK_NOPE = 64
QK_ROPE = 32
V_DIM = 64
Q_RANK = 256
KV_RANK = 128
SSM_W = 512
SSM_GROUP = 16
SSM_GROUPS = 32
SSM_STATE = 64
D_FF = 4096
EPS = 1e-6
ROPE_THETA = 10000.0

LANES = 128
SUBLANES = 8
VMEM_LIMIT_CAP = 60000 * 1024

N_GROUPS = SUBLANES
HEAD_PAD = LANES
N_PAIRS = N_HEADS // 2
SSM_LANE_BLOCKS = SSM_W // LANES
GROUPS_PER_BLOCK = LANES // SSM_GROUP
STATES_PER_BLOCK = GROUPS_PER_BLOCK * SSM_STATE

PRE_TT = 128
ATT_BQ = 512
ATT_BK = 512
SSM_TT = 128
POST_TM = 512
FF_CHUNK = 1024


def _vmem_limit(nbytes):
    return int(min(VMEM_LIMIT_CAP, nbytes))


def _rms(x, g):
    return x * lax.rsqrt(jnp.mean(x * x, axis=-1, keepdims=True) + EPS) * g


def _dot(a, b):
    return jnp.dot(a, b, preferred_element_type=F32)


def _pre_kernel(x_ref, cos_ref, sin_ref, n1_ref, wa_ref, wu_ref, qg_ref, wqa_ref, wqb_ref,
                kvg_ref, wk_ref, wv_ref, perm_ref, q_ref, kt_ref, v_ref, u_ref, *, tt):
    rows = N_GROUPS * tt
    x = x_ref[...].reshape(rows, D_MODEL)
    h = _rms(x, n1_ref[...]).astype(BF16)
    pa = _dot(h, wa_ref[...])
    u = _dot(h, wu_ref[...])
    cqn = _rms(pa[:, :Q_RANK], qg_ref[...]).astype(BF16)
    ckvn = _rms(pa[:, Q_RANK:Q_RANK + KV_RANK], kvg_ref[...]).astype(BF16)
    cos = jnp.broadcast_to(cos_ref[...], (N_GROUPS, tt, HEAD_PAD)).reshape(rows, HEAD_PAD)
    sin = jnp.broadcast_to(sin_ref[...], (N_GROUPS, tt, HEAD_PAD)).reshape(rows, HEAD_PAD)
    o = Q_RANK + KV_RANK
    k_rope = pa[:, o:o + HEAD_PAD] * cos + pa[:, o + HEAD_PAD:o + 2 * HEAD_PAD] * sin
    qa = _dot(cqn, wqa_ref[...])
    qb = _dot(cqn, wqb_ref[...])
    k = _dot(ckvn, wk_ref[...])
    v = _dot(ckvn, wv_ref[...])
    for hd in range(N_HEADS):
        sl = slice(hd * HEAD_PAD, (hd + 1) * HEAD_PAD)
        qh = qa[:, sl] * cos + qb[:, sl] * sin
        q_ref[:, hd] = qh.astype(BF16).reshape(N_GROUPS, tt, HEAD_PAD)
        kht = (k[:, sl] + k_rope).T
        for b in range(N_GROUPS):
            kt_ref[b, hd, 0] = kht[:, b * tt:(b + 1) * tt].astype(BF16)
        v_ref[:, hd] = v[:, sl].astype(BF16).reshape(N_GROUPS, tt, HEAD_PAD)
    u_hi = u.astype(BF16)
    u_lo = (u - u_hi.astype(F32)).astype(BF16)
    perm = perm_ref[...]
    u_ref[...] = _dot(perm, u_hi) + _dot(perm, u_lo)


def _pre_call(x, cos_t, sin_t, w, perm):
    nb, lr, _ = x.shape
    tt = PRE_TT
    rows = nb * tt
    n_t = lr // tt
    tab_nb = cos_t.shape[0]
    kt_sub = ATT_BK // tt
    full = lambda a: pl.BlockSpec(a.shape, lambda j: (0,) * a.ndim)
    weights = [w["n1"], w["wa"], w["wu"], w["qg"], w["wqa"], w["wqb"], w["kvg"], w["wk"], w["wv"], perm]
    in_specs = [
        pl.BlockSpec((nb, tt, D_MODEL), lambda j: (0, j, 0)),
        pl.BlockSpec((tab_nb, tt, HEAD_PAD), lambda j: (0, j, 0)),
        pl.BlockSpec((tab_nb, tt, HEAD_PAD), lambda j: (0, j, 0)),
    ] + [full(a) for a in weights]
    out_shape = [
        jax.ShapeDtypeStruct((nb, N_HEADS, lr, HEAD_PAD), BF16),
        jax.ShapeDtypeStruct((nb, N_HEADS, lr // ATT_BK, HEAD_PAD, ATT_BK), BF16),
        jax.ShapeDtypeStruct((nb, N_HEADS, lr, HEAD_PAD), BF16),
        jax.ShapeDtypeStruct((lr * nb, SSM_W), F32),
    ]
    out_specs = [
        pl.BlockSpec((nb, N_HEADS, tt, HEAD_PAD), lambda j: (0, 0, j, 0)),
        pl.BlockSpec((nb, N_HEADS, 1, HEAD_PAD, tt), lambda j: (0, 0, j // kt_sub, 0, j % kt_sub)),
        pl.BlockSpec((nb, N_HEADS, tt, HEAD_PAD), lambda j: (0, 0, j, 0)),
        pl.BlockSpec((rows, SSM_W), lambda j: (j, 0)),
    ]
    return pl.pallas_call(
        functools.partial(_pre_kernel, tt=tt),
        grid=(n_t,),
        in_specs=in_specs,
        out_specs=out_specs,
        out_shape=out_shape,
        compiler_params=pltpu.CompilerParams(
            dimension_semantics=("arbitrary",), vmem_limit_bytes=_vmem_limit(56 << 20)),
        name="pre",
    )(x, cos_t, sin_t, *weights)


def _attn_kernel(q_ref, kt_ref, v_ref, o_ref, *, n_seg, n_kb, scale):
    bq = q_ref.shape[3]
    lane = lax.broadcasted_iota(jnp.int32, (1, HEAD_PAD), 1)
    low = lane < V_DIM
    qs = [q_ref[0, 0, j] for j in range(2)]
    neg = jnp.full((bq, 1), -0.5 * float(np.finfo(np.float32).max), F32)
    zero = jnp.zeros((bq, 1), F32)

    def body(i, carry):
        ms, ls, acc = carry
        seg = i // n_kb
        kb = i % n_kb
        r0 = pl.multiple_of(kb * ATT_BK, ATT_BK)
        new_ms, new_ls, alphas, pvs = [], [], [], []
        for j in range(2):
            s = _dot(qs[j], kt_ref[0, seg, j, kb]) * scale
            m_new = jnp.maximum(ms[j], jnp.max(s, axis=-1, keepdims=True))
            alpha = jnp.exp(ms[j] - m_new)
            p = jnp.exp(s - m_new)
            new_ls.append(alpha * ls[j] + jnp.sum(p, axis=-1, keepdims=True))
            new_ms.append(m_new)
            alphas.append(alpha)
            pvs.append(_dot(p.astype(BF16), v_ref[0, seg, j, pl.ds(r0, ATT_BK), :]))
        acc = acc * jnp.where(low, alphas[0], alphas[1]) + pvs[0] + pvs[1]
        return tuple(new_ms), tuple(new_ls), acc

    init = ((neg, neg), (zero, zero), jnp.zeros((bq, HEAD_PAD), F32))
    ms, ls, acc = lax.fori_loop(0, n_seg * n_kb, body, init)
    o_ref[0] = acc / jnp.where(low, ls[0], ls[1])


def _attn_call(q, kt, v):
    n_seq, n_seg, _, lr, _ = q.shape
    n_kb = lr // ATT_BK
    n_qb = lr // ATT_BQ
    scale = 1.0 / math.sqrt(QK_NOPE + QK_ROPE)
    resident = dict(pipeline_mode=pl.Buffered(1)) if n_seg > 1 else {}
    in_specs = [
        pl.BlockSpec((1, 1, 2, ATT_BQ, HEAD_PAD), lambda s, p, i: (s, i // n_qb, p, i % n_qb, 0)),
        pl.BlockSpec((1, n_seg, 2, n_kb, HEAD_PAD, ATT_BK), lambda s, p, i: (s, 0, p, 0, 0, 0), **resident),
        pl.BlockSpec((1, n_seg, 2, lr, HEAD_PAD), lambda s, p, i: (s, 0, p, 0, 0), **resident),
    ]
    kv_bytes = 2 * (2 * n_seg * lr * HEAD_PAD * 2) * (1 if n_seg > 1 else 2)
    return pl.pallas_call(
        functools.partial(_attn_kernel, n_seg=n_seg, n_kb=n_kb, scale=scale),
        grid=(n_seq, N_PAIRS, n_seg * n_qb),
        in_specs=in_specs,
        out_specs=pl.BlockSpec((1, ATT_BQ, HEAD_PAD), lambda s, p, i: (s, i, p)),
        out_shape=jax.ShapeDtypeStruct((n_seq, n_seg * lr, N_HEADS * V_DIM), F32),
        compiler_params=pltpu.CompilerParams(
            dimension_semantics=("arbitrary", "arbitrary", "arbitrary"),
            vmem_limit_bytes=_vmem_limit(kv_bytes + (24 << 20))),
        name="attn",
    )(q, kt, v)


def _cstep(xre, xim, are, aim, bre, bim):
    return are * xre - aim * xim + bre, are * xim + aim * xre + bim


def _ssm_kernel(uf_ref, ub_ref, bmat_ref, cmat_ref, a_ref, init_ref, *rest, tt, with_y):
    if with_y:
        yf_ref, yb_ref, fin_ref, state_ref, bu_ref, xs_ref = rest
    else:
        fin_ref, state_ref, bu_ref = rest
    j = pl.program_id(0)
    rows = N_GROUPS * tt
    n_pairs = tt // 2
    S = STATES_PER_BLOCK

    @pl.when(j == 0)
    def _():
        state_ref[...] = init_ref[...]

    for q in range(SSM_LANE_BLOCKS):
        lanes = slice(q * LANES, (q + 1) * LANES)
        bu_ref[0] = _dot(uf_ref[:, lanes].astype(BF16), bmat_ref[0, q])
        bu_ref[1] = _dot(ub_ref[:, lanes].astype(BF16), bmat_ref[1, q])
        a = [jnp.broadcast_to(a_ref[d, q], (N_GROUPS, 2 * S)) for d in range(2)]
        are = [a[d][:, :S] for d in range(2)]
        aim = [a[d][:, S:] for d in range(2)]

        def step(k, carry):
            fre, fim, bre, bim = carry
            r0 = pl.multiple_of(k * 2 * N_GROUPS, 2 * N_GROUPS)
            blk = bu_ref[0, pl.ds(r0, 2 * N_GROUPS), :]
            f1 = _cstep(fre, fim, are[0], aim[0], blk[:N_GROUPS, :S], blk[:N_GROUPS, S:])
            f2 = _cstep(f1[0], f1[1], are[0], aim[0], blk[N_GROUPS:, :S], blk[N_GROUPS:, S:])
            r1 = pl.multiple_of(rows - (k + 1) * 2 * N_GROUPS, 2 * N_GROUPS)
            blk = bu_ref[1, pl.ds(r1, 2 * N_GROUPS), :]
            b1 = _cstep(bre, bim, are[1], aim[1], blk[N_GROUPS:, :S], blk[N_GROUPS:, S:])
            b2 = _cstep(b1[0], b1[1], are[1], aim[1], blk[:N_GROUPS, :S], blk[:N_GROUPS, S:])
            if with_y:
                xs_ref[0, pl.ds(r0, 2 * N_GROUPS), :] = jnp.concatenate(
                    [jnp.concatenate(f1, axis=1), jnp.concatenate(f2, axis=1)], axis=0).astype(BF16)
                xs_ref[1, pl.ds(r1, 2 * N_GROUPS), :] = jnp.concatenate(
                    [jnp.concatenate(b2, axis=1), jnp.concatenate(b1, axis=1)], axis=0).astype(BF16)
            return f2[0], f2[1], b2[0], b2[1]

        init = (state_ref[0, q, :, :S], state_ref[0, q, :, S:], state_ref[1, q, :, :S], state_ref[1, q, :, S:])
        fre, fim, bre, bim = lax.fori_loop(0, n_pairs, step, init, unroll=2)
        state_ref[0, q] = jnp.concatenate([fre, fim], axis=1)
        state_ref[1, q] = jnp.concatenate([bre, bim], axis=1)
        if with_y:
            yf_ref[:, lanes] = _dot(xs_ref[0], cmat_ref[0, q])
            yb_ref[:, lanes] = _dot(xs_ref[1], cmat_ref[1, q])

    @pl.when(j == pl.num_programs(0) - 1)
    def _():
        fin_ref[...] = state_ref[...]


def _ssm_call(u, bmat, cmat, a, init, with_y):
    n_rows = u.shape[0]
    tt = SSM_TT
    rows = N_GROUPS * tt
    n_t = n_rows // rows
    full = lambda arr: pl.BlockSpec(arr.shape, lambda j: (0,) * arr.ndim)
    in_specs = [
        pl.BlockSpec((rows, SSM_W), lambda j: (j, 0)),
        pl.BlockSpec((rows, SSM_W), lambda j: (n_t - 1 - j, 0)),
        full(bmat), full(cmat), full(a), full(init),
    ]
    state_shape = (2, SSM_LANE_BLOCKS, N_GROUPS, 2 * STATES_PER_BLOCK)
    out_shape = [jax.ShapeDtypeStruct(state_shape, F32)]
    out_specs = [pl.BlockSpec(state_shape, lambda j: (0, 0, 0, 0))]
    scratch = [pltpu.VMEM(state_shape, F32), pltpu.VMEM((2, rows, 2 * STATES_PER_BLOCK), F32)]
    if with_y:
        out_shape = [jax.ShapeDtypeStruct((n_rows, SSM_W), F32)] * 2 + out_shape
        out_specs = [pl.BlockSpec((rows, SSM_W), lambda j: (j, 0)),
                     pl.BlockSpec((rows, SSM_W), lambda j: (n_t - 1 - j, 0))] + out_specs
        scratch.append(pltpu.VMEM((2, rows, 2 * STATES_PER_BLOCK), BF16))
    return pl.pallas_call(
        functools.partial(_ssm_kernel, tt=tt, with_y=with_y),
        grid=(n_t,),
        in_specs=in_specs,
        out_specs=out_specs,
        out_shape=out_shape,
        scratch_shapes=scratch,
        compiler_params=pltpu.CompilerParams(
            dimension_semantics=("arbitrary",), vmem_limit_bytes=_vmem_limit(48 << 20)),
        name="ssm" if with_y else "ssm_ends",
    )(u, u, bmat, cmat, a, init)


def _carry_kernel(e_ref, as_ref, i_ref):
    S = STATES_PER_BLOCK
    row = lax.broadcasted_iota(jnp.int32, (N_GROUPS, S), 0)
    for d in range(2):
        first = 0 if d == 0 else N_GROUPS - 1
        shift = 1 if d == 0 else N_GROUPS - 1
        for q in range(SSM_LANE_BLOCKS):
            e = e_ref[d, q]
            a = jnp.broadcast_to(as_ref[d, q], (N_GROUPS, 2 * S))
            ere, eim, are, aim = e[:, :S], e[:, S:], a[:, :S], a[:, S:]
            ire = jnp.zeros((N_GROUPS, S), F32)
            iim = jnp.zeros((N_GROUPS, S), F32)
            for _ in range(N_GROUPS - 1):
                tre, tim = _cstep(ire, iim, are, aim, ere, eim)
                ire = jnp.where(row == first, 0.0, pltpu.roll(tre, shift, 0))
                iim = jnp.where(row == first, 0.0, pltpu.roll(tim, shift, 0))
            i_ref[d, q] = jnp.concatenate([ire, iim], axis=1)


def _carry_call(ends, a_seg):
    return pl.pallas_call(
        _carry_kernel,
        out_shape=jax.ShapeDtypeStruct(ends.shape, F32),
        name="ssm_carry",
    )(ends, a_seg)


def _ssm_post_kernel(yf_ref, yb_ref, u_ref, dsk_ref, wglu_ref, bglu_ref, sg_ref, permt_ref, o_ref, *, tt):
    y = yf_ref[...] + yb_ref[...] + dsk_ref[...] * u_ref[...]
    g = jax.nn.gelu(y)
    z = _dot(g.astype(BF16), wglu_ref[...]) + bglu_ref[...]
    s = g * jax.nn.sigmoid(z)
    sn = _rms(s, sg_ref[...]).astype(BF16)
    o_ref[...] = _dot(permt_ref[...], sn).astype(BF16).reshape(N_GROUPS, tt, SSM_W)


def _ssm_post_call(yf, yb, u, w, permt):
    n_rows = u.shape[0]
    tt = PRE_TT
    rows = N_GROUPS * tt
    n_t = n_rows // rows
    lr = n_rows // N_GROUPS
    full = lambda a: pl.BlockSpec(a.shape, lambda j: (0,) * a.ndim)
    row_spec = pl.BlockSpec((rows, SSM_W), lambda j: (j, 0))
    weights = [w["dsk"], w["wglu"], w["bglu"], w["sg"], permt]
    return pl.pallas_call(
        functools.partial(_ssm_post_kernel, tt=tt),
        grid=(n_t,),
        in_specs=[row_spec, row_spec, row_spec] + [full(a) for a in weights],
        out_specs=pl.BlockSpec((N_GROUPS, tt, SSM_W), lambda j: (0, j, 0)),
        out_shape=jax.ShapeDtypeStruct((N_GROUPS, lr, SSM_W), BF16),
        compiler_params=pltpu.CompilerParams(
            dimension_semantics=("arbitrary",), vmem_limit_bytes=_vmem_limit(40 << 20)),
        name="ssm_post",
    )(yf, yb, u, *weights)


def _post_kernel(x_ref, a_ref, sn_ref, ag_ref, wo_ref, n2_ref, w1_ref, w2_ref, fg_ref, o_ref):
    an = _rms(a_ref[0], ag_ref[...]).astype(BF16)
    mixed = jnp.concatenate([an, sn_ref[0]], axis=-1)
    x1 = x_ref[0] + _dot(mixed, wo_ref[...])
    h2 = _rms(x1, n2_ref[...]).astype(BF16)
    acc = jnp.zeros_like(x1)
    for c in range(D_FF // FF_CHUNK):
        hid = _dot(h2, w1_ref[:, c * FF_CHUNK:(c + 1) * FF_CHUNK])
        hid = jnp.square(jnp.maximum(hid, 0.0)).astype(BF16)
        acc = acc + _dot(hid, w2_ref[c * FF_CHUNK:(c + 1) * FF_CHUNK, :])
    o_ref[0] = _rms(x1 + acc, fg_ref[...])


def _post_call(x, a, sn, w):
    nb, lr, _ = x.shape
    tm = POST_TM
    const = lambda arr: pl.BlockSpec(arr.shape, lambda b, i: (0,) * arr.ndim, pipeline_mode=pl.Buffered(1))
    weights = [w["ag"], w["wo"], w["n2"], w["w1"], w["w2"], w["fg"]]
    return pl.pallas_call(
        _post_kernel,
        grid=(nb, lr // tm),
        in_specs=[
            pl.BlockSpec((1, tm, D_MODEL), lambda b, i: (b, i, 0)),
            pl.BlockSpec((1, tm, N_HEADS * V_DIM), lambda b, i: (b, i, 0)),
            pl.BlockSpec((1, tm, SSM_W), lambda b, i: (b, i, 0)),
        ] + [const(arr) for arr in weights],
        out_specs=pl.BlockSpec((1, tm, D_MODEL), lambda b, i: (b, i, 0)),
        out_shape=jax.ShapeDtypeStruct((nb, lr, D_MODEL), F32),
        compiler_params=pltpu.CompilerParams(
            dimension_semantics=("arbitrary", "arbitrary"), vmem_limit_bytes=_vmem_limit(48 << 20)),
        name="post",
    )(x, a, sn, *weights)


def _rope_tables(positions):
    inv = ROPE_THETA ** (-jnp.arange(0, QK_ROPE, 2, dtype=F32) / QK_ROPE)
    ang = positions.astype(F32)[..., None] * inv
    cos, sin = jnp.cos(ang), jnp.sin(ang)
    ones = jnp.ones(ang.shape[:-1] + (QK_NOPE,), F32)
    pad = jnp.zeros(ang.shape[:-1] + (HEAD_PAD - QK_NOPE - QK_ROPE,), F32)
    cos_t = jnp.concatenate([ones, cos, cos, pad], axis=-1)
    sin_t = jnp.concatenate([0.0 * ones, sin, sin, pad], axis=-1)
    return cos_t, sin_t


def _rot_half_cols(w):
    half = QK_ROPE // 2
    return jnp.concatenate([-w[..., half:], w[..., :half]], axis=-1)


def _pack_weights(norm1_g, w_in, q_norm_g, w_uq, kv_norm_g, w_ukv, d_skip, w_glu, b_glu,
                  attn_out_g, ssm_out_g, w_out, norm2_g, w_mlp1, w_mlp2, final_g):
    row = lambda g: g.reshape(1, -1).astype(F32)
    o = Q_RANK + KV_RANK
    w_kr = w_in[:, o:o + QK_ROPE]
    zk = jnp.zeros((D_MODEL, QK_NOPE), F32)
    zp = jnp.zeros((D_MODEL, HEAD_PAD - QK_NOPE - QK_ROPE), F32)
    wa = jnp.concatenate([w_in[:, :o], zk, w_kr, zp, zk, _rot_half_cols(w_kr), zp], axis=1)
    wu = w_in[:, o + QK_ROPE:]
    wq = w_uq.reshape(Q_RANK, N_HEADS, QK_NOPE + QK_ROPE)
    zq = jnp.zeros((Q_RANK, N_HEADS, HEAD_PAD - QK_NOPE - QK_ROPE), F32)
    wqa = jnp.concatenate([wq, zq], axis=-1).reshape(Q_RANK, N_HEADS * HEAD_PAD)
    wqb = jnp.concatenate([jnp.zeros((Q_RANK, N_HEADS, QK_NOPE), F32), _rot_half_cols(wq[..., QK_NOPE:]), zq],
                          axis=-1).reshape(Q_RANK, N_HEADS * HEAD_PAD)
    wkv = w_ukv.reshape(KV_RANK, N_HEADS, QK_NOPE + V_DIM)
    zh = jnp.zeros((KV_RANK, N_HEADS, HEAD_PAD - QK_NOPE), F32)
    wk = jnp.concatenate([wkv[..., :QK_NOPE], zh], axis=-1).reshape(KV_RANK, N_HEADS * HEAD_PAD)
    wv_h = wkv[..., QK_NOPE:].reshape(KV_RANK, N_PAIRS, 2, V_DIM)
    zv = jnp.zeros((KV_RANK, N_PAIRS, V_DIM), F32)
    wv = jnp.stack([jnp.concatenate([wv_h[:, :, 0], zv], axis=-1),
                    jnp.concatenate([zv, wv_h[:, :, 1]], axis=-1)], axis=2).reshape(KV_RANK, N_HEADS * HEAD_PAD)
    bf = lambda a: a.astype(BF16)
    return dict(
        n1=row(norm1_g), wa=bf(wa), wu=bf(wu), qg=row(q_norm_g), wqa=bf(wqa), wqb=bf(wqb),
        kvg=row(kv_norm_g), wk=bf(wk), wv=bf(wv),
        dsk=row(d_skip), wglu=bf(w_glu), bglu=row(b_glu), sg=row(ssm_out_g),
        ag=row(attn_out_g), wo=bf(w_out), n2=row(norm2_g), w1=bf(w_mlp1), w2=bf(w_mlp2), fg=row(final_g))


def _pack_ssm(lam_re, lam_im, log_dt, b_re, b_im, c_re, c_im, seg_len):
    lam = lax.complex(lam_re.astype(F32), lam_im.astype(F32))
    dt = jnp.exp(log_dt.astype(F32))[..., None]
    a_bar = jnp.exp(lam * dt)
    b_bar = ((a_bar - 1.0) / lam)[..., None] * lax.complex(b_re.astype(F32), b_im.astype(F32))
    a_seg = a_bar
    result = jnp.ones_like(a_bar)
    n = seg_len
    while n:
        if n & 1:
            result = result * a_seg
        a_seg = a_seg * a_seg
        n >>= 1
    eye = jnp.eye(GROUPS_PER_BLOCK, dtype=F32)
    nq, gb = SSM_LANE_BLOCKS, GROUPS_PER_BLOCK

    def b_block(part):
        p = part.reshape(2, nq, gb, SSM_STATE, SSM_GROUP)
        return jnp.einsum("dqgnh,gk->dqghkn", p, eye).reshape(2, nq, LANES, STATES_PER_BLOCK)

    def c_block(part):
        p = part.reshape(2, nq, gb, SSM_GROUP, SSM_STATE)
        return jnp.einsum("dqghn,gk->dqgnkh", p, eye).reshape(2, nq, STATES_PER_BLOCK, LANES)

    def a_block(z):
        z = z.reshape(2, nq, 1, STATES_PER_BLOCK)
        return jnp.concatenate([jnp.real(z), jnp.imag(z)], axis=-1)

    bmat = jnp.concatenate([b_block(jnp.real(b_bar)), b_block(jnp.imag(b_bar))], axis=-1).astype(BF16)
    cmat = jnp.concatenate([c_block(c_re.astype(F32)), -c_block(c_im.astype(F32))], axis=-2).astype(BF16)
    return bmat, cmat, a_block(a_bar), a_block(result)


def _perm_matrix(tt):
    rows = N_GROUPS * tt
    dst = np.arange(rows)
    src = (dst % N_GROUPS) * tt + dst // N_GROUPS
    p = np.zeros((rows, rows), np.float32)
    p[dst, src] = 1.0
    return jnp.asarray(p, BF16)


def _trunk(x, positions, segmented, w, ssm):
    nb, lr, _ = x.shape
    bmat, cmat, a_blk, a_seg = ssm
    cos_t, sin_t = _rope_tables(positions)
    perm = _perm_matrix(PRE_TT)
    q, kt, v, u = _pre_call(x, cos_t, sin_t, w, perm)
    if segmented:
        q5, kt5, v5 = q[None], kt[None], v[None]
    else:
        q5, kt5, v5 = q[:, None], kt[:, None], v[:, None]
    att = _attn_call(q5, kt5, v5).reshape(nb, lr, N_HEADS * V_DIM)
    init = jnp.zeros((2, SSM_LANE_BLOCKS, N_GROUPS, 2 * STATES_PER_BLOCK), F32)
    if segmented:
        (ends,) = _ssm_call(u, bmat, cmat, a_blk, init, with_y=False)
        init = _carry_call(ends, a_seg)
    yf, yb, _ = _ssm_call(u, bmat, cmat, a_blk, init, with_y=True)
    sn = _ssm_post_call(yf, yb, u, w, perm.T)
    return _post_call(x, att, sn, w)


def kernel(x_prompt, x_sample, norm1_g, w_in, q_norm_g, w_uq, kv_norm_g, w_ukv, lam_re, lam_im, log_dt,
           b_re, b_im, c_re, c_im, d_skip, w_glu, b_glu, attn_out_g, ssm_out_g, w_out, norm2_g, w_mlp1,
           w_mlp2, final_g):
    assert norm1_g.shape[0] == 1, "single-layer trunk"
    w = _pack_weights(norm1_g[0], w_in[0], q_norm_g[0], w_uq[0], kv_norm_g[0], w_ukv[0], d_skip[0], w_glu[0],
                      b_glu[0], attn_out_g[0], ssm_out_g[0], w_out[0], norm2_g[0], w_mlp1[0], w_mlp2[0], final_g)
    bp, lp, _ = x_prompt.shape
    bs, ls, _ = x_sample.shape
    assert bp == N_GROUPS and bs == 1 and ls % N_GROUPS == 0
    seg = ls // N_GROUPS
    ssm = _pack_ssm(lam_re[0], lam_im[0], log_dt[0], b_re[0], b_im[0], c_re[0], c_im[0], seg)
    y_prompt = _trunk(x_prompt, jnp.arange(lp)[None], False, w, ssm)
    y_sample = _trunk(x_sample.reshape(N_GROUPS, seg, D_MODEL), jnp.arange(ls).reshape(N_GROUPS, seg), True, w, ssm)
    return y_prompt, y_sample.reshape(bs, ls, D_MODEL)
```

```python
import functools
import math

import jax
import jax.numpy as jnp
import numpy as np
from jax import lax
from jax.experimental import pallas as pl
from jax.experimental.pallas import tpu as pltpu

F32 = jnp.float32
BF16 = jnp.bfloat16

D_MODEL = 1024
N_HEADS = 8
QK_NOPE = 64
QK_ROPE = 32
V_DIM = 64
Q_RANK = 256
KV_RANK = 128
SSM_W = 512
SSM_GROUP = 16
SSM_GROUPS = 32
SSM_STATE = 64
D_FF = 4096
EPS = 1e-6
ROPE_THETA = 10000.0

LANES = 128
SUBLANES = 8
VMEM_LIMIT_CAP = 60000 * 1024

N_GROUPS = SUBLANES
HEAD_PAD = LANES
N_PAIRS = N_HEADS // 2
SSM_LANE_BLOCKS = SSM_W // LANES
GROUPS_PER_BLOCK = LANES // SSM_GROUP
STATES_PER_BLOCK = GROUPS_PER_BLOCK * SSM_STATE

PRE_TT = 128
ATT_BQ = 512
ATT_BK = 512
ATT_RB = 32
Q_SCALE = math.log2(math.e) / math.sqrt(QK_NOPE + QK_ROPE)
SSM_TT = 128
POST_TM = 512
FF_CHUNK = 1024


def _vmem_limit(nbytes):
    return int(min(VMEM_LIMIT_CAP, nbytes))


def _rms(x, g):
    return x * lax.rsqrt(jnp.mean(x * x, axis=-1, keepdims=True) + EPS) * g


def _dot(a, b):
    return jnp.dot(a, b, preferred_element_type=F32)


def _pre_kernel(x_ref, cos_ref, sin_ref, n1_ref, wa_ref, wu_ref, qg_ref, wqa_ref, wqb_ref,
                kvg_ref, wk_ref, wv_ref, perm_ref, q_ref, kt_ref, v_ref, u_ref, *, tt):
    rows = N_GROUPS * tt
    x = x_ref[...].reshape(rows, D_MODEL)
    h = _rms(x, n1_ref[...]).astype(BF16)
    pa = _dot(h, wa_ref[...])
    u = _dot(h, wu_ref[...])
    cqn = _rms(pa[:, :Q_RANK], qg_ref[...]).astype(BF16)
    ckvn = _rms(pa[:, Q_RANK:Q_RANK + KV_RANK], kvg_ref[...]).astype(BF16)
    cos = jnp.broadcast_to(cos_ref[...], (N_GROUPS, tt, HEAD_PAD)).reshape(rows, HEAD_PAD)
    sin = jnp.broadcast_to(sin_ref[...], (N_GROUPS, tt, HEAD_PAD)).reshape(rows, HEAD_PAD)
    o = Q_RANK + KV_RANK
    k_rope = pa[:, o:o + HEAD_PAD] * cos + pa[:, o + HEAD_PAD:o + 2 * HEAD_PAD] * sin
    qa = _dot(cqn, wqa_ref[...])
    qb = _dot(cqn, wqb_ref[...])
    k = _dot(ckvn, wk_ref[...])
    v = _dot(ckvn, wv_ref[...])
    for hd in range(N_HEADS):
        sl = slice(hd * HEAD_PAD, (hd + 1) * HEAD_PAD)
        qh = (qa[:, sl] * cos + qb[:, sl] * sin) * Q_SCALE
        q_ref[:, hd] = qh.astype(BF16).reshape(N_GROUPS, tt, HEAD_PAD)
        kht = (k[:, sl] + k_rope).T
        for b in range(N_GROUPS):
            kt_ref[b, hd, 0] = kht[:, b * tt:(b + 1) * tt].astype(BF16)
        v_ref[:, hd] = v[:, sl].astype(BF16).reshape(N_GROUPS, tt, HEAD_PAD)
    u_hi = u.astype(BF16)
    u_lo = (u - u_hi.astype(F32)).astype(BF16)
    perm = perm_ref[...]
    u_ref[...] = _dot(perm, u_hi) + _dot(perm, u_lo)


def _pre_call(x, cos_t, sin_t, w, perm):
    nb, lr, _ = x.shape
    tt = PRE_TT
    rows = nb * tt
    n_t = lr // tt
    tab_nb = cos_t.shape[0]
    kt_sub = ATT_BK // tt
    full = lambda a: pl.BlockSpec(a.shape, lambda j: (0,) * a.ndim)
    weights = [w["n1"], w["wa"], w["wu"], w["qg"], w["wqa"], w["wqb"], w["kvg"], w["wk"], w["wv"], perm]
    in_specs = [
        pl.BlockSpec((nb, tt, D_MODEL), lambda j: (0, j, 0)),
        pl.BlockSpec((tab_nb, tt, HEAD_PAD), lambda j: (0, j, 0)),
        pl.BlockSpec((tab_nb, tt, HEAD_PAD), lambda j: (0, j, 0)),
    ] + [full(a) for a in weights]
    out_shape = [
        jax.ShapeDtypeStruct((nb, N_HEADS, lr, HEAD_PAD), BF16),
        jax.ShapeDtypeStruct((nb, N_HEADS, lr // ATT_BK, HEAD_PAD, ATT_BK), BF16),
        jax.ShapeDtypeStruct((nb, N_HEADS, lr, HEAD_PAD), BF16),
        jax.ShapeDtypeStruct((lr * nb, SSM_W), F32),
    ]
    out_specs = [
        pl.BlockSpec((nb, N_HEADS, tt, HEAD_PAD), lambda j: (0, 0, j, 0)),
        pl.BlockSpec((nb, N_HEADS, 1, HEAD_PAD, tt), lambda j: (0, 0, j // kt_sub, 0, j % kt_sub)),
        pl.BlockSpec((nb, N_HEADS, tt, HEAD_PAD), lambda j: (0, 0, j, 0)),
        pl.BlockSpec((rows, SSM_W), lambda j: (j, 0)),
    ]
    return pl.pallas_call(
        functools.partial(_pre_kernel, tt=tt),
        grid=(n_t,),
        in_specs=in_specs,
        out_specs=out_specs,
        out_shape=out_shape,
        compiler_params=pltpu.CompilerParams(
            dimension_semantics=("arbitrary",), vmem_limit_bytes=_vmem_limit(56 << 20)),
        name="pre",
    )(x, cos_t, sin_t, *weights)


def _tree(op, xs):
    while len(xs) > 1:
        xs = [op(xs[i], xs[i + 1]) for i in range(0, len(xs) - 1, 2)] + ([xs[-1]] if len(xs) % 2 else [])
    return xs[0]


def _attn_kernel(q_ref, kt_ref, v_ref, o_ref, s_ref, p_ref, alpha_ref, m_ref, l_ref, acc_ref, *, n_seg, n_kb):
    bq, bk = ATT_BQ, ATT_BK
    low = lax.broadcasted_iota(jnp.int32, (1, HEAD_PAD), 1) < V_DIM
    m_ref[...] = jnp.full(m_ref.shape, -0.5 * float(np.finfo(np.float32).max), F32)
    l_ref[...] = jnp.zeros(l_ref.shape, F32)
    acc_ref[...] = jnp.zeros(acc_ref.shape, F32)

    n = n_seg * n_kb

    def scores(c, slot):
        for j in range(2):
            s_ref[slot, j] = _dot(q_ref[0, 0, j], kt_ref[0, c // n_kb, j, c % n_kb])

    def softmax(slot):
        for j in range(2):
            for rb in range(bq // ATT_RB):
                rows = slice(rb * ATT_RB, (rb + 1) * ATT_RB)
                s = s_ref[slot, j, rows, :]
                tiles = [s[:, t * LANES:(t + 1) * LANES] for t in range(bk // LANES)]
                m_old = m_ref[j, rows, :]
                m_new = jnp.maximum(m_old, jnp.max(_tree(jnp.maximum, tiles), axis=-1, keepdims=True))
                alpha = jnp.exp2(m_old - m_new)
                ps = [jnp.exp2(t - m_new) for t in tiles]
                l_ref[j, rows, :] = alpha * l_ref[j, rows, :] + _tree(jnp.add, ps)
                m_ref[j, rows, :] = m_new
                alpha_ref[slot, j, rows, :] = alpha
                p_ref[slot, j, rows, :] = jnp.concatenate(ps, axis=1).astype(BF16)

    def weighted_values(c, slot):
        r0 = (c % n_kb) * bk
        if not isinstance(c, int):
            r0 = pl.multiple_of(r0, bk)
        pv = [_dot(p_ref[slot, j], v_ref[0, c // n_kb, j, pl.ds(r0, bk), :]) for j in range(2)]
        acc_ref[...] = acc_ref[...] * jnp.where(low, alpha_ref[slot, 0], alpha_ref[slot, 1]) + pv[0] + pv[1]

    def stage(t, parity, do_scores=True, do_softmax=True, do_values=True):
        if do_scores:
            scores(t, parity)
        if do_softmax:
            softmax(1 - parity)
        if do_values:
            weighted_values(t - 2, parity)

    for t in range(min(2, n + 2)):
        stage(t, t % 2, t < n, 1 <= t <= n, False)
    n_pairs = max(n - 2, 0) // 2
    if n_pairs:
        def body(k, _):
            stage(2 + 2 * k, 0)
            stage(3 + 2 * k, 1)
            return 0
        lax.fori_loop(0, n_pairs, body, 0)
    for t in range(2 + 2 * n_pairs, n + 2):
        stage(t, t % 2, t < n, t <= n, True)
    l0 = jnp.sum(l_ref[0], axis=-1, keepdims=True)
    l1 = jnp.sum(l_ref[1], axis=-1, keepdims=True)
    o_ref[0] = acc_ref[...] / jnp.where(low, l0, l1)


def _attn_call(q, kt, v):
    n_seq, n_seg, _, lr, _ = q.shape
    n_kb = lr // ATT_BK
    n_qb = lr // ATT_BQ
    resident = dict(pipeline_mode=pl.Buffered(1)) if n_seg > 1 else {}
    scratch = [
        pltpu.VMEM((2, 2, ATT_BQ, ATT_BK), F32),
        pltpu.VMEM((2, 2, ATT_BQ, ATT_BK), BF16),
        pltpu.VMEM((2, 2, ATT_BQ, LANES), F32),
        pltpu.VMEM((2, ATT_BQ, LANES), F32),
        pltpu.VMEM((2, ATT_BQ, LANES), F32),
        pltpu.VMEM((ATT_BQ, HEAD_PAD), F32),
    ]
    in_specs = [
        pl.BlockSpec((1, 1, 2, ATT_BQ, HEAD_PAD), lambda s, p, i: (s, i // n_qb, p, i % n_qb, 0)),
        pl.BlockSpec((1, n_seg, 2, n_kb, HEAD_PAD, ATT_BK), lambda s, p, i: (s, 0, p, 0, 0, 0), **resident),
        pl.BlockSpec((1, n_seg, 2, lr, HEAD_PAD), lambda s, p, i: (s, 0, p, 0, 0), **resident),
    ]
    kv_bytes = 2 * (2 * n_seg * lr * HEAD_PAD * 2) * (1 if n_seg > 1 else 2)
    return pl.pallas_call(
        functools.partial(_attn_kernel, n_seg=n_seg, n_kb=n_kb),
        grid=(n_seq, N_PAIRS, n_seg * n_qb),
        in_specs=in_specs,
        out_specs=pl.BlockSpec((1, ATT_BQ, HEAD_PAD), lambda s, p, i: (s, i, p)),
        out_shape=jax.ShapeDtypeStruct((n_seq, n_seg * lr, N_HEADS * V_DIM), F32),
        scratch_shapes=scratch,
        compiler_params=pltpu.CompilerParams(
            dimension_semantics=("arbitrary", "arbitrary", "arbitrary"),
            vmem_limit_bytes=_vmem_limit(kv_bytes + (24 << 20))),
        name="attn",
    )(q, kt, v)


def _cstep(xre, xim, are, aim, bre, bim):
    return are * xre - aim * xim + bre, are * xim + aim * xre + bim


def _ssm_kernel(uf_ref, ub_ref, bmat_ref, cmat_ref, a_ref, init_ref, *rest, tt, with_y):
    if with_y:
        yf_ref, yb_ref, fin_ref, state_ref, bu_ref, xs_ref = rest
    else:
        fin_ref, state_ref, bu_ref = rest
    j = pl.program_id(0)
    rows = N_GROUPS * tt
    n_pairs = tt // 2
    S = STATES_PER_BLOCK

    @pl.when(j == 0)
    def _():
        state_ref[...] = init_ref[...]

    for q in range(SSM_LANE_BLOCKS):
        lanes = slice(q * LANES, (q + 1) * LANES)
        bu_ref[0] = _dot(uf_ref[:, lanes].astype(BF16), bmat_ref[0, q])
        bu_ref[1] = _dot(ub_ref[:, lanes].astype(BF16), bmat_ref[1, q])
        a = [jnp.broadcast_to(a_ref[d, q], (N_GROUPS, 2 * S)) for d in range(2)]
        are = [a[d][:, :S] for d in range(2)]
        aim = [a[d][:, S:] for d in range(2)]

        def step(k, carry):
            fre, fim, bre, bim = carry
            r0 = pl.multiple_of(k * 2 * N_GROUPS, 2 * N_GROUPS)
            blk = bu_ref[0, pl.ds(r0, 2 * N_GROUPS), :]
            f1 = _cstep(fre, fim, are[0], aim[0], blk[:N_GROUPS, :S], blk[:N_GROUPS, S:])
            f2 = _cstep(f1[0], f1[1], are[0], aim[0], blk[N_GROUPS:, :S], blk[N_GROUPS:, S:])
            r1 = pl.multiple_of(rows - (k + 1) * 2 * N_GROUPS, 2 * N_GROUPS)
            blk = bu_ref[1, pl.ds(r1, 2 * N_GROUPS), :]
            b1 = _cstep(bre, bim, are[1], aim[1], blk[N_GROUPS:, :S], blk[N_GROUPS:, S:])
            b2 = _cstep(b1[0], b1[1], are[1], aim[1], blk[:N_GROUPS, :S], blk[:N_GROUPS, S:])
            if with_y:
                xs_ref[0, pl.ds(r0, 2 * N_GROUPS), :] = jnp.concatenate(
                    [jnp.concatenate(f1, axis=1), jnp.concatenate(f2, axis=1)], axis=0).astype(BF16)
                xs_ref[1, pl.ds(r1, 2 * N_GROUPS), :] = jnp.concatenate(
                    [jnp.concatenate(b2, axis=1), jnp.concatenate(b1, axis=1)], axis=0).astype(BF16)
            return f2[0], f2[1], b2[0], b2[1]

        init = (state_ref[0, q, :, :S], state_ref[0, q, :, S:], state_ref[1, q, :, :S], state_ref[1, q, :, S:])
        fre, fim, bre, bim = lax.fori_loop(0, n_pairs, step, init, unroll=2)
        state_ref[0, q] = jnp.concatenate([fre, fim], axis=1)
        state_ref[1, q] = jnp.concatenate([bre, bim], axis=1)
        if with_y:
            yf_ref[:, lanes] = _dot(xs_ref[0], cmat_ref[0, q])
            yb_ref[:, lanes] = _dot(xs_ref[1], cmat_ref[1, q])

    @pl.when(j == pl.num_programs(0) - 1)
    def _():
        fin_ref[...] = state_ref[...]


def _ssm_call(u, bmat, cmat, a, init, with_y):
    n_rows = u.shape[0]
    tt = SSM_TT
    rows = N_GROUPS * tt
    n_t = n_rows // rows
    full = lambda arr: pl.BlockSpec(arr.shape, lambda j: (0,) * arr.ndim)
    in_specs = [
        pl.BlockSpec((rows, SSM_W), lambda j: (j, 0)),
        pl.BlockSpec((rows, SSM_W), lambda j: (n_t - 1 - j, 0)),
        full(bmat), full(cmat), full(a), full(init),
    ]
    state_shape = (2, SSM_LANE_BLOCKS, N_GROUPS, 2 * STATES_PER_BLOCK)
    out_shape = [jax.ShapeDtypeStruct(state_shape, F32)]
    out_specs = [pl.BlockSpec(state_shape, lambda j: (0, 0, 0, 0))]
    scratch = [pltpu.VMEM(state_shape, F32), pltpu.VMEM((2, rows, 2 * STATES_PER_BLOCK), F32)]
    if with_y:
        out_shape = [jax.ShapeDtypeStruct((n_rows, SSM_W), F32)] * 2 + out_shape
        out_specs = [pl.BlockSpec((rows, SSM_W), lambda j: (j, 0)),
                     pl.BlockSpec((rows, SSM_W), lambda j: (n_t - 1 - j, 0))] + out_specs
        scratch.append(pltpu.VMEM((2, rows, 2 * STATES_PER_BLOCK), BF16))
    return pl.pallas_call(
        functools.partial(_ssm_kernel, tt=tt, with_y=with_y),
        grid=(n_t,),
        in_specs=in_specs,
        out_specs=out_specs,
        out_shape=out_shape,
        scratch_shapes=scratch,
        compiler_params=pltpu.CompilerParams(
            dimension_semantics=("arbitrary",), vmem_limit_bytes=_vmem_limit(48 << 20)),
        name="ssm" if with_y else "ssm_ends",
    )(u, u, bmat, cmat, a, init)


def _carry_kernel(e_ref, as_ref, i_ref):
    S = STATES_PER_BLOCK
    row = lax.broadcasted_iota(jnp.int32, (N_GROUPS, S), 0)
    for d in range(2):
        first = 0 if d == 0 else N_GROUPS - 1
        shift = 1 if d == 0 else N_GROUPS - 1
        for q in range(SSM_LANE_BLOCKS):
            e = e_ref[d, q]
            a = jnp.broadcast_to(as_ref[d, q], (N_GROUPS, 2 * S))
            ere, eim, are, aim = e[:, :S], e[:, S:], a[:, :S], a[:, S:]
            ire = jnp.zeros((N_GROUPS, S), F32)
            iim = jnp.zeros((N_GROUPS, S), F32)
            for _ in range(N_GROUPS - 1):
                tre, tim = _cstep(ire, iim, are, aim, ere, eim)
                ire = jnp.where(row == first, 0.0, pltpu.roll(tre, shift, 0))
                iim = jnp.where(row == first, 0.0, pltpu.roll(tim, shift, 0))
            i_ref[d, q] = jnp.concatenate([ire, iim], axis=1)


def _carry_call(ends, a_seg):
    return pl.pallas_call(
        _carry_kernel,
        out_shape=jax.ShapeDtypeStruct(ends.shape, F32),
        name="ssm_carry",
    )(ends, a_seg)


def _ssm_post_kernel(yf_ref, yb_ref, u_ref, dsk_ref, wglu_ref, bglu_ref, sg_ref, permt_ref, o_ref, *, tt):
    y = yf_ref[...] + yb_ref[...] + dsk_ref[...] * u_ref[...]
    g = jax.nn.gelu(y)
    z = _dot(g.astype(BF16), wglu_ref[...]) + bglu_ref[...]
    s = g * jax.nn.sigmoid(z)
    sn = _rms(s, sg_ref[...]).astype(BF16)
    o_ref[...] = _dot(permt_ref[...], sn).astype(BF16).reshape(N_GROUPS, tt, SSM_W)


def _ssm_post_call(yf, yb, u, w, permt):
    n_rows = u.shape[0]
    tt = PRE_TT
    rows = N_GROUPS * tt
    n_t = n_rows // rows
    lr = n_rows // N_GROUPS
    full = lambda a: pl.BlockSpec(a.shape, lambda j: (0,) * a.ndim)
    row_spec = pl.BlockSpec((rows, SSM_W), lambda j: (j, 0))
    weights = [w["dsk"], w["wglu"], w["bglu"], w["sg"], permt]
    return pl.pallas_call(
        functools.partial(_ssm_post_kernel, tt=tt),
        grid=(n_t,),
        in_specs=[row_spec, row_spec, row_spec] + [full(a) for a in weights],
        out_specs=pl.BlockSpec((N_GROUPS, tt, SSM_W), lambda j: (0, j, 0)),
        out_shape=jax.ShapeDtypeStruct((N_GROUPS, lr, SSM_W), BF16),
        compiler_params=pltpu.CompilerParams(
            dimension_semantics=("arbitrary",), vmem_limit_bytes=_vmem_limit(40 << 20)),
        name="ssm_post",
    )(yf, yb, u, *weights)


def _post_kernel(x_ref, a_ref, sn_ref, ag_ref, wo_ref, n2_ref, w1_ref, w2_ref, fg_ref, o_ref):
    an = _rms(a_ref[0], ag_ref[...]).astype(BF16)
    mixed = jnp.concatenate([an, sn_ref[0]], axis=-1)
    x1 = x_ref[0] + _dot(mixed, wo_ref[...])
    h2 = _rms(x1, n2_ref[...]).astype(BF16)
    acc = jnp.zeros_like(x1)
    for c in range(D_FF // FF_CHUNK):
        hid = _dot(h2, w1_ref[:, c * FF_CHUNK:(c + 1) * FF_CHUNK])
        hid = jnp.square(jnp.maximum(hid, 0.0)).astype(BF16)
        acc = acc + _dot(hid, w2_ref[c * FF_CHUNK:(c + 1) * FF_CHUNK, :])
    o_ref[0] = _rms(x1 + acc, fg_ref[...])


def _post_call(x, a, sn, w):
    nb, lr, _ = x.shape
    tm = POST_TM
    const = lambda arr: pl.BlockSpec(arr.shape, lambda b, i: (0,) * arr.ndim, pipeline_mode=pl.Buffered(1))
    weights = [w["ag"], w["wo"], w["n2"], w["w1"], w["w2"], w["fg"]]
    return pl.pallas_call(
        _post_kernel,
        grid=(nb, lr // tm),
        in_specs=[
            pl.BlockSpec((1, tm, D_MODEL), lambda b, i: (b, i, 0)),
            pl.BlockSpec((1, tm, N_HEADS * V_DIM), lambda b, i: (b, i, 0)),
            pl.BlockSpec((1, tm, SSM_W), lambda b, i: (b, i, 0)),
        ] + [const(arr) for arr in weights],
        out_specs=pl.BlockSpec((1, tm, D_MODEL), lambda b, i: (b, i, 0)),
        out_shape=jax.ShapeDtypeStruct((nb, lr, D_MODEL), F32),
        compiler_params=pltpu.CompilerParams(
            dimension_semantics=("arbitrary", "arbitrary"), vmem_limit_bytes=_vmem_limit(48 << 20)),
        name="post",
    )(x, a, sn, *weights)


def _rope_tables(positions):
    inv = ROPE_THETA ** (-jnp.arange(0, QK_ROPE, 2, dtype=F32) / QK_ROPE)
    ang = positions.astype(F32)[..., None] * inv
    cos, sin = jnp.cos(ang), jnp.sin(ang)
    ones = jnp.ones(ang.shape[:-1] + (QK_NOPE,), F32)
    pad = jnp.zeros(ang.shape[:-1] + (HEAD_PAD - QK_NOPE - QK_ROPE,), F32)
    cos_t = jnp.concatenate([ones, cos, cos, pad], axis=-1)
    sin_t = jnp.concatenate([0.0 * ones, sin, sin, pad], axis=-1)
    return cos_t, sin_t


def _rot_half_cols(w):
    half = QK_ROPE // 2
    return jnp.concatenate([-w[..., half:], w[..., :half]], axis=-1)


def _pack_weights(norm1_g, w_in, q_norm_g, w_uq, kv_norm_g, w_ukv, d_skip, w_glu, b_glu,
                  attn_out_g, ssm_out_g, w_out, norm2_g, w_mlp1, w_mlp2, final_g):
    row = lambda g: g.reshape(1, -1).astype(F32)
    o = Q_RANK + KV_RANK
    w_kr = w_in[:, o:o + QK_ROPE]
    zk = jnp.zeros((D_MODEL, QK_NOPE), F32)
    zp = jnp.zeros((D_MODEL, HEAD_PAD - QK_NOPE - QK_ROPE), F32)
    wa = jnp.concatenate([w_in[:, :o], zk, w_kr, zp, zk, _rot_half_cols(w_kr), zp], axis=1)
    wu = w_in[:, o + QK_ROPE:]
    wq = w_uq.reshape(Q_RANK, N_HEADS, QK_NOPE + QK_ROPE)
    zq = jnp.zeros((Q_RANK, N_HEADS, HEAD_PAD - QK_NOPE - QK_ROPE), F32)
    wqa = jnp.concatenate([wq, zq], axis=-1).reshape(Q_RANK, N_HEADS * HEAD_PAD)
    wqb = jnp.concatenate([jnp.zeros((Q_RANK, N_HEADS, QK_NOPE), F32), _rot_half_cols(wq[..., QK_NOPE:]), zq],
                          axis=-1).reshape(Q_RANK, N_HEADS * HEAD_PAD)
    wkv = w_ukv.reshape(KV_RANK, N_HEADS, QK_NOPE + V_DIM)
    zh = jnp.zeros((KV_RANK, N_HEADS, HEAD_PAD - QK_NOPE), F32)
    wk = jnp.concatenate([wkv[..., :QK_NOPE], zh], axis=-1).reshape(KV_RANK, N_HEADS * HEAD_PAD)
    wv_h = wkv[..., QK_NOPE:].reshape(KV_RANK, N_PAIRS, 2, V_DIM)
    zv = jnp.zeros((KV_RANK, N_PAIRS, V_DIM), F32)
    wv = jnp.stack([jnp.concatenate([wv_h[:, :, 0], zv], axis=-1),
                    jnp.concatenate([zv, wv_h[:, :, 1]], axis=-1)], axis=2).reshape(KV_RANK, N_HEADS * HEAD_PAD)
    bf = lambda a: a.astype(BF16)
    return dict(
        n1=row(norm1_g), wa=bf(wa), wu=bf(wu), qg=row(q_norm_g), wqa=bf(wqa), wqb=bf(wqb),
        kvg=row(kv_norm_g), wk=bf(wk), wv=bf(wv),
        dsk=row(d_skip), wglu=bf(w_glu), bglu=row(b_glu), sg=row(ssm_out_g),
        ag=row(attn_out_g), wo=bf(w_out), n2=row(norm2_g), w1=bf(w_mlp1), w2=bf(w_mlp2), fg=row(final_g))


def _pack_ssm(lam_re, lam_im, log_dt, b_re, b_im, c_re, c_im, seg_len):
    cmul = lambda xr, xi, yr, yi: (xr * yr - xi * yi, xr * yi + xi * yr)
    lam_re, lam_im = lam_re.astype(F32), lam_im.astype(F32)
    dt = jnp.exp(log_dt.astype(F32))[..., None]
    mag = jnp.exp(lam_re * dt)
    a_re, a_im = mag * jnp.cos(lam_im * dt), mag * jnp.sin(lam_im * dt)
    den = lam_re * lam_re + lam_im * lam_im
    k_re = ((a_re - 1.0) * lam_re + a_im * lam_im) / den
    k_im = (a_im * lam_re - (a_re - 1.0) * lam_im) / den
    bb_re, bb_im = cmul(k_re[..., None], k_im[..., None], b_re.astype(F32), b_im.astype(F32))
    p_re, p_im = a_re, a_im
    s_re, s_im = jnp.ones_like(a_re), jnp.zeros_like(a_re)
    n = seg_len
    while n:
        if n & 1:
            s_re, s_im = cmul(s_re, s_im, p_re, p_im)
        p_re, p_im = cmul(p_re, p_im, p_re, p_im)
        n >>= 1
    eye = jnp.eye(GROUPS_PER_BLOCK, dtype=F32)
    nq, gb = SSM_LANE_BLOCKS, GROUPS_PER_BLOCK

    def b_block(part):
        p = part.reshape(2, nq, gb, SSM_STATE, SSM_GROUP)
        return jnp.einsum("dqgnh,gk->dqghkn", p, eye).reshape(2, nq, LANES, STATES_PER_BLOCK)

    def c_block(part):
        p = part.reshape(2, nq, gb, SSM_GROUP, SSM_STATE)
        return jnp.einsum("dqghn,gk->dqgnkh", p, eye).reshape(2, nq, STATES_PER_BLOCK, LANES)

    def a_block(zr, zi):
        shape = (2, nq, 1, STATES_PER_BLOCK)
        return jnp.concatenate([zr.reshape(shape), zi.reshape(shape)], axis=-1)

    bmat = jnp.concatenate([b_block(bb_re), b_block(bb_im)], axis=-1).astype(BF16)
    cmat = jnp.concatenate([c_block(c_re.astype(F32)), -c_block(c_im.astype(F32))], axis=-2).astype(BF16)
    return bmat, cmat, a_block(a_re, a_im), a_block(s_re, s_im)


def _perm_matrix(tt):
    rows = N_GROUPS * tt
    dst = np.arange(rows)
    src = (dst % N_GROUPS) * tt + dst // N_GROUPS
    p = np.zeros((rows, rows), np.float32)
    p[dst, src] = 1.0
    return jnp.asarray(p, BF16)


def _trunk(x, positions, segmented, w, ssm):
    nb, lr, _ = x.shape
    bmat, cmat, a_blk, a_seg = ssm
    cos_t, sin_t = _rope_tables(positions)
    perm = _perm_matrix(PRE_TT)
    q, kt, v, u = _pre_call(x, cos_t, sin_t, w, perm)
    if segmented:
        q5, kt5, v5 = q[None], kt[None], v[None]
    else:
        q5, kt5, v5 = q[:, None], kt[:, None], v[:, None]
    att = _attn_call(q5, kt5, v5).reshape(nb, lr, N_HEADS * V_DIM)
    init = jnp.zeros((2, SSM_LANE_BLOCKS, N_GROUPS, 2 * STATES_PER_BLOCK), F32)
    if segmented:
        (ends,) = _ssm_call(u, bmat, cmat, a_blk, init, with_y=False)
        init = _carry_call(ends, a_seg)
    yf, yb, _ = _ssm_call(u, bmat, cmat, a_blk, init, with_y=True)
    sn = _ssm_post_call(yf, yb, u, w, perm.T)
    return _post_call(x, att, sn, w)


def kernel(x_prompt, x_sample, norm1_g, w_in, q_norm_g, w_uq, kv_norm_g, w_ukv, lam_re, lam_im, log_dt,
           b_re, b_im, c_re, c_im, d_skip, w_glu, b_glu, attn_out_g, ssm_out_g, w_out, norm2_g, w_mlp1,
           w_mlp2, final_g):
    assert norm1_g.shape[0] == 1, "single-layer trunk"
    w = _pack_weights(norm1_g[0], w_in[0], q_norm_g[0], w_uq[0], kv_norm_g[0], w_ukv[0], d_skip[0], w_glu[0],
                      b_glu[0], attn_out_g[0], ssm_out_g[0], w_out[0], norm2_g[0], w_mlp1[0], w_mlp2[0], final_g)
    bp, lp, _ = x_prompt.shape
    bs, ls, _ = x_sample.shape
    assert bp == N_GROUPS and bs == 1 and ls % N_GROUPS == 0
    seg = ls // N_GROUPS
    ssm = _pack_ssm(lam_re[0], lam_im[0], log_dt[0], b_re[0], b_im[0], c_re[0], c_im[0], seg)
    y_prompt = _trunk(x_prompt, jnp.arange(lp)[None], False, w, ssm)
    y_sample = _trunk(x_sample.reshape(N_GROUPS, seg, D_MODEL), jnp.arange(ls).reshape(N_GROUPS, seg), True, w, ssm)
    return y_prompt, y_sample.reshape(bs, ls, D_MODEL)
```

```python
import functools
import math

import jax
import jax.numpy as jnp
import numpy as np
from jax import lax
from jax.experimental import pallas as pl
from jax.experimental.pallas import tpu as pltpu

F32 = jnp.float32
BF16 = jnp.bfloat16

D_MODEL = 1024
N_HEADS = 8
QK_NOPE = 64
QK_ROPE = 32
V_DIM = 64
Q_RANK = 256
KV_RANK = 128
SSM_W = 512
SSM_GROUP = 16
SSM_GROUPS = 32
SSM_STATE = 64
D_FF = 4096
EPS = 1e-6
ROPE_THETA = 10000.0

LANES = 128
SUBLANES = 8
VMEM_LIMIT_CAP = 60000 * 1024

N_GROUPS = SUBLANES
HEAD_PAD = LANES
N_PAIRS = N_HEADS // 2
SSM_LANE_BLOCKS = SSM_W // LANES
GROUPS_PER_BLOCK = LANES // SSM_GROUP
STATES_PER_BLOCK = GROUPS_PER_BLOCK * SSM_STATE

PRE_TT = 128
PERM_TT = 32
ATT_BQ = 512
ATT_BK = 512
ATT_RB = 32
Q_SCALE = math.log2(math.e) / math.sqrt(QK_NOPE + QK_ROPE)
SSM_TT = 128
POST_TM = 512
FF_CHUNK = 1024


def _vmem_limit(nbytes):
    return int(min(VMEM_LIMIT_CAP, nbytes))


def _rms(x, g):
    return x * lax.rsqrt(jnp.mean(x * x, axis=-1, keepdims=True) + EPS) * g


def _dot(a, b):
    return jnp.dot(a, b, preferred_element_type=F32)


def _pre_kernel(x_ref, cos_ref, sin_ref, n1_ref, wa_ref, wu_ref, qg_ref, wqa_ref, wqb_ref,
                kvg_ref, wk_ref, wv_ref, perm_ref, q_ref, kt_ref, v_ref, u_ref, *, tt):
    rows = N_GROUPS * tt
    x = x_ref[...].reshape(rows, D_MODEL)
    h = _rms(x, n1_ref[...]).astype(BF16)
    pa = _dot(h, wa_ref[...])
    u = _dot(h, wu_ref[...])
    cqn = _rms(pa[:, :Q_RANK], qg_ref[...]).astype(BF16)
    ckvn = _rms(pa[:, Q_RANK:Q_RANK + KV_RANK], kvg_ref[...]).astype(BF16)
    cos = jnp.broadcast_to(cos_ref[...], (N_GROUPS, tt, HEAD_PAD)).reshape(rows, HEAD_PAD)
    sin = jnp.broadcast_to(sin_ref[...], (N_GROUPS, tt, HEAD_PAD)).reshape(rows, HEAD_PAD)
    o = Q_RANK + KV_RANK
    k_rope = pa[:, o:o + HEAD_PAD] * cos + pa[:, o + HEAD_PAD:o + 2 * HEAD_PAD] * sin
    qa = _dot(cqn, wqa_ref[...])
    qb = _dot(cqn, wqb_ref[...])
    k = _dot(ckvn, wk_ref[...])
    v = _dot(ckvn, wv_ref[...])
    ones_col = (lax.broadcasted_iota(jnp.int32, (1, HEAD_PAD), 1) == V_DIM).astype(F32)
    for hd in range(N_HEADS):
        sl = slice(hd * HEAD_PAD, (hd + 1) * HEAD_PAD)
        qh = (qa[:, sl] * cos + qb[:, sl] * sin) * Q_SCALE
        q_ref[:, hd] = qh.astype(BF16).reshape(N_GROUPS, tt, HEAD_PAD)
        kht = (k[:, sl] + k_rope).T
        for b in range(N_GROUPS):
            kt_ref[b, hd, 0] = kht[:, b * tt:(b + 1) * tt].astype(BF16)
        v_ref[:, hd] = (v[:, sl] + ones_col).astype(BF16).reshape(N_GROUPS, tt, HEAD_PAD)
    u_hi = u.astype(BF16)
    u_lo = (u - u_hi.astype(F32)).astype(BF16)
    perm = perm_ref[...]
    sub = N_GROUPS * PERM_TT
    for s in range(tt // PERM_TT):
        pick = lambda a: jnp.concatenate(
            [a[b * tt + s * PERM_TT:b * tt + (s + 1) * PERM_TT] for b in range(N_GROUPS)], axis=0)
        u_ref[s * sub:(s + 1) * sub, :] = _dot(perm, pick(u_hi)) + _dot(perm, pick(u_lo))


def _pre_call(x, cos_t, sin_t, w, perm):
    nb, lr, _ = x.shape
    tt = PRE_TT
    rows = nb * tt
    n_t = lr // tt
    tab_nb = cos_t.shape[0]
    kt_sub = ATT_BK // tt
    full = lambda a: pl.BlockSpec(a.shape, lambda j: (0,) * a.ndim)
    weights = [w["n1"], w["wa"], w["wu"], w["qg"], w["wqa"], w["wqb"], w["kvg"], w["wk"], w["wv"], perm]
    in_specs = [
        pl.BlockSpec((nb, tt, D_MODEL), lambda j: (0, j, 0)),
        pl.BlockSpec((tab_nb, tt, HEAD_PAD), lambda j: (0, j, 0)),
        pl.BlockSpec((tab_nb, tt, HEAD_PAD), lambda j: (0, j, 0)),
    ] + [full(a) for a in weights]
    out_shape = [
        jax.ShapeDtypeStruct((nb, N_HEADS, lr, HEAD_PAD), BF16),
        jax.ShapeDtypeStruct((nb, N_HEADS, lr // ATT_BK, HEAD_PAD, ATT_BK), BF16),
        jax.ShapeDtypeStruct((nb, N_HEADS, lr, HEAD_PAD), BF16),
        jax.ShapeDtypeStruct((lr * nb, SSM_W), F32),
    ]
    out_specs = [
        pl.BlockSpec((nb, N_HEADS, tt, HEAD_PAD), lambda j: (0, 0, j, 0)),
        pl.BlockSpec((nb, N_HEADS, 1, HEAD_PAD, tt), lambda j: (0, 0, j // kt_sub, 0, j % kt_sub)),
        pl.BlockSpec((nb, N_HEADS, tt, HEAD_PAD), lambda j: (0, 0, j, 0)),
        pl.BlockSpec((rows, SSM_W), lambda j: (j, 0)),
    ]
    return pl.pallas_call(
        functools.partial(_pre_kernel, tt=tt),
        grid=(n_t,),
        in_specs=in_specs,
        out_specs=out_specs,
        out_shape=out_shape,
        compiler_params=pltpu.CompilerParams(
            dimension_semantics=("arbitrary",), vmem_limit_bytes=_vmem_limit(56 << 20)),
        name="pre",
    )(x, cos_t, sin_t, *weights)


def _tree(op, xs):
    while len(xs) > 1:
        xs = [op(xs[i], xs[i + 1]) for i in range(0, len(xs) - 1, 2)] + ([xs[-1]] if len(xs) % 2 else [])
    return xs[0]


def _attn_kernel(q_ref, kt_ref, v_ref, o_ref, s_ref, p_ref, alpha_ref, m_ref, acc_ref, *, n_seg, n_kb):
    bq, bk = ATT_BQ, ATT_BK
    m_ref[...] = jnp.full(m_ref.shape, -0.5 * float(np.finfo(np.float32).max), F32)
    acc_ref[...] = jnp.zeros(acc_ref.shape, F32)

    n = n_seg * n_kb

    def scores(c, slot):
        for j in range(2):
            s_ref[slot, j] = _dot(q_ref[0, 0, j], kt_ref[0, c // n_kb, j, c % n_kb])

    def softmax(slot):
        for j in range(2):
            for rb in range(bq // ATT_RB):
                rows = slice(rb * ATT_RB, (rb + 1) * ATT_RB)
                s = s_ref[slot, j, rows, :]
                tiles = [s[:, t * LANES:(t + 1) * LANES] for t in range(bk // LANES)]
                m_old = m_ref[j, rows, :]
                m_new = jnp.maximum(m_old, jnp.max(_tree(jnp.maximum, tiles), axis=-1, keepdims=True))
                alpha = jnp.exp2(m_old - m_new)
                ps = [jnp.exp2(t - m_new) for t in tiles]
                m_ref[j, rows, :] = m_new
                alpha_ref[slot, j, rows, :] = alpha
                p_ref[slot, j, rows, :] = jnp.concatenate(ps, axis=1).astype(BF16)

    def weighted_values(c, slot):
        r0 = (c % n_kb) * bk
        if not isinstance(c, int):
            r0 = pl.multiple_of(r0, bk)
        for j in range(2):
            pv = _dot(p_ref[slot, j], v_ref[0, c // n_kb, j, pl.ds(r0, bk), :])
            acc_ref[j] = acc_ref[j] * alpha_ref[slot, j] + pv

    lag = 1
    unroll = 6

    def stage(t, parity, do_scores=True, do_softmax=True, do_values=True):
        if do_values:
            weighted_values(t - 2 * lag, parity)
        if do_softmax:
            softmax((parity + lag) % 2)
        if do_scores:
            scores(t, parity)

    for t in range(2 * lag):
        stage(t, t % 2, t < n, lag <= t < n + lag, False)
    n_iter = max(n - 2 * lag, 0) // unroll
    if n_iter:
        def body(k, _):
            for i in range(unroll):
                stage(2 * lag + unroll * k + i, i % 2)
            return 0
        lax.fori_loop(0, n_iter, body, 0)
    for t in range(2 * lag + unroll * n_iter, n + 2 * lag):
        stage(t, t % 2, t < n, t < n + lag, True)
    outs = [acc_ref[j] / acc_ref[j][:, V_DIM:V_DIM + 1] for j in range(2)]
    low = lax.broadcasted_iota(jnp.int32, (1, HEAD_PAD), 1) < V_DIM
    o_ref[0] = jnp.where(low, outs[0], pltpu.roll(outs[1], V_DIM, 1))


def _attn_call(q, kt, v):
    n_seq, n_seg, _, lr, _ = q.shape
    n_kb = lr // ATT_BK
    n_qb = lr // ATT_BQ
    resident = dict(pipeline_mode=pl.Buffered(1)) if n_seg > 1 else {}
    scratch = [
        pltpu.VMEM((2, 2, ATT_BQ, ATT_BK), F32),
        pltpu.VMEM((2, 2, ATT_BQ, ATT_BK), BF16),
        pltpu.VMEM((2, 2, ATT_BQ, LANES), F32),
        pltpu.VMEM((2, ATT_BQ, LANES), F32),
        pltpu.VMEM((2, ATT_BQ, HEAD_PAD), F32),
    ]
    in_specs = [
        pl.BlockSpec((1, 1, 2, ATT_BQ, HEAD_PAD), lambda s, p, i: (s, i // n_qb, p, i % n_qb, 0)),
        pl.BlockSpec((1, n_seg, 2, n_kb, HEAD_PAD, ATT_BK), lambda s, p, i: (s, 0, p, 0, 0, 0), **resident),
        pl.BlockSpec((1, n_seg, 2, lr, HEAD_PAD), lambda s, p, i: (s, 0, p, 0, 0), **resident),
    ]
    kv_bytes = 2 * (2 * n_seg * lr * HEAD_PAD * 2) * (1 if n_seg > 1 else 2)
    return pl.pallas_call(
        functools.partial(_attn_kernel, n_seg=n_seg, n_kb=n_kb),
        grid=(n_seq, N_PAIRS, n_seg * n_qb),
        in_specs=in_specs,
        out_specs=pl.BlockSpec((1, ATT_BQ, HEAD_PAD), lambda s, p, i: (s, i, p)),
        out_shape=jax.ShapeDtypeStruct((n_seq, n_seg * lr, N_HEADS * V_DIM), F32),
        scratch_shapes=scratch,
        compiler_params=pltpu.CompilerParams(
            dimension_semantics=("arbitrary", "arbitrary", "arbitrary"),
            vmem_limit_bytes=_vmem_limit(kv_bytes + (24 << 20))),
        name="attn",
    )(q, kt, v)


def _cstep(xre, xim, are, aim, bre, bim):
    return are * xre - aim * xim + bre, are * xim + aim * xre + bim


def _ssm_kernel(uf_ref, ub_ref, bmat_ref, cmat_ref, a_ref, init_ref, *rest, tt, with_y):
    if with_y:
        yf_ref, yb_ref, fin_ref, state_ref, bu_ref, xs_ref = rest
    else:
        fin_ref, state_ref, bu_ref = rest
    j = pl.program_id(0)
    rows = N_GROUPS * tt
    n_pairs = tt // 2
    S = STATES_PER_BLOCK

    @pl.when(j == 0)
    def _():
        state_ref[...] = init_ref[...]

    for q in range(SSM_LANE_BLOCKS):
        lanes = slice(q * LANES, (q + 1) * LANES)
        bu_ref[0] = _dot(uf_ref[:, lanes].astype(BF16), bmat_ref[0, q])
        bu_ref[1] = _dot(ub_ref[:, lanes].astype(BF16), bmat_ref[1, q])
        a = [jnp.broadcast_to(a_ref[d, q], (N_GROUPS, 2 * S)) for d in range(2)]
        are = [a[d][:, :S] for d in range(2)]
        aim = [a[d][:, S:] for d in range(2)]

        def step(k, carry):
            fre, fim, bre, bim = carry
            r0 = pl.multiple_of(k * 2 * N_GROUPS, 2 * N_GROUPS)
            blk = bu_ref[0, pl.ds(r0, 2 * N_GROUPS), :]
            f1 = _cstep(fre, fim, are[0], aim[0], blk[:N_GROUPS, :S], blk[:N_GROUPS, S:])
            f2 = _cstep(f1[0], f1[1], are[0], aim[0], blk[N_GROUPS:, :S], blk[N_GROUPS:, S:])
            r1 = pl.multiple_of(rows - (k + 1) * 2 * N_GROUPS, 2 * N_GROUPS)
            blk = bu_ref[1, pl.ds(r1, 2 * N_GROUPS), :]
            b1 = _cstep(bre, bim, are[1], aim[1], blk[N_GROUPS:, :S], blk[N_GROUPS:, S:])
            b2 = _cstep(b1[0], b1[1], are[1], aim[1], blk[:N_GROUPS, :S], blk[:N_GROUPS, S:])
            if with_y:
                xs_ref[0, pl.ds(r0, 2 * N_GROUPS), :] = jnp.concatenate(
                    [jnp.concatenate(f1, axis=1), jnp.concatenate(f2, axis=1)], axis=0).astype(BF16)
                xs_ref[1, pl.ds(r1, 2 * N_GROUPS), :] = jnp.concatenate(
                    [jnp.concatenate(b2, axis=1), jnp.concatenate(b1, axis=1)], axis=0).astype(BF16)
            return f2[0], f2[1], b2[0], b2[1]

        init = (state_ref[0, q, :, :S], state_ref[0, q, :, S:], state_ref[1, q, :, :S], state_ref[1, q, :, S:])
        fre, fim, bre, bim = lax.fori_loop(0, n_pairs, step, init, unroll=2)
        state_ref[0, q] = jnp.concatenate([fre, fim], axis=1)
        state_ref[1, q] = jnp.concatenate([bre, bim], axis=1)
        if with_y:
            yf_ref[:, lanes] = _dot(xs_ref[0], cmat_ref[0, q])
            yb_ref[:, lanes] = _dot(xs_ref[1], cmat_ref[1, q])

    @pl.when(j == pl.num_programs(0) - 1)
    def _():
        fin_ref[...] = state_ref[...]


def _ssm_call(u, bmat, cmat, a, init, with_y):
    n_rows = u.shape[0]
    tt = SSM_TT
    rows = N_GROUPS * tt
    n_t = n_rows // rows
    full = lambda arr: pl.BlockSpec(arr.shape, lambda j: (0,) * arr.ndim)
    in_specs = [
        pl.BlockSpec((rows, SSM_W), lambda j: (j, 0)),
        pl.BlockSpec((rows, SSM_W), lambda j: (n_t - 1 - j, 0)),
        full(bmat), full(cmat), full(a), full(init),
    ]
    state_shape = (2, SSM_LANE_BLOCKS, N_GROUPS, 2 * STATES_PER_BLOCK)
    out_shape = [jax.ShapeDtypeStruct(state_shape, F32)]
    out_specs = [pl.BlockSpec(state_shape, lambda j: (0, 0, 0, 0))]
    scratch = [pltpu.VMEM(state_shape, F32), pltpu.VMEM((2, rows, 2 * STATES_PER_BLOCK), F32)]
    if with_y:
        out_shape = [jax.ShapeDtypeStruct((n_rows, SSM_W), F32)] * 2 + out_shape
        out_specs = [pl.BlockSpec((rows, SSM_W), lambda j: (j, 0)),
                     pl.BlockSpec((rows, SSM_W), lambda j: (n_t - 1 - j, 0))] + out_specs
        scratch.append(pltpu.VMEM((2, rows, 2 * STATES_PER_BLOCK), BF16))
    return pl.pallas_call(
        functools.partial(_ssm_kernel, tt=tt, with_y=with_y),
        grid=(n_t,),
        in_specs=in_specs,
        out_specs=out_specs,
        out_shape=out_shape,
        scratch_shapes=scratch,
        compiler_params=pltpu.CompilerParams(
            dimension_semantics=("arbitrary",), vmem_limit_bytes=_vmem_limit(48 << 20)),
        name="ssm" if with_y else "ssm_ends",
    )(u, u, bmat, cmat, a, init)


def _carry_kernel(e_ref, as_ref, i_ref):
    S = STATES_PER_BLOCK
    row = lax.broadcasted_iota(jnp.int32, (N_GROUPS, S), 0)
    for d in range(2):
        first = 0 if d == 0 else N_GROUPS - 1
        shift = 1 if d == 0 else N_GROUPS - 1
        for q in range(SSM_LANE_BLOCKS):
            e = e_ref[d, q]
            a = jnp.broadcast_to(as_ref[d, q], (N_GROUPS, 2 * S))
            ere, eim, are, aim = e[:, :S], e[:, S:], a[:, :S], a[:, S:]
            ire = jnp.zeros((N_GROUPS, S), F32)
            iim = jnp.zeros((N_GROUPS, S), F32)
            for _ in range(N_GROUPS - 1):
                tre, tim = _cstep(ire, iim, are, aim, ere, eim)
                ire = jnp.where(row == first, 0.0, pltpu.roll(tre, shift, 0))
                iim = jnp.where(row == first, 0.0, pltpu.roll(tim, shift, 0))
            i_ref[d, q] = jnp.concatenate([ire, iim], axis=1)


def _carry_call(ends, a_seg):
    return pl.pallas_call(
        _carry_kernel,
        out_shape=jax.ShapeDtypeStruct(ends.shape, F32),
        name="ssm_carry",
    )(ends, a_seg)


def _ssm_post_kernel(yf_ref, yb_ref, u_ref, dsk_ref, wglu_ref, bglu_ref, sg_ref, permt_ref, o_ref, *, tt):
    y = yf_ref[...] + yb_ref[...] + dsk_ref[...] * u_ref[...]
    g = jax.nn.gelu(y)
    z = _dot(g.astype(BF16), wglu_ref[...]) + bglu_ref[...]
    s = g * jax.nn.sigmoid(z)
    sn = _rms(s, sg_ref[...]).astype(BF16)
    permt = permt_ref[...]
    sub = N_GROUPS * PERM_TT
    for k in range(tt // PERM_TT):
        blk = _dot(permt, sn[k * sub:(k + 1) * sub]).astype(BF16)
        o_ref[:, k * PERM_TT:(k + 1) * PERM_TT, :] = blk.reshape(N_GROUPS, PERM_TT, SSM_W)


def _ssm_post_call(yf, yb, u, w, permt):
    n_rows = u.shape[0]
    tt = PRE_TT
    rows = N_GROUPS * tt
    n_t = n_rows // rows
    lr = n_rows // N_GROUPS
    full = lambda a: pl.BlockSpec(a.shape, lambda j: (0,) * a.ndim)
    row_spec = pl.BlockSpec((rows, SSM_W), lambda j: (j, 0))
    weights = [w["dsk"], w["wglu"], w["bglu"], w["sg"], permt]
    return pl.pallas_call(
        functools.partial(_ssm_post_kernel, tt=tt),
        grid=(n_t,),
        in_specs=[row_spec, row_spec, row_spec] + [full(a) for a in weights],
        out_specs=pl.BlockSpec((N_GROUPS, tt, SSM_W), lambda j: (0, j, 0)),
        out_shape=jax.ShapeDtypeStruct((N_GROUPS, lr, SSM_W), BF16),
        compiler_params=pltpu.CompilerParams(
            dimension_semantics=("arbitrary",), vmem_limit_bytes=_vmem_limit(40 << 20)),
        name="ssm_post",
    )(yf, yb, u, *weights)


def _post_kernel(x_ref, a_ref, sn_ref, ag_ref, wo_ref, n2_ref, w1_ref, w2_ref, fg_ref, o_ref):
    an = _rms(a_ref[0], ag_ref[...]).astype(BF16)
    mixed = jnp.concatenate([an, sn_ref[0]], axis=-1)
    x1 = x_ref[0] + _dot(mixed, wo_ref[...])
    h2 = _rms(x1, n2_ref[...]).astype(BF16)
    acc = jnp.zeros_like(x1)
    for c in range(D_FF // FF_CHUNK):
        hid = _dot(h2, w1_ref[:, c * FF_CHUNK:(c + 1) * FF_CHUNK])
        hid = jnp.square(jnp.maximum(hid, 0.0)).astype(BF16)
        acc = acc + _dot(hid, w2_ref[c * FF_CHUNK:(c + 1) * FF_CHUNK, :])
    o_ref[0] = _rms(x1 + acc, fg_ref[...])


def _post_call(x, a, sn, w):
    nb, lr, _ = x.shape
    tm = POST_TM
    const = lambda arr: pl.BlockSpec(arr.shape, lambda b, i: (0,) * arr.ndim, pipeline_mode=pl.Buffered(1))
    weights = [w["ag"], w["wo"], w["n2"], w["w1"], w["w2"], w["fg"]]
    return pl.pallas_call(
        _post_kernel,
        grid=(nb, lr // tm),
        in_specs=[
            pl.BlockSpec((1, tm, D_MODEL), lambda b, i: (b, i, 0)),
            pl.BlockSpec((1, tm, N_HEADS * V_DIM), lambda b, i: (b, i, 0)),
            pl.BlockSpec((1, tm, SSM_W), lambda b, i: (b, i, 0)),
        ] + [const(arr) for arr in weights],
        out_specs=pl.BlockSpec((1, tm, D_MODEL), lambda b, i: (b, i, 0)),
        out_shape=jax.ShapeDtypeStruct((nb, lr, D_MODEL), F32),
        compiler_params=pltpu.CompilerParams(
            dimension_semantics=("arbitrary", "arbitrary"), vmem_limit_bytes=_vmem_limit(48 << 20)),
        name="post",
    )(x, a, sn, *weights)


def _rope_tables(positions):
    inv = ROPE_THETA ** (-jnp.arange(0, QK_ROPE, 2, dtype=F32) / QK_ROPE)
    ang = positions.astype(F32)[..., None] * inv
    cos, sin = jnp.cos(ang), jnp.sin(ang)
    ones = jnp.ones(ang.shape[:-1] + (QK_NOPE,), F32)
    pad = jnp.zeros(ang.shape[:-1] + (HEAD_PAD - QK_NOPE - QK_ROPE,), F32)
    cos_t = jnp.concatenate([ones, cos, cos, pad], axis=-1)
    sin_t = jnp.concatenate([0.0 * ones, sin, sin, pad], axis=-1)
    return cos_t, sin_t


def _rot_half_cols(w):
    half = QK_ROPE // 2
    return jnp.concatenate([-w[..., half:], w[..., :half]], axis=-1)


def _pack_weights(norm1_g, w_in, q_norm_g, w_uq, kv_norm_g, w_ukv, d_skip, w_glu, b_glu,
                  attn_out_g, ssm_out_g, w_out, norm2_g, w_mlp1, w_mlp2, final_g):
    row = lambda g: g.reshape(1, -1).astype(F32)
    o = Q_RANK + KV_RANK
    w_kr = w_in[:, o:o + QK_ROPE]
    zk = jnp.zeros((D_MODEL, QK_NOPE), F32)
    zp = jnp.zeros((D_MODEL, HEAD_PAD - QK_NOPE - QK_ROPE), F32)
    wa = jnp.concatenate([w_in[:, :o], zk, w_kr, zp, zk, _rot_half_cols(w_kr), zp], axis=1)
    wu = w_in[:, o + QK_ROPE:]
    wq = w_uq.reshape(Q_RANK, N_HEADS, QK_NOPE + QK_ROPE)
    zq = jnp.zeros((Q_RANK, N_HEADS, HEAD_PAD - QK_NOPE - QK_ROPE), F32)
    wqa = jnp.concatenate([wq, zq], axis=-1).reshape(Q_RANK, N_HEADS * HEAD_PAD)
    wqb = jnp.concatenate([jnp.zeros((Q_RANK, N_HEADS, QK_NOPE), F32), _rot_half_cols(wq[..., QK_NOPE:]), zq],
                          axis=-1).reshape(Q_RANK, N_HEADS * HEAD_PAD)
    wkv = w_ukv.reshape(KV_RANK, N_HEADS, QK_NOPE + V_DIM)
    zh = jnp.zeros((KV_RANK, N_HEADS, HEAD_PAD - QK_NOPE), F32)
    wk = jnp.concatenate([wkv[..., :QK_NOPE], zh], axis=-1).reshape(KV_RANK, N_HEADS * HEAD_PAD)
    zv = jnp.zeros((KV_RANK, N_HEADS, HEAD_PAD - V_DIM), F32)
    wv = jnp.concatenate([wkv[..., QK_NOPE:], zv], axis=-1).reshape(KV_RANK, N_HEADS * HEAD_PAD)
    bf = lambda a: a.astype(BF16)
    return dict(
        n1=row(norm1_g), wa=bf(wa), wu=bf(wu), qg=row(q_norm_g), wqa=bf(wqa), wqb=bf(wqb),
        kvg=row(kv_norm_g), wk=bf(wk), wv=bf(wv),
        dsk=row(d_skip), wglu=bf(w_glu), bglu=row(b_glu), sg=row(ssm_out_g),
        ag=row(attn_out_g), wo=bf(w_out), n2=row(norm2_g), w1=bf(w_mlp1), w2=bf(w_mlp2), fg=row(final_g))


def _pack_ssm(lam_re, lam_im, log_dt, b_re, b_im, c_re, c_im, seg_len):
    cmul = lambda xr, xi, yr, yi: (xr * yr - xi * yi, xr * yi + xi * yr)
    lam_re, lam_im = lam_re.astype(F32), lam_im.astype(F32)
    dt = jnp.exp(log_dt.astype(F32))[..., None]
    mag = jnp.exp(lam_re * dt)
    a_re, a_im = mag * jnp.cos(lam_im * dt), mag * jnp.sin(lam_im * dt)
    den = lam_re * lam_re + lam_im * lam_im
    k_re = ((a_re - 1.0) * lam_re + a_im * lam_im) / den
    k_im = (a_im * lam_re - (a_re - 1.0) * lam_im) / den
    bb_re, bb_im = cmul(k_re[..., None], k_im[..., None], b_re.astype(F32), b_im.astype(F32))
    p_re, p_im = a_re, a_im
    s_re, s_im = jnp.ones_like(a_re), jnp.zeros_like(a_re)
    n = seg_len
    while n:
        if n & 1:
            s_re, s_im = cmul(s_re, s_im, p_re, p_im)
        p_re, p_im = cmul(p_re, p_im, p_re, p_im)
        n >>= 1
    eye = jnp.eye(GROUPS_PER_BLOCK, dtype=F32)
    nq, gb = SSM_LANE_BLOCKS, GROUPS_PER_BLOCK

    def b_block(part):
        p = part.reshape(2, nq, gb, SSM_STATE, SSM_GROUP)
        return jnp.einsum("dqgnh,gk->dqghkn", p, eye).reshape(2, nq, LANES, STATES_PER_BLOCK)

    def c_block(part):
        p = part.reshape(2, nq, gb, SSM_GROUP, SSM_STATE)
        return jnp.einsum("dqghn,gk->dqgnkh", p, eye).reshape(2, nq, STATES_PER_BLOCK, LANES)

    def a_block(zr, zi):
        shape = (2, nq, 1, STATES_PER_BLOCK)
        return jnp.concatenate([zr.reshape(shape), zi.reshape(shape)], axis=-1)

    bmat = jnp.concatenate([b_block(bb_re), b_block(bb_im)], axis=-1).astype(BF16)
    cmat = jnp.concatenate([c_block(c_re.astype(F32)), -c_block(c_im.astype(F32))], axis=-2).astype(BF16)
    return bmat, cmat, a_block(a_re, a_im), a_block(s_re, s_im)


def _perm_matrix(tt):
    rows = N_GROUPS * tt
    dst = np.arange(rows)
    src = (dst % N_GROUPS) * tt + dst // N_GROUPS
    p = np.zeros((rows, rows), np.float32)
    p[dst, src] = 1.0
    return jnp.asarray(p, BF16)


def _trunk(x, positions, segmented, w, ssm):
    nb, lr, _ = x.shape
    bmat, cmat, a_blk, a_seg = ssm
    cos_t, sin_t = _rope_tables(positions)
    perm = _perm_matrix(PERM_TT)
    q, kt, v, u = _pre_call(x, cos_t, sin_t, w, perm)
    if segmented:
        q5, kt5, v5 = q[None], kt[None], v[None]
    else:
        q5, kt5, v5 = q[:, None], kt[:, None], v[:, None]
    att = _attn_call(q5, kt5, v5).reshape(nb, lr, N_HEADS * V_DIM)
    init = jnp.zeros((2, SSM_LANE_BLOCKS, N_GROUPS, 2 * STATES_PER_BLOCK), F32)
    if segmented:
        (ends,) = _ssm_call(u, bmat, cmat, a_blk, init, with_y=False)
        init = _carry_call(ends, a_seg)
    yf, yb, _ = _ssm_call(u, bmat, cmat, a_blk, init, with_y=True)
    sn = _ssm_post_call(yf, yb, u, w, perm.T)
    return _post_call(x, att, sn, w)


def kernel(x_prompt, x_sample, norm1_g, w_in, q_norm_g, w_uq, kv_norm_g, w_ukv, lam_re, lam_im, log_dt,
           b_re, b_im, c_re, c_im, d_skip, w_glu, b_glu, attn_out_g, ssm_out_g, w_out, norm2_g, w_mlp1,
           w_mlp2, final_g):
    assert norm1_g.shape[0] == 1, "single-layer trunk"
    w = _pack_weights(norm1_g[0], w_in[0], q_norm_g[0], w_uq[0], kv_norm_g[0], w_ukv[0], d_skip[0], w_glu[0],
                      b_glu[0], attn_out_g[0], ssm_out_g[0], w_out[0], norm2_g[0], w_mlp1[0], w_mlp2[0], final_g)
    bp, lp, _ = x_prompt.shape
    bs, ls, _ = x_sample.shape
    assert bp == N_GROUPS and bs == 1 and ls % N_GROUPS == 0
    seg = ls // N_GROUPS
    ssm = _pack_ssm(lam_re[0], lam_im[0], log_dt[0], b_re[0], b_im[0], c_re[0], c_im[0], seg)
    y_prompt = _trunk(x_prompt, jnp.arange(lp)[None], False, w, ssm)
    y_sample = _trunk(x_sample.reshape(N_GROUPS, seg, D_MODEL), jnp.arange(ls).reshape(N_GROUPS, seg), True, w, ssm)
    return y_prompt, y_sample.reshape(bs, ls, D_MODEL)
```

```python
import functools
import math

import jax
import jax.numpy as jnp
import numpy as np
from jax import lax
from jax.experimental import pallas as pl
from jax.experimental.pallas import tpu as pltpu

F32 = jnp.float32
BF16 = jnp.bfloat16

D_MODEL = 1024
N_HEADS = 8
QK_NOPE = 64
QK_ROPE = 32
V_DIM = 64
Q_RANK = 256
KV_RANK = 128
SSM_W = 512
SSM_GROUP = 16
SSM_GROUPS = 32
SSM_STATE = 64
D_FF = 4096
EPS = 1e-6
ROPE_THETA = 10000.0

LANES = 128
SUBLANES = 8
VMEM_LIMIT_CAP = 60000 * 1024

N_GROUPS = SUBLANES
HEAD_PAD = LANES
N_PAIRS = N_HEADS // 2
SSM_LANE_BLOCKS = SSM_W // LANES
GROUPS_PER_BLOCK = LANES // SSM_GROUP
STATES_PER_BLOCK = GROUPS_PER_BLOCK * SSM_STATE

PRE_TT = 128
PERM_TT = 32
ATT_BQ = 512
ATT_BK = 512
ATT_RB = 32
ATT_MAX_UNROLL = 10
Q_SCALE = math.log2(math.e) / math.sqrt(QK_NOPE + QK_ROPE)
SSM_TT = 64
POST_TM = 512
FF_CHUNK = 1024


def _vmem_limit(nbytes):
    return int(min(VMEM_LIMIT_CAP, nbytes))


def _rms(x, g):
    return x * lax.rsqrt(jnp.mean(x * x, axis=-1, keepdims=True) + EPS) * g


def _dot(a, b):
    return jnp.dot(a, b, preferred_element_type=F32)


def _pre_kernel(x_ref, cos_ref, sin_ref, n1_ref, wa_ref, wu_ref, qg_ref, wqa_ref, wqb_ref,
                kvg_ref, wk_ref, wv_ref, perm_ref, q_ref, kt_ref, v_ref, u_ref, *, tt):
    rows = N_GROUPS * tt
    x = x_ref[...].reshape(rows, D_MODEL)
    h = _rms(x, n1_ref[...]).astype(BF16)
    pa = _dot(h, wa_ref[...])
    u = _dot(h, wu_ref[...])
    cqn = _rms(pa[:, :Q_RANK], qg_ref[...]).astype(BF16)
    ckvn = _rms(pa[:, Q_RANK:Q_RANK + KV_RANK], kvg_ref[...]).astype(BF16)
    cos = jnp.broadcast_to(cos_ref[...], (N_GROUPS, tt, HEAD_PAD)).reshape(rows, HEAD_PAD)
    sin = jnp.broadcast_to(sin_ref[...], (N_GROUPS, tt, HEAD_PAD)).reshape(rows, HEAD_PAD)
    o = Q_RANK + KV_RANK
    k_rope = pa[:, o:o + HEAD_PAD] * cos + pa[:, o + HEAD_PAD:o + 2 * HEAD_PAD] * sin
    qa = _dot(cqn, wqa_ref[...])
    qb = _dot(cqn, wqb_ref[...])
    k = _dot(ckvn, wk_ref[...])
    v = _dot(ckvn, wv_ref[...])
    ones_col = (lax.broadcasted_iota(jnp.int32, (1, HEAD_PAD), 1) == V_DIM).astype(F32)
    for hd in range(N_HEADS):
        sl = slice(hd * HEAD_PAD, (hd + 1) * HEAD_PAD)
        qh = (qa[:, sl] * cos + qb[:, sl] * sin) * Q_SCALE
        q_ref[:, hd] = qh.astype(BF16).reshape(N_GROUPS, tt, HEAD_PAD)
        kht = (k[:, sl] + k_rope).T
        for b in range(N_GROUPS):
            kt_ref[b, hd, 0] = kht[:, b * tt:(b + 1) * tt].astype(BF16)
        v_ref[:, hd] = (v[:, sl] + ones_col).astype(BF16).reshape(N_GROUPS, tt, HEAD_PAD)
    u_hi = u.astype(BF16)
    u_lo = (u - u_hi.astype(F32)).astype(BF16)
    perm = perm_ref[...]
    sub = N_GROUPS * PERM_TT
    for s in range(tt // PERM_TT):
        pick = lambda a: jnp.concatenate(
            [a[b * tt + s * PERM_TT:b * tt + (s + 1) * PERM_TT] for b in range(N_GROUPS)], axis=0)
        u_ref[s * sub:(s + 1) * sub, :] = _dot(perm, pick(u_hi)) + _dot(perm, pick(u_lo))


def _pre_call(x, cos_t, sin_t, w, perm):
    nb, lr, _ = x.shape
    tt = PRE_TT
    rows = nb * tt
    n_t = lr // tt
    tab_nb = cos_t.shape[0]
    kt_sub = ATT_BK // tt
    full = lambda a: pl.BlockSpec(a.shape, lambda j: (0,) * a.ndim)
    weights = [w["n1"], w["wa"], w["wu"], w["qg"], w["wqa"], w["wqb"], w["kvg"], w["wk"], w["wv"], perm]
    in_specs = [
        pl.BlockSpec((nb, tt, D_MODEL), lambda j: (0, j, 0)),
        pl.BlockSpec((tab_nb, tt, HEAD_PAD), lambda j: (0, j, 0)),
        pl.BlockSpec((tab_nb, tt, HEAD_PAD), lambda j: (0, j, 0)),
    ] + [full(a) for a in weights]
    out_shape = [
        jax.ShapeDtypeStruct((nb, N_HEADS, lr, HEAD_PAD), BF16),
        jax.ShapeDtypeStruct((nb, N_HEADS, lr // ATT_BK, HEAD_PAD, ATT_BK), BF16),
        jax.ShapeDtypeStruct((nb, N_HEADS, lr, HEAD_PAD), BF16),
        jax.ShapeDtypeStruct((lr * nb, SSM_W), F32),
    ]
    out_specs = [
        pl.BlockSpec((nb, N_HEADS, tt, HEAD_PAD), lambda j: (0, 0, j, 0)),
        pl.BlockSpec((nb, N_HEADS, 1, HEAD_PAD, tt), lambda j: (0, 0, j // kt_sub, 0, j % kt_sub)),
        pl.BlockSpec((nb, N_HEADS, tt, HEAD_PAD), lambda j: (0, 0, j, 0)),
        pl.BlockSpec((rows, SSM_W), lambda j: (j, 0)),
    ]
    return pl.pallas_call(
        functools.partial(_pre_kernel, tt=tt),
        grid=(n_t,),
        in_specs=in_specs,
        out_specs=out_specs,
        out_shape=out_shape,
        compiler_params=pltpu.CompilerParams(
            dimension_semantics=("arbitrary",), vmem_limit_bytes=_vmem_limit(56 << 20)),
        name="pre",
    )(x, cos_t, sin_t, *weights)


def _tree(op, xs):
    while len(xs) > 1:
        xs = [op(xs[i], xs[i + 1]) for i in range(0, len(xs) - 1, 2)] + ([xs[-1]] if len(xs) % 2 else [])
    return xs[0]


def _attn_kernel(q_ref, kt_ref, v_ref, o_ref, s_ref, p_ref, alpha_ref, m_ref, acc_ref, *, n_seg, n_kb):
    bq, bk = ATT_BQ, ATT_BK
    m_ref[...] = jnp.full(m_ref.shape, -0.5 * float(np.finfo(np.float32).max), F32)
    acc_ref[...] = jnp.zeros(acc_ref.shape, F32)

    n = n_seg * n_kb

    def scores(c, slot):
        for j in range(2):
            s_ref[slot, j] = _dot(q_ref[0, 0, j], kt_ref[0, c // n_kb, j, c % n_kb])

    def softmax(slot):
        for j in range(2):
            for rb in range(bq // ATT_RB):
                rows = slice(rb * ATT_RB, (rb + 1) * ATT_RB)
                s = s_ref[slot, j, rows, :]
                tiles = [s[:, t * LANES:(t + 1) * LANES] for t in range(bk // LANES)]
                m_old = m_ref[j, rows, :]
                m_new = jnp.maximum(m_old, jnp.max(_tree(jnp.maximum, tiles), axis=-1, keepdims=True))
                alpha = jnp.exp2(m_old - m_new)
                ps = [jnp.exp2(t - m_new) for t in tiles]
                m_ref[j, rows, :] = m_new
                alpha_ref[slot, j, rows, :] = alpha
                p_ref[slot, j, rows, :] = jnp.concatenate(ps, axis=1).astype(BF16)

    def weighted_values(c, slot):
        r0 = (c % n_kb) * bk
        if not isinstance(c, int):
            r0 = pl.multiple_of(r0, bk)
        for j in range(2):
            pv = _dot(p_ref[slot, j], v_ref[0, c // n_kb, j, pl.ds(r0, bk), :])
            acc_ref[j] = acc_ref[j] * alpha_ref[slot, j] + pv

    steady = max(n - 2, 0)
    unroll = max([u for u in range(2, ATT_MAX_UNROLL + 1, 2) if steady % u == 0], default=2)

    def stage(t, parity, do_scores=True, do_softmax=True, do_values=True):
        if do_values:
            weighted_values(t - 2, parity)
        if do_softmax:
            softmax(1 - parity)
        if do_scores:
            scores(t, parity)

    for t in range(2):
        stage(t, t % 2, t < n, 1 <= t <= n, False)
    n_iter = steady // unroll
    if n_iter:
        def body(k, _):
            for i in range(unroll):
                stage(2 + unroll * k + i, i % 2)
            return 0
        lax.fori_loop(0, n_iter, body, 0)
    for t in range(2 + unroll * n_iter, n + 2):
        stage(t, t % 2, t < n, t <= n, True)
    outs = [acc_ref[j] / acc_ref[j][:, V_DIM:V_DIM + 1] for j in range(2)]
    low = lax.broadcasted_iota(jnp.int32, (1, HEAD_PAD), 1) < V_DIM
    o_ref[0] = jnp.where(low, outs[0], pltpu.roll(outs[1], V_DIM, 1))


def _attn_call(q, kt, v):
    n_seq, n_seg, _, lr, _ = q.shape
    n_kb = lr // ATT_BK
    n_qb = lr // ATT_BQ
    resident = dict(pipeline_mode=pl.Buffered(1)) if n_seg > 1 else {}
    scratch = [
        pltpu.VMEM((2, 2, ATT_BQ, ATT_BK), F32),
        pltpu.VMEM((2, 2, ATT_BQ, ATT_BK), BF16),
        pltpu.VMEM((2, 2, ATT_BQ, LANES), F32),
        pltpu.VMEM((2, ATT_BQ, LANES), F32),
        pltpu.VMEM((2, ATT_BQ, HEAD_PAD), F32),
    ]
    in_specs = [
        pl.BlockSpec((1, 1, 2, ATT_BQ, HEAD_PAD), lambda s, p, i: (s, i // n_qb, p, i % n_qb, 0)),
        pl.BlockSpec((1, n_seg, 2, n_kb, HEAD_PAD, ATT_BK), lambda s, p, i: (s, 0, p, 0, 0, 0), **resident),
        pl.BlockSpec((1, n_seg, 2, lr, HEAD_PAD), lambda s, p, i: (s, 0, p, 0, 0), **resident),
    ]
    kv_bytes = 2 * (2 * n_seg * lr * HEAD_PAD * 2) * (1 if n_seg > 1 else 2)
    return pl.pallas_call(
        functools.partial(_attn_kernel, n_seg=n_seg, n_kb=n_kb),
        grid=(n_seq, N_PAIRS, n_seg * n_qb),
        in_specs=in_specs,
        out_specs=pl.BlockSpec((1, ATT_BQ, HEAD_PAD), lambda s, p, i: (s, i, p)),
        out_shape=jax.ShapeDtypeStruct((n_seq, n_seg * lr, N_HEADS * V_DIM), F32),
        scratch_shapes=scratch,
        compiler_params=pltpu.CompilerParams(
            dimension_semantics=("arbitrary", "arbitrary", "arbitrary"),
            vmem_limit_bytes=_vmem_limit(kv_bytes + (24 << 20))),
        name="attn",
    )(q, kt, v)


def _cstep(xre, xim, are, aim, bre, bim):
    return are * xre - aim * xim + bre, are * xim + aim * xre + bim


def _ssm_kernel(uf_ref, ub_ref, bmat_ref, cmat_ref, a_ref, init_ref, *rest, tt, with_y):
    if with_y:
        yf_ref, yb_ref, fin_ref, state_ref, bu_ref, xs_ref = rest
    else:
        fin_ref, state_ref, bu_ref = rest
    j = pl.program_id(0)
    rows = N_GROUPS * tt
    n_pairs = tt // 2
    S = STATES_PER_BLOCK

    @pl.when(j == 0)
    def _():
        state_ref[...] = init_ref[...]

    for q in range(SSM_LANE_BLOCKS):
        lanes = slice(q * LANES, (q + 1) * LANES)
        slot = q % 2
        bu_ref[slot, 0] = _dot(uf_ref[:, lanes].astype(BF16), bmat_ref[0, q])
        bu_ref[slot, 1] = _dot(ub_ref[:, lanes].astype(BF16), bmat_ref[1, q])
        a = [jnp.broadcast_to(a_ref[d, q], (N_GROUPS, 2 * S)) for d in range(2)]
        are = [a[d][:, :S] for d in range(2)]
        aim = [a[d][:, S:] for d in range(2)]
        fwd = (state_ref[0, q, :, :S], state_ref[0, q, :, S:])
        bwd = (state_ref[1, q, :, :S], state_ref[1, q, :, S:])
        for k in range(n_pairs):
            r0 = k * 2 * N_GROUPS
            blk = bu_ref[slot, 0, r0:r0 + 2 * N_GROUPS, :]
            f1 = _cstep(*fwd, are[0], aim[0], blk[:N_GROUPS, :S], blk[:N_GROUPS, S:])
            fwd = _cstep(*f1, are[0], aim[0], blk[N_GROUPS:, :S], blk[N_GROUPS:, S:])
            r1 = rows - (k + 1) * 2 * N_GROUPS
            blk = bu_ref[slot, 1, r1:r1 + 2 * N_GROUPS, :]
            b1 = _cstep(*bwd, are[1], aim[1], blk[N_GROUPS:, :S], blk[N_GROUPS:, S:])
            bwd = _cstep(*b1, are[1], aim[1], blk[:N_GROUPS, :S], blk[:N_GROUPS, S:])
            if with_y:
                xs_ref[slot, 0, r0:r0 + 2 * N_GROUPS, :] = jnp.concatenate(
                    [jnp.concatenate(f1, axis=1), jnp.concatenate(fwd, axis=1)], axis=0).astype(BF16)
                xs_ref[slot, 1, r1:r1 + 2 * N_GROUPS, :] = jnp.concatenate(
                    [jnp.concatenate(bwd, axis=1), jnp.concatenate(b1, axis=1)], axis=0).astype(BF16)
        state_ref[0, q] = jnp.concatenate(fwd, axis=1)
        state_ref[1, q] = jnp.concatenate(bwd, axis=1)
        if with_y:
            yf_ref[:, lanes] = _dot(xs_ref[slot, 0], cmat_ref[0, q])
            yb_ref[:, lanes] = _dot(xs_ref[slot, 1], cmat_ref[1, q])

    @pl.when(j == pl.num_programs(0) - 1)
    def _():
        fin_ref[...] = state_ref[...]


def _ssm_call(u, bmat, cmat, a, init, with_y):
    n_rows = u.shape[0]
    tt = SSM_TT
    rows = N_GROUPS * tt
    n_t = n_rows // rows
    full = lambda arr: pl.BlockSpec(arr.shape, lambda j: (0,) * arr.ndim)
    in_specs = [
        pl.BlockSpec((rows, SSM_W), lambda j: (j, 0)),
        pl.BlockSpec((rows, SSM_W), lambda j: (n_t - 1 - j, 0)),
        full(bmat), full(cmat), full(a), full(init),
    ]
    state_shape = (2, SSM_LANE_BLOCKS, N_GROUPS, 2 * STATES_PER_BLOCK)
    out_shape = [jax.ShapeDtypeStruct(state_shape, F32)]
    out_specs = [pl.BlockSpec(state_shape, lambda j: (0, 0, 0, 0))]
    scratch = [pltpu.VMEM(state_shape, F32), pltpu.VMEM((2, 2, rows, 2 * STATES_PER_BLOCK), F32)]
    if with_y:
        out_shape = [jax.ShapeDtypeStruct((n_rows, SSM_W), F32)] * 2 + out_shape
        out_specs = [pl.BlockSpec((rows, SSM_W), lambda j: (j, 0)),
                     pl.BlockSpec((rows, SSM_W), lambda j: (n_t - 1 - j, 0))] + out_specs
        scratch.append(pltpu.VMEM((2, 2, rows, 2 * STATES_PER_BLOCK), BF16))
    return pl.pallas_call(
        functools.partial(_ssm_kernel, tt=tt, with_y=with_y),
        grid=(n_t,),
        in_specs=in_specs,
        out_specs=out_specs,
        out_shape=out_shape,
        scratch_shapes=scratch,
        compiler_params=pltpu.CompilerParams(
            dimension_semantics=("arbitrary",), vmem_limit_bytes=_vmem_limit(48 << 20)),
        name="ssm" if with_y else "ssm_ends",
    )(u, u, bmat, cmat, a, init)


def _carry_kernel(e_ref, as_ref, i_ref):
    S = STATES_PER_BLOCK
    row = lax.broadcasted_iota(jnp.int32, (N_GROUPS, S), 0)
    for d in range(2):
        first = 0 if d == 0 else N_GROUPS - 1
        shift = 1 if d == 0 else N_GROUPS - 1
        for q in range(SSM_LANE_BLOCKS):
            e = e_ref[d, q]
            a = jnp.broadcast_to(as_ref[d, q], (N_GROUPS, 2 * S))
            ere, eim, are, aim = e[:, :S], e[:, S:], a[:, :S], a[:, S:]
            ire = jnp.zeros((N_GROUPS, S), F32)
            iim = jnp.zeros((N_GROUPS, S), F32)
            for _ in range(N_GROUPS - 1):
                tre, tim = _cstep(ire, iim, are, aim, ere, eim)
                ire = jnp.where(row == first, 0.0, pltpu.roll(tre, shift, 0))
                iim = jnp.where(row == first, 0.0, pltpu.roll(tim, shift, 0))
            i_ref[d, q] = jnp.concatenate([ire, iim], axis=1)


def _carry_call(ends, a_seg):
    return pl.pallas_call(
        _carry_kernel,
        out_shape=jax.ShapeDtypeStruct(ends.shape, F32),
        name="ssm_carry",
    )(ends, a_seg)


def _ssm_post_kernel(yf_ref, yb_ref, u_ref, dsk_ref, wglu_ref, bglu_ref, sg_ref, permt_ref, o_ref, *, tt):
    y = yf_ref[...] + yb_ref[...] + dsk_ref[...] * u_ref[...]
    g = jax.nn.gelu(y)
    z = _dot(g.astype(BF16), wglu_ref[...]) + bglu_ref[...]
    s = g * jax.nn.sigmoid(z)
    sn = _rms(s, sg_ref[...]).astype(BF16)
    permt = permt_ref[...]
    sub = N_GROUPS * PERM_TT
    for k in range(tt // PERM_TT):
        blk = _dot(permt, sn[k * sub:(k + 1) * sub]).astype(BF16)
        o_ref[:, k * PERM_TT:(k + 1) * PERM_TT, :] = blk.reshape(N_GROUPS, PERM_TT, SSM_W)


def _ssm_post_call(yf, yb, u, w, permt):
    n_rows = u.shape[0]
    tt = PRE_TT
    rows = N_GROUPS * tt
    n_t = n_rows // rows
    lr = n_rows // N_GROUPS
    full = lambda a: pl.BlockSpec(a.shape, lambda j: (0,) * a.ndim)
    row_spec = pl.BlockSpec((rows, SSM_W), lambda j: (j, 0))
    weights = [w["dsk"], w["wglu"], w["bglu"], w["sg"], permt]
    return pl.pallas_call(
        functools.partial(_ssm_post_kernel, tt=tt),
        grid=(n_t,),
        in_specs=[row_spec, row_spec, row_spec] + [full(a) for a in weights],
        out_specs=pl.BlockSpec((N_GROUPS, tt, SSM_W), lambda j: (0, j, 0)),
        out_shape=jax.ShapeDtypeStruct((N_GROUPS, lr, SSM_W), BF16),
        compiler_params=pltpu.CompilerParams(
            dimension_semantics=("arbitrary",), vmem_limit_bytes=_vmem_limit(40 << 20)),
        name="ssm_post",
    )(yf, yb, u, *weights)


def _post_kernel(x_ref, a_ref, sn_ref, ag_ref, wo_ref, n2_ref, w1_ref, w2_ref, fg_ref, o_ref):
    an = _rms(a_ref[0], ag_ref[...]).astype(BF16)
    mixed = jnp.concatenate([an, sn_ref[0]], axis=-1)
    x1 = x_ref[0] + _dot(mixed, wo_ref[...])
    h2 = _rms(x1, n2_ref[...]).astype(BF16)
    acc = jnp.zeros_like(x1)
    for c in range(D_FF // FF_CHUNK):
        hid = _dot(h2, w1_ref[:, c * FF_CHUNK:(c + 1) * FF_CHUNK])
        hid = jnp.square(jnp.maximum(hid, 0.0)).astype(BF16)
        acc = acc + _dot(hid, w2_ref[c * FF_CHUNK:(c + 1) * FF_CHUNK, :])
    o_ref[0] = _rms(x1 + acc, fg_ref[...])


def _post_call(x, a, sn, w):
    nb, lr, _ = x.shape
    tm = POST_TM
    const = lambda arr: pl.BlockSpec(arr.shape, lambda b, i: (0,) * arr.ndim, pipeline_mode=pl.Buffered(1))
    weights = [w["ag"], w["wo"], w["n2"], w["w1"], w["w2"], w["fg"]]
    return pl.pallas_call(
        _post_kernel,
        grid=(nb, lr // tm),
        in_specs=[
            pl.BlockSpec((1, tm, D_MODEL), lambda b, i: (b, i, 0)),
            pl.BlockSpec((1, tm, N_HEADS * V_DIM), lambda b, i: (b, i, 0)),
            pl.BlockSpec((1, tm, SSM_W), lambda b, i: (b, i, 0)),
        ] + [const(arr) for arr in weights],
        out_specs=pl.BlockSpec((1, tm, D_MODEL), lambda b, i: (b, i, 0)),
        out_shape=jax.ShapeDtypeStruct((nb, lr, D_MODEL), F32),
        compiler_params=pltpu.CompilerParams(
            dimension_semantics=("arbitrary", "arbitrary"), vmem_limit_bytes=_vmem_limit(48 << 20)),
        name="post",
    )(x, a, sn, *weights)


def _rope_tables(positions):
    inv = ROPE_THETA ** (-jnp.arange(0, QK_ROPE, 2, dtype=F32) / QK_ROPE)
    ang = positions.astype(F32)[..., None] * inv
    cos, sin = jnp.cos(ang), jnp.sin(ang)
    ones = jnp.ones(ang.shape[:-1] + (QK_NOPE,), F32)
    pad = jnp.zeros(ang.shape[:-1] + (HEAD_PAD - QK_NOPE - QK_ROPE,), F32)
    cos_t = jnp.concatenate([ones, cos, cos, pad], axis=-1)
    sin_t = jnp.concatenate([0.0 * ones, sin, sin, pad], axis=-1)
    return cos_t, sin_t


def _rot_half_cols(w):
    half = QK_ROPE // 2
    return jnp.concatenate([-w[..., half:], w[..., :half]], axis=-1)


def _pack_weights(norm1_g, w_in, q_norm_g, w_uq, kv_norm_g, w_ukv, d_skip, w_glu, b_glu,
                  attn_out_g, ssm_out_g, w_out, norm2_g, w_mlp1, w_mlp2, final_g):
    row = lambda g: g.reshape(1, -1).astype(F32)
    o = Q_RANK + KV_RANK
    w_kr = w_in[:, o:o + QK_ROPE]
    zk = jnp.zeros((D_MODEL, QK_NOPE), F32)
    zp = jnp.zeros((D_MODEL, HEAD_PAD - QK_NOPE - QK_ROPE), F32)
    wa = jnp.concatenate([w_in[:, :o], zk, w_kr, zp, zk, _rot_half_cols(w_kr), zp], axis=1)
    wu = w_in[:, o + QK_ROPE:]
    wq = w_uq.reshape(Q_RANK, N_HEADS, QK_NOPE + QK_ROPE)
    zq = jnp.zeros((Q_RANK, N_HEADS, HEAD_PAD - QK_NOPE - QK_ROPE), F32)
    wqa = jnp.concatenate([wq, zq], axis=-1).reshape(Q_RANK, N_HEADS * HEAD_PAD)
    wqb = jnp.concatenate([jnp.zeros((Q_RANK, N_HEADS, QK_NOPE), F32), _rot_half_cols(wq[..., QK_NOPE:]), zq],
                          axis=-1).reshape(Q_RANK, N_HEADS * HEAD_PAD)
    wkv = w_ukv.reshape(KV_RANK, N_HEADS, QK_NOPE + V_DIM)
    zh = jnp.zeros((KV_RANK, N_HEADS, HEAD_PAD - QK_NOPE), F32)
    wk = jnp.concatenate([wkv[..., :QK_NOPE], zh], axis=-1).reshape(KV_RANK, N_HEADS * HEAD_PAD)
    zv = jnp.zeros((KV_RANK, N_HEADS, HEAD_PAD - V_DIM), F32)
    wv = jnp.concatenate([wkv[..., QK_NOPE:], zv], axis=-1).reshape(KV_RANK, N_HEADS * HEAD_PAD)
    bf = lambda a: a.astype(BF16)
    return dict(
        n1=row(norm1_g), wa=bf(wa), wu=bf(wu), qg=row(q_norm_g), wqa=bf(wqa), wqb=bf(wqb),
        kvg=row(kv_norm_g), wk=bf(wk), wv=bf(wv),
        dsk=row(d_skip), wglu=bf(w_glu), bglu=row(b_glu), sg=row(ssm_out_g),
        ag=row(attn_out_g), wo=bf(w_out), n2=row(norm2_g), w1=bf(w_mlp1), w2=bf(w_mlp2), fg=row(final_g))


def _pack_ssm(lam_re, lam_im, log_dt, b_re, b_im, c_re, c_im, seg_len):
    cmul = lambda xr, xi, yr, yi: (xr * yr - xi * yi, xr * yi + xi * yr)
    lam_re, lam_im = lam_re.astype(F32), lam_im.astype(F32)
    dt = jnp.exp(log_dt.astype(F32))[..., None]
    mag = jnp.exp(lam_re * dt)
    a_re, a_im = mag * jnp.cos(lam_im * dt), mag * jnp.sin(lam_im * dt)
    den = lam_re * lam_re + lam_im * lam_im
    k_re = ((a_re - 1.0) * lam_re + a_im * lam_im) / den
    k_im = (a_im * lam_re - (a_re - 1.0) * lam_im) / den
    bb_re, bb_im = cmul(k_re[..., None], k_im[..., None], b_re.astype(F32), b_im.astype(F32))
    p_re, p_im = a_re, a_im
    s_re, s_im = jnp.ones_like(a_re), jnp.zeros_like(a_re)
    n = seg_len
    while n:
        if n & 1:
            s_re, s_im = cmul(s_re, s_im, p_re, p_im)
        p_re, p_im = cmul(p_re, p_im, p_re, p_im)
        n >>= 1
    eye = jnp.eye(GROUPS_PER_BLOCK, dtype=F32)
    nq, gb = SSM_LANE_BLOCKS, GROUPS_PER_BLOCK

    def b_block(part):
        p = part.reshape(2, nq, gb, SSM_STATE, SSM_GROUP)
        return jnp.einsum("dqgnh,gk->dqghkn", p, eye).reshape(2, nq, LANES, STATES_PER_BLOCK)

    def c_block(part):
        p = part.reshape(2, nq, gb, SSM_GROUP, SSM_STATE)
        return jnp.einsum("dqghn,gk->dqgnkh", p, eye).reshape(2, nq, STATES_PER_BLOCK, LANES)

    def a_block(zr, zi):
        shape = (2, nq, 1, STATES_PER_BLOCK)
        return jnp.concatenate([zr.reshape(shape), zi.reshape(shape)], axis=-1)

    bmat = jnp.concatenate([b_block(bb_re), b_block(bb_im)], axis=-1).astype(BF16)
    cmat = jnp.concatenate([c_block(c_re.astype(F32)), -c_block(c_im.astype(F32))], axis=-2).astype(BF16)
    return bmat, cmat, a_block(a_re, a_im), a_block(s_re, s_im)


def _perm_matrix(tt):
    rows = N_GROUPS * tt
    dst = np.arange(rows)
    src = (dst % N_GROUPS) * tt + dst // N_GROUPS
    p = np.zeros((rows, rows), np.float32)
    p[dst, src] = 1.0
    return jnp.asarray(p, BF16)


def _trunk(x, rope, segmented, w, ssm):
    nb, lr, _ = x.shape
    bmat, cmat, a_blk, a_seg = ssm
    cos_t, sin_t = rope
    perm = _perm_matrix(PERM_TT)
    q, kt, v, u = _pre_call(x, cos_t, sin_t, w, perm)
    if segmented:
        q5, kt5, v5 = q[None], kt[None], v[None]
    else:
        q5, kt5, v5 = q[:, None], kt[:, None], v[:, None]
    att = _attn_call(q5, kt5, v5).reshape(nb, lr, N_HEADS * V_DIM)
    init = jnp.zeros((2, SSM_LANE_BLOCKS, N_GROUPS, 2 * STATES_PER_BLOCK), F32)
    if segmented:
        (ends,) = _ssm_call(u, bmat, cmat, a_blk, init, with_y=False)
        init = _carry_call(ends, a_seg)
    yf, yb, _ = _ssm_call(u, bmat, cmat, a_blk, init, with_y=True)
    sn = _ssm_post_call(yf, yb, u, w, perm.T)
    return _post_call(x, att, sn, w)


def kernel(x_prompt, x_sample, norm1_g, w_in, q_norm_g, w_uq, kv_norm_g, w_ukv, lam_re, lam_im, log_dt,
           b_re, b_im, c_re, c_im, d_skip, w_glu, b_glu, attn_out_g, ssm_out_g, w_out, norm2_g, w_mlp1,
           w_mlp2, final_g):
    assert norm1_g.shape[0] == 1, "single-layer trunk"
    w = _pack_weights(norm1_g[0], w_in[0], q_norm_g[0], w_uq[0], kv_norm_g[0], w_ukv[0], d_skip[0], w_glu[0],
                      b_glu[0], attn_out_g[0], ssm_out_g[0], w_out[0], norm2_g[0], w_mlp1[0], w_mlp2[0], final_g)
    bp, lp, _ = x_prompt.shape
    bs, ls, _ = x_sample.shape
    assert bp == N_GROUPS and bs == 1 and ls % N_GROUPS == 0
    seg = ls // N_GROUPS
    ssm = _pack_ssm(lam_re[0], lam_im[0], log_dt[0], b_re[0], b_im[0], c_re[0], c_im[0], seg)
    cos_t, sin_t = _rope_tables(jnp.arange(max(lp, ls)))
    rope_p = (cos_t[None, :lp], sin_t[None, :lp])
    rope_s = (cos_t[:ls].reshape(N_GROUPS, seg, HEAD_PAD), sin_t[:ls].reshape(N_GROUPS, seg, HEAD_PAD))
    y_prompt = _trunk(x_prompt, rope_p, False, w, ssm)
    y_sample = _trunk(x_sample.reshape(N_GROUPS, seg, D_MODEL), rope_s, True, w, ssm)
    return y_prompt, y_sample.reshape(bs, ls, D_MODEL)
```

```python
import functools
import math

import jax
import jax.numpy as jnp
import numpy as np
from jax import lax
from jax.experimental import pallas as pl
from jax.experimental.pallas import tpu as pltpu

F32 = jnp.float32
BF16 = jnp.bfloat16

D_MODEL = 1024
N_HEADS = 8
QK_NOPE = 64
QK_ROPE = 32
V_DIM = 64
Q_RANK = 256
KV_RANK = 128
SSM_W = 512
SSM_GROUP = 16
SSM_GROUPS = 32
SSM_STATE = 64
D_FF = 4096
EPS = 1e-6
ROPE_THETA = 10000.0
ROPE_BLOCK = 128

LANES = 128
SUBLANES = 8
VMEM_LIMIT_CAP = 60000 * 1024

N_GROUPS = SUBLANES
HEAD_PAD = LANES
N_PAIRS = N_HEADS // 2
SSM_LANE_BLOCKS = SSM_W // LANES
GROUPS_PER_BLOCK = LANES // SSM_GROUP
STATES_PER_BLOCK = GROUPS_PER_BLOCK * SSM_STATE

PRE_TT = 128
PERM_TT = 32
ATT_BQ = 512
ATT_BK = 512
ATT_RB = 32
ATT_MAX_UNROLL = 10
Q_SCALE = math.log2(math.e) / math.sqrt(QK_NOPE + QK_ROPE)
SSM_TT = 64
POST_TM = 512
FF_CHUNK = 1024


def _vmem_limit(nbytes):
    return int(min(VMEM_LIMIT_CAP, nbytes))


def _rms(x, g):
    return x * lax.rsqrt(jnp.mean(x * x, axis=-1, keepdims=True) + EPS) * g


def _dot(a, b):
    return jnp.dot(a, b, preferred_element_type=F32)


def _pre_kernel(x_ref, cos_ref, sin_ref, n1_ref, wa_ref, wu_ref, qg_ref, wqa_ref, wqb_ref,
                kvg_ref, wk_ref, wv_ref, perm_ref, q_ref, kt_ref, v_ref, u_ref, *, tt):
    rows = N_GROUPS * tt
    x = x_ref[...].reshape(rows, D_MODEL)
    h = _rms(x, n1_ref[...]).astype(BF16)
    pa = _dot(h, wa_ref[...])
    u = _dot(h, wu_ref[...])
    cqn = _rms(pa[:, :Q_RANK], qg_ref[...]).astype(BF16)
    ckvn = _rms(pa[:, Q_RANK:Q_RANK + KV_RANK], kvg_ref[...]).astype(BF16)
    cos = jnp.broadcast_to(cos_ref[...], (N_GROUPS, tt, HEAD_PAD)).reshape(rows, HEAD_PAD)
    sin = jnp.broadcast_to(sin_ref[...], (N_GROUPS, tt, HEAD_PAD)).reshape(rows, HEAD_PAD)
    o = Q_RANK + KV_RANK
    k_rope = pa[:, o:o + HEAD_PAD] * cos + pa[:, o + HEAD_PAD:o + 2 * HEAD_PAD] * sin
    qa = _dot(cqn, wqa_ref[...])
    qb = _dot(cqn, wqb_ref[...])
    k = _dot(ckvn, wk_ref[...])
    v = _dot(ckvn, wv_ref[...])
    ones_col = (lax.broadcasted_iota(jnp.int32, (1, HEAD_PAD), 1) == V_DIM).astype(F32)
    for hd in range(N_HEADS):
        sl = slice(hd * HEAD_PAD, (hd + 1) * HEAD_PAD)
        qh = (qa[:, sl] * cos + qb[:, sl] * sin) * Q_SCALE
        q_ref[:, hd] = qh.astype(BF16).reshape(N_GROUPS, tt, HEAD_PAD)
        kht = (k[:, sl] + k_rope).T
        for b in range(N_GROUPS):
            kt_ref[b, hd, 0] = kht[:, b * tt:(b + 1) * tt].astype(BF16)
        v_ref[:, hd] = (v[:, sl] + ones_col).astype(BF16).reshape(N_GROUPS, tt, HEAD_PAD)
    u_hi = u.astype(BF16)
    u_lo = (u - u_hi.astype(F32)).astype(BF16)
    perm = perm_ref[...]
    sub = N_GROUPS * PERM_TT
    for s in range(tt // PERM_TT):
        pick = lambda a: jnp.concatenate(
            [a[b * tt + s * PERM_TT:b * tt + (s + 1) * PERM_TT] for b in range(N_GROUPS)], axis=0)
        u_ref[s * sub:(s + 1) * sub, :] = _dot(perm, pick(u_hi)) + _dot(perm, pick(u_lo))


def _pre_call(x, cos_t, sin_t, w, perm):
    nb, lr, _ = x.shape
    tt = PRE_TT
    rows = nb * tt
    n_t = lr // tt
    tab_nb = cos_t.shape[0]
    kt_sub = ATT_BK // tt
    full = lambda a: pl.BlockSpec(a.shape, lambda j: (0,) * a.ndim)
    weights = [w["n1"], w["wa"], w["wu"], w["qg"], w["wqa"], w["wqb"], w["kvg"], w["wk"], w["wv"], perm]
    in_specs = [
        pl.BlockSpec((nb, tt, D_MODEL), lambda j: (0, j, 0)),
        pl.BlockSpec((tab_nb, tt, HEAD_PAD), lambda j: (0, j, 0)),
        pl.BlockSpec((tab_nb, tt, HEAD_PAD), lambda j: (0, j, 0)),
    ] + [full(a) for a in weights]
    out_shape = [
        jax.ShapeDtypeStruct((nb, N_HEADS, lr, HEAD_PAD), BF16),
        jax.ShapeDtypeStruct((nb, N_HEADS, lr // ATT_BK, HEAD_PAD, ATT_BK), BF16),
        jax.ShapeDtypeStruct((nb, N_HEADS, lr, HEAD_PAD), BF16),
        jax.ShapeDtypeStruct((lr * nb, SSM_W), F32),
    ]
    out_specs = [
        pl.BlockSpec((nb, N_HEADS, tt, HEAD_PAD), lambda j: (0, 0, j, 0)),
        pl.BlockSpec((nb, N_HEADS, 1, HEAD_PAD, tt), lambda j: (0, 0, j // kt_sub, 0, j % kt_sub)),
        pl.BlockSpec((nb, N_HEADS, tt, HEAD_PAD), lambda j: (0, 0, j, 0)),
        pl.BlockSpec((rows, SSM_W), lambda j: (j, 0)),
    ]
    return pl.pallas_call(
        functools.partial(_pre_kernel, tt=tt),
        grid=(n_t,),
        in_specs=in_specs,
        out_specs=out_specs,
        out_shape=out_shape,
        compiler_params=pltpu.CompilerParams(
            dimension_semantics=("arbitrary",), vmem_limit_bytes=_vmem_limit(56 << 20)),
        name="pre",
    )(x, cos_t, sin_t, *weights)


def _tree(op, xs):
    while len(xs) > 1:
        xs = [op(xs[i], xs[i + 1]) for i in range(0, len(xs) - 1, 2)] + ([xs[-1]] if len(xs) % 2 else [])
    return xs[0]


def _attn_kernel(q_ref, kt_ref, v_ref, o_ref, s_ref, p_ref, alpha_ref, m_ref, acc_ref, *, n_seg, n_kb):
    bq, bk = ATT_BQ, ATT_BK
    m_ref[...] = jnp.full(m_ref.shape, -0.5 * float(np.finfo(np.float32).max), F32)
    acc_ref[...] = jnp.zeros(acc_ref.shape, F32)

    n = n_seg * n_kb

    def scores(c, slot):
        for j in range(2):
            s_ref[slot, j] = _dot(q_ref[0, 0, j], kt_ref[0, c // n_kb, j, c % n_kb])

    def softmax(slot):
        for j in range(2):
            for rb in range(bq // ATT_RB):
                rows = slice(rb * ATT_RB, (rb + 1) * ATT_RB)
                s = s_ref[slot, j, rows, :]
                tiles = [s[:, t * LANES:(t + 1) * LANES] for t in range(bk // LANES)]
                m_old = m_ref[j, rows, :]
                m_new = jnp.maximum(m_old, jnp.max(_tree(jnp.maximum, tiles), axis=-1, keepdims=True))
                alpha = jnp.exp2(m_old - m_new)
                ps = [jnp.exp2(t - m_new) for t in tiles]
                m_ref[j, rows, :] = m_new
                alpha_ref[slot, j, rows, :] = alpha
                p_ref[slot, j, rows, :] = jnp.concatenate(ps, axis=1).astype(BF16)

    def weighted_values(c, slot):
        r0 = (c % n_kb) * bk
        if not isinstance(c, int):
            r0 = pl.multiple_of(r0, bk)
        for j in range(2):
            pv = _dot(p_ref[slot, j], v_ref[0, c // n_kb, j, pl.ds(r0, bk), :])
            acc_ref[j] = acc_ref[j] * alpha_ref[slot, j] + pv

    steady = max(n - 2, 0)
    unroll = max([u for u in range(2, ATT_MAX_UNROLL + 1, 2) if steady % u == 0], default=2)

    def stage(t, parity, do_scores=True, do_softmax=True, do_values=True):
        if do_values:
            weighted_values(t - 2, parity)
        if do_softmax:
            softmax(1 - parity)
        if do_scores:
            scores(t, parity)

    for t in range(2):
        stage(t, t % 2, t < n, 1 <= t <= n, False)
    n_iter = steady // unroll
    if n_iter:
        def body(k, _):
            for i in range(unroll):
                stage(2 + unroll * k + i, i % 2)
            return 0
        lax.fori_loop(0, n_iter, body, 0)
    for t in range(2 + unroll * n_iter, n + 2):
        stage(t, t % 2, t < n, t <= n, True)
    outs = [acc_ref[j] / acc_ref[j][:, V_DIM:V_DIM + 1] for j in range(2)]
    low = lax.broadcasted_iota(jnp.int32, (1, HEAD_PAD), 1) < V_DIM
    o_ref[0] = jnp.where(low, outs[0], pltpu.roll(outs[1], V_DIM, 1))


def _attn_call(q, kt, v):
    n_seq, n_seg, _, lr, _ = q.shape
    n_kb = lr // ATT_BK
    n_qb = lr // ATT_BQ
    resident = dict(pipeline_mode=pl.Buffered(1)) if n_seg > 1 else {}
    scratch = [
        pltpu.VMEM((2, 2, ATT_BQ, ATT_BK), F32),
        pltpu.VMEM((2, 2, ATT_BQ, ATT_BK), BF16),
        pltpu.VMEM((2, 2, ATT_BQ, LANES), F32),
        pltpu.VMEM((2, ATT_BQ, LANES), F32),
        pltpu.VMEM((2, ATT_BQ, HEAD_PAD), F32),
    ]
    in_specs = [
        pl.BlockSpec((1, 1, 2, ATT_BQ, HEAD_PAD), lambda s, p, i: (s, i // n_qb, p, i % n_qb, 0)),
        pl.BlockSpec((1, n_seg, 2, n_kb, HEAD_PAD, ATT_BK), lambda s, p, i: (s, 0, p, 0, 0, 0), **resident),
        pl.BlockSpec((1, n_seg, 2, lr, HEAD_PAD), lambda s, p, i: (s, 0, p, 0, 0), **resident),
    ]
    kv_bytes = 2 * (2 * n_seg * lr * HEAD_PAD * 2) * (1 if n_seg > 1 else 2)
    return pl.pallas_call(
        functools.partial(_attn_kernel, n_seg=n_seg, n_kb=n_kb),
        grid=(n_seq, N_PAIRS, n_seg * n_qb),
        in_specs=in_specs,
        out_specs=pl.BlockSpec((1, ATT_BQ, HEAD_PAD), lambda s, p, i: (s, i, p)),
        out_shape=jax.ShapeDtypeStruct((n_seq, n_seg * lr, N_HEADS * V_DIM), F32),
        scratch_shapes=scratch,
        compiler_params=pltpu.CompilerParams(
            dimension_semantics=("arbitrary", "arbitrary", "arbitrary"),
            vmem_limit_bytes=_vmem_limit(kv_bytes + (24 << 20))),
        name="attn",
    )(q, kt, v)


def _cstep(xre, xim, are, aim, bre, bim):
    return are * xre - aim * xim + bre, are * xim + aim * xre + bim


def _ssm_kernel(uf_ref, ub_ref, bmat_ref, cmat_ref, a_ref, init_ref, *rest, tt, with_y):
    if with_y:
        yf_ref, yb_ref, fin_ref, state_ref, bu_ref, xs_ref = rest
    else:
        fin_ref, state_ref, bu_ref = rest
    j = pl.program_id(0)
    rows = N_GROUPS * tt
    n_pairs = tt // 2
    S = STATES_PER_BLOCK

    @pl.when(j == 0)
    def _():
        state_ref[...] = init_ref[...]

    for q in range(SSM_LANE_BLOCKS):
        lanes = slice(q * LANES, (q + 1) * LANES)
        slot = q % 2
        bu_ref[slot, 0] = _dot(uf_ref[:, lanes].astype(BF16), bmat_ref[0, q])
        bu_ref[slot, 1] = _dot(ub_ref[:, lanes].astype(BF16), bmat_ref[1, q])
        a = [jnp.broadcast_to(a_ref[d, q], (N_GROUPS, 2 * S)) for d in range(2)]
        are = [a[d][:, :S] for d in range(2)]
        aim = [a[d][:, S:] for d in range(2)]
        fwd = (state_ref[0, q, :, :S], state_ref[0, q, :, S:])
        bwd = (state_ref[1, q, :, :S], state_ref[1, q, :, S:])
        for k in range(n_pairs):
            r0 = k * 2 * N_GROUPS
            blk = bu_ref[slot, 0, r0:r0 + 2 * N_GROUPS, :]
            f1 = _cstep(*fwd, are[0], aim[0], blk[:N_GROUPS, :S], blk[:N_GROUPS, S:])
            fwd = _cstep(*f1, are[0], aim[0], blk[N_GROUPS:, :S], blk[N_GROUPS:, S:])
            r1 = rows - (k + 1) * 2 * N_GROUPS
            blk = bu_ref[slot, 1, r1:r1 + 2 * N_GROUPS, :]
            b1 = _cstep(*bwd, are[1], aim[1], blk[N_GROUPS:, :S], blk[N_GROUPS:, S:])
            bwd = _cstep(*b1, are[1], aim[1], blk[:N_GROUPS, :S], blk[:N_GROUPS, S:])
            if with_y:
                xs_ref[slot, 0, r0:r0 + 2 * N_GROUPS, :] = jnp.concatenate(
                    [jnp.concatenate(f1, axis=1), jnp.concatenate(fwd, axis=1)], axis=0).astype(BF16)
                xs_ref[slot, 1, r1:r1 + 2 * N_GROUPS, :] = jnp.concatenate(
                    [jnp.concatenate(bwd, axis=1), jnp.concatenate(b1, axis=1)], axis=0).astype(BF16)
        state_ref[0, q] = jnp.concatenate(fwd, axis=1)
        state_ref[1, q] = jnp.concatenate(bwd, axis=1)
        if with_y:
            yf_ref[:, lanes] = _dot(xs_ref[slot, 0], cmat_ref[0, q])
            yb_ref[:, lanes] = _dot(xs_ref[slot, 1], cmat_ref[1, q])

    @pl.when(j == pl.num_programs(0) - 1)
    def _():
        fin_ref[...] = state_ref[...]


def _ssm_call(u, bmat, cmat, a, init, with_y):
    n_rows = u.shape[0]
    tt = SSM_TT
    rows = N_GROUPS * tt
    n_t = n_rows // rows
    full = lambda arr: pl.BlockSpec(arr.shape, lambda j: (0,) * arr.ndim)
    in_specs = [
        pl.BlockSpec((rows, SSM_W), lambda j: (j, 0)),
        pl.BlockSpec((rows, SSM_W), lambda j: (n_t - 1 - j, 0)),
        full(bmat), full(cmat), full(a), full(init),
    ]
    state_shape = (2, SSM_LANE_BLOCKS, N_GROUPS, 2 * STATES_PER_BLOCK)
    out_shape = [jax.ShapeDtypeStruct(state_shape, F32)]
    out_specs = [pl.BlockSpec(state_shape, lambda j: (0, 0, 0, 0))]
    scratch = [pltpu.VMEM(state_shape, F32), pltpu.VMEM((2, 2, rows, 2 * STATES_PER_BLOCK), F32)]
    if with_y:
        out_shape = [jax.ShapeDtypeStruct((n_rows, SSM_W), F32)] * 2 + out_shape
        out_specs = [pl.BlockSpec((rows, SSM_W), lambda j: (j, 0)),
                     pl.BlockSpec((rows, SSM_W), lambda j: (n_t - 1 - j, 0))] + out_specs
        scratch.append(pltpu.VMEM((2, 2, rows, 2 * STATES_PER_BLOCK), BF16))
    return pl.pallas_call(
        functools.partial(_ssm_kernel, tt=tt, with_y=with_y),
        grid=(n_t,),
        in_specs=in_specs,
        out_specs=out_specs,
        out_shape=out_shape,
        scratch_shapes=scratch,
        compiler_params=pltpu.CompilerParams(
            dimension_semantics=("arbitrary",), vmem_limit_bytes=_vmem_limit(48 << 20)),
        name="ssm" if with_y else "ssm_ends",
    )(u, u, bmat, cmat, a, init)


def _carry_kernel(e_ref, as_ref, i_ref):
    S = STATES_PER_BLOCK
    row = lax.broadcasted_iota(jnp.int32, (N_GROUPS, S), 0)
    for d in range(2):
        first = 0 if d == 0 else N_GROUPS - 1
        shift = 1 if d == 0 else N_GROUPS - 1
        for q in range(SSM_LANE_BLOCKS):
            e = e_ref[d, q]
            a = jnp.broadcast_to(as_ref[d, q], (N_GROUPS, 2 * S))
            ere, eim, are, aim = e[:, :S], e[:, S:], a[:, :S], a[:, S:]
            ire = jnp.zeros((N_GROUPS, S), F32)
            iim = jnp.zeros((N_GROUPS, S), F32)
            for _ in range(N_GROUPS - 1):
                tre, tim = _cstep(ire, iim, are, aim, ere, eim)
                ire = jnp.where(row == first, 0.0, pltpu.roll(tre, shift, 0))
                iim = jnp.where(row == first, 0.0, pltpu.roll(tim, shift, 0))
            i_ref[d, q] = jnp.concatenate([ire, iim], axis=1)


def _carry_call(ends, a_seg):
    return pl.pallas_call(
        _carry_kernel,
        out_shape=jax.ShapeDtypeStruct(ends.shape, F32),
        name="ssm_carry",
    )(ends, a_seg)


def _ssm_post_kernel(yf_ref, yb_ref, u_ref, dsk_ref, wglu_ref, bglu_ref, sg_ref, permt_ref, o_ref, *, tt):
    y = yf_ref[...] + yb_ref[...] + dsk_ref[...] * u_ref[...]
    g = jax.nn.gelu(y)
    z = _dot(g.astype(BF16), wglu_ref[...]) + bglu_ref[...]
    s = g * jax.nn.sigmoid(z)
    sn = _rms(s, sg_ref[...]).astype(BF16)
    permt = permt_ref[...]
    sub = N_GROUPS * PERM_TT
    for k in range(tt // PERM_TT):
        blk = _dot(permt, sn[k * sub:(k + 1) * sub]).astype(BF16)
        o_ref[:, k * PERM_TT:(k + 1) * PERM_TT, :] = blk.reshape(N_GROUPS, PERM_TT, SSM_W)


def _ssm_post_call(yf, yb, u, w, permt):
    n_rows = u.shape[0]
    tt = PRE_TT
    rows = N_GROUPS * tt
    n_t = n_rows // rows
    lr = n_rows // N_GROUPS
    full = lambda a: pl.BlockSpec(a.shape, lambda j: (0,) * a.ndim)
    row_spec = pl.BlockSpec((rows, SSM_W), lambda j: (j, 0))
    weights = [w["dsk"], w["wglu"], w["bglu"], w["sg"], permt]
    return pl.pallas_call(
        functools.partial(_ssm_post_kernel, tt=tt),
        grid=(n_t,),
        in_specs=[row_spec, row_spec, row_spec] + [full(a) for a in weights],
        out_specs=pl.BlockSpec((N_GROUPS, tt, SSM_W), lambda j: (0, j, 0)),
        out_shape=jax.ShapeDtypeStruct((N_GROUPS, lr, SSM_W), BF16),
        compiler_params=pltpu.CompilerParams(
            dimension_semantics=("arbitrary",), vmem_limit_bytes=_vmem_limit(40 << 20)),
        name="ssm_post",
    )(yf, yb, u, *weights)


def _post_kernel(x_ref, a_ref, sn_ref, ag_ref, wo_ref, n2_ref, w1_ref, w2_ref, fg_ref, o_ref):
    an = _rms(a_ref[0], ag_ref[...]).astype(BF16)
    mixed = jnp.concatenate([an, sn_ref[0]], axis=-1)
    x1 = x_ref[0] + _dot(mixed, wo_ref[...])
    h2 = _rms(x1, n2_ref[...]).astype(BF16)
    acc = jnp.zeros_like(x1)
    for c in range(D_FF // FF_CHUNK):
        hid = _dot(h2, w1_ref[:, c * FF_CHUNK:(c + 1) * FF_CHUNK])
        hid = jnp.square(jnp.maximum(hid, 0.0)).astype(BF16)
        acc = acc + _dot(hid, w2_ref[c * FF_CHUNK:(c + 1) * FF_CHUNK, :])
    o_ref[0] = _rms(x1 + acc, fg_ref[...])


def _post_call(x, a, sn, w):
    nb, lr, _ = x.shape
    tm = POST_TM
    const = lambda arr: pl.BlockSpec(arr.shape, lambda b, i: (0,) * arr.ndim, pipeline_mode=pl.Buffered(1))
    weights = [w["ag"], w["wo"], w["n2"], w["w1"], w["w2"], w["fg"]]
    return pl.pallas_call(
        _post_kernel,
        grid=(nb, lr // tm),
        in_specs=[
            pl.BlockSpec((1, tm, D_MODEL), lambda b, i: (b, i, 0)),
            pl.BlockSpec((1, tm, N_HEADS * V_DIM), lambda b, i: (b, i, 0)),
            pl.BlockSpec((1, tm, SSM_W), lambda b, i: (b, i, 0)),
        ] + [const(arr) for arr in weights],
        out_specs=pl.BlockSpec((1, tm, D_MODEL), lambda b, i: (b, i, 0)),
        out_shape=jax.ShapeDtypeStruct((nb, lr, D_MODEL), F32),
        compiler_params=pltpu.CompilerParams(
            dimension_semantics=("arbitrary", "arbitrary"), vmem_limit_bytes=_vmem_limit(48 << 20)),
        name="post",
    )(x, a, sn, *weights)


def _rope_tables(length):
    inv = ROPE_THETA ** (-jnp.arange(0, QK_ROPE, 2, dtype=F32) / QK_ROPE)
    zeros = lambda n: jnp.zeros((n,), F32)
    inv_lanes = jnp.concatenate([zeros(QK_NOPE), inv, inv, zeros(HEAD_PAD - QK_NOPE - QK_ROPE)])
    n_hi = -(-length // ROPE_BLOCK)
    a_hi = (jnp.arange(n_hi, dtype=F32) * ROPE_BLOCK)[:, None] * inv_lanes
    a_lo = jnp.arange(ROPE_BLOCK, dtype=F32)[:, None] * inv_lanes
    ch, sh = jnp.cos(a_hi)[:, None], jnp.sin(a_hi)[:, None]
    cl, sl = jnp.cos(a_lo)[None], jnp.sin(a_lo)[None]
    cos_t = (ch * cl - sh * sl).reshape(n_hi * ROPE_BLOCK, HEAD_PAD)[:length]
    sin_t = (sh * cl + ch * sl).reshape(n_hi * ROPE_BLOCK, HEAD_PAD)[:length]
    return cos_t, sin_t


def _rot_half_cols(w):
    half = QK_ROPE // 2
    return jnp.concatenate([-w[..., half:], w[..., :half]], axis=-1)


def _pack_weights(norm1_g, w_in, q_norm_g, w_uq, kv_norm_g, w_ukv, d_skip, w_glu, b_glu,
                  attn_out_g, ssm_out_g, w_out, norm2_g, w_mlp1, w_mlp2, final_g):
    row = lambda g: g.reshape(1, -1).astype(F32)
    o = Q_RANK + KV_RANK
    w_kr = w_in[:, o:o + QK_ROPE]
    zk = jnp.zeros((D_MODEL, QK_NOPE), F32)
    zp = jnp.zeros((D_MODEL, HEAD_PAD - QK_NOPE - QK_ROPE), F32)
    wa = jnp.concatenate([w_in[:, :o], zk, w_kr, zp, zk, _rot_half_cols(w_kr), zp], axis=1)
    wu = w_in[:, o + QK_ROPE:]
    wq = w_uq.reshape(Q_RANK, N_HEADS, QK_NOPE + QK_ROPE)
    zq = jnp.zeros((Q_RANK, N_HEADS, HEAD_PAD - QK_NOPE - QK_ROPE), F32)
    wqa = jnp.concatenate([wq, zq], axis=-1).reshape(Q_RANK, N_HEADS * HEAD_PAD)
    wqb = jnp.concatenate([jnp.zeros((Q_RANK, N_HEADS, QK_NOPE), F32), _rot_half_cols(wq[..., QK_NOPE:]), zq],
                          axis=-1).reshape(Q_RANK, N_HEADS * HEAD_PAD)
    wkv = w_ukv.reshape(KV_RANK, N_HEADS, QK_NOPE + V_DIM)
    zh = jnp.zeros((KV_RANK, N_HEADS, HEAD_PAD - QK_NOPE), F32)
    wk = jnp.concatenate([wkv[..., :QK_NOPE], zh], axis=-1).reshape(KV_RANK, N_HEADS * HEAD_PAD)
    zv = jnp.zeros((KV_RANK, N_HEADS, HEAD_PAD - V_DIM), F32)
    wv = jnp.concatenate([wkv[..., QK_NOPE:], zv], axis=-1).reshape(KV_RANK, N_HEADS * HEAD_PAD)
    bf = lambda a: a.astype(BF16)
    return dict(
        n1=row(norm1_g), wa=bf(wa), wu=bf(wu), qg=row(q_norm_g), wqa=bf(wqa), wqb=bf(wqb),
        kvg=row(kv_norm_g), wk=bf(wk), wv=bf(wv),
        dsk=row(d_skip), wglu=bf(w_glu), bglu=row(b_glu), sg=row(ssm_out_g),
        ag=row(attn_out_g), wo=bf(w_out), n2=row(norm2_g), w1=bf(w_mlp1), w2=bf(w_mlp2), fg=row(final_g))


def _pack_ssm(lam_re, lam_im, log_dt, b_re, b_im, c_re, c_im, seg_len):
    cmul = lambda xr, xi, yr, yi: (xr * yr - xi * yi, xr * yi + xi * yr)
    lam_re, lam_im = lam_re.astype(F32), lam_im.astype(F32)
    dt = jnp.exp(log_dt.astype(F32))[..., None]
    mag = jnp.exp(lam_re * dt)
    a_re, a_im = mag * jnp.cos(lam_im * dt), mag * jnp.sin(lam_im * dt)
    den = lam_re * lam_re + lam_im * lam_im
    k_re = ((a_re - 1.0) * lam_re + a_im * lam_im) / den
    k_im = (a_im * lam_re - (a_re - 1.0) * lam_im) / den
    bb_re, bb_im = cmul(k_re[..., None], k_im[..., None], b_re.astype(F32), b_im.astype(F32))
    p_re, p_im = a_re, a_im
    s_re, s_im = jnp.ones_like(a_re), jnp.zeros_like(a_re)
    n = seg_len
    while n:
        if n & 1:
            s_re, s_im = cmul(s_re, s_im, p_re, p_im)
        p_re, p_im = cmul(p_re, p_im, p_re, p_im)
        n >>= 1
    eye = jnp.eye(GROUPS_PER_BLOCK, dtype=F32)
    nq, gb = SSM_LANE_BLOCKS, GROUPS_PER_BLOCK

    def b_block(part):
        p = part.reshape(2, nq, gb, SSM_STATE, SSM_GROUP)
        return jnp.einsum("dqgnh,gk->dqghkn", p, eye).reshape(2, nq, LANES, STATES_PER_BLOCK)

    def c_block(part):
        p = part.reshape(2, nq, gb, SSM_GROUP, SSM_STATE)
        return jnp.einsum("dqghn,gk->dqgnkh", p, eye).reshape(2, nq, STATES_PER_BLOCK, LANES)

    def a_block(zr, zi):
        shape = (2, nq, 1, STATES_PER_BLOCK)
        return jnp.concatenate([zr.reshape(shape), zi.reshape(shape)], axis=-1)

    bmat = jnp.concatenate([b_block(bb_re), b_block(bb_im)], axis=-1).astype(BF16)
    cmat = jnp.concatenate([c_block(c_re.astype(F32)), -c_block(c_im.astype(F32))], axis=-2).astype(BF16)
    return bmat, cmat, a_block(a_re, a_im), a_block(s_re, s_im)


def _perm_matrix(tt):
    rows = N_GROUPS * tt
    dst = np.arange(rows)
    src = (dst % N_GROUPS) * tt + dst // N_GROUPS
    p = np.zeros((rows, rows), np.float32)
    p[dst, src] = 1.0
    return jnp.asarray(p, BF16)


def _trunk(x, rope, segmented, w, ssm):
    nb, lr, _ = x.shape
    bmat, cmat, a_blk, a_seg = ssm
    cos_t, sin_t = rope
    perm = _perm_matrix(PERM_TT)
    q, kt, v, u = _pre_call(x, cos_t, sin_t, w, perm)
    if segmented:
        q5, kt5, v5 = q[None], kt[None], v[None]
    else:
        q5, kt5, v5 = q[:, None], kt[:, None], v[:, None]
    att = _attn_call(q5, kt5, v5).reshape(nb, lr, N_HEADS * V_DIM)
    init = jnp.zeros((2, SSM_LANE_BLOCKS, N_GROUPS, 2 * STATES_PER_BLOCK), F32)
    if segmented:
        (ends,) = _ssm_call(u, bmat, cmat, a_blk, init, with_y=False)
        init = _carry_call(ends, a_seg)
    yf, yb, _ = _ssm_call(u, bmat, cmat, a_blk, init, with_y=True)
    sn = _ssm_post_call(yf, yb, u, w, perm.T)
    return _post_call(x, att, sn, w)


def kernel(x_prompt, x_sample, norm1_g, w_in, q_norm_g, w_uq, kv_norm_g, w_ukv, lam_re, lam_im, log_dt,
           b_re, b_im, c_re, c_im, d_skip, w_glu, b_glu, attn_out_g, ssm_out_g, w_out, norm2_g, w_mlp1,
           w_mlp2, final_g):
    assert norm1_g.shape[0] == 1, "single-layer trunk"
    w = _pack_weights(norm1_g[0], w_in[0], q_norm_g[0], w_uq[0], kv_norm_g[0], w_ukv[0], d_skip[0], w_glu[0],
                      b_glu[0], attn_out_g[0], ssm_out_g[0], w_out[0], norm2_g[0], w_mlp1[0], w_mlp2[0], final_g)
    bp, lp, _ = x_prompt.shape
    bs, ls, _ = x_sample.shape
    assert bp == N_GROUPS and bs == 1 and ls % N_GROUPS == 0
    seg = ls // N_GROUPS
    ssm = _pack_ssm(lam_re[0], lam_im[0], log_dt[0], b_re[0], b_im[0], c_re[0], c_im[0], seg)
    cos_t, sin_t = _rope_tables(max(lp, ls))
    rope_p = (cos_t[None, :lp], sin_t[None, :lp])
    rope_s = (cos_t[:ls].reshape(N_GROUPS, seg, HEAD_PAD), sin_t[:ls].reshape(N_GROUPS, seg, HEAD_PAD))
    y_prompt = _trunk(x_prompt, rope_p, False, w, ssm)
    y_sample = _trunk(x_sample.reshape(N_GROUPS, seg, D_MODEL), rope_s, True, w, ssm)
    return y_prompt, y_sample.reshape(bs, ls, D_MODEL)
```

```python
import functools
import math

import jax
import jax.numpy as jnp
import numpy as np
from jax import lax
from jax.experimental import pallas as pl
from jax.experimental.pallas import tpu as pltpu

F32 = jnp.float32
BF16 = jnp.bfloat16

D_MODEL = 1024
N_HEADS = 8
QK_NOPE = 64
QK_ROPE = 32
V_DIM = 64
Q_RANK = 256
KV_RANK = 128
SSM_W = 512
SSM_GROUP = 16
SSM_GROUPS = 32
SSM_STATE = 64
D_FF = 4096
EPS = 1e-6
ROPE_THETA = 10000.0
ROPE_BLOCK = 128

LANES = 128
SUBLANES = 8
VMEM_LIMIT_CAP = 60000 * 1024

N_GROUPS = SUBLANES
HEAD_PAD = LANES
BF16_SUBLANES = 2 * SUBLANES
VT_ROWS = -(-(V_DIM + 1) // BF16_SUBLANES) * BF16_SUBLANES
N_PAIRS = N_HEADS // 2
SSM_LANE_BLOCKS = SSM_W // LANES
GROUPS_PER_BLOCK = LANES // SSM_GROUP
STATES_PER_BLOCK = GROUPS_PER_BLOCK * SSM_STATE

PRE_TT = 128
PERM_TT = 32
ATT_BQ = 512
ATT_BK = 256
ATT_UNROLL = 10
Q_SCALE = math.log2(math.e) / math.sqrt(QK_NOPE + QK_ROPE)
SSM_TT = 64
POST_TM = 512
FF_CHUNK = 1024


def _vmem_limit(nbytes):
    return int(min(VMEM_LIMIT_CAP, nbytes))


def _rms(x, g):
    return x * lax.rsqrt(jnp.mean(x * x, axis=-1, keepdims=True) + EPS) * g


def _dot(a, b):
    return jnp.dot(a, b, preferred_element_type=F32)


def _pre_kernel(x_ref, cos_ref, sin_ref, n1_ref, wa_ref, wu_ref, qg_ref, wqa_ref, wqb_ref,
                kvg_ref, wk_ref, wv_ref, perm_ref, qt_ref, k_ref, vt_ref, u_ref, *, tt):
    rows = N_GROUPS * tt
    x = x_ref[...].reshape(rows, D_MODEL)
    h = _rms(x, n1_ref[...]).astype(BF16)
    pa = _dot(h, wa_ref[...])
    u = _dot(h, wu_ref[...])
    cqn = _rms(pa[:, :Q_RANK], qg_ref[...]).astype(BF16)
    ckvn = _rms(pa[:, Q_RANK:Q_RANK + KV_RANK], kvg_ref[...]).astype(BF16)
    cos = jnp.broadcast_to(cos_ref[...], (N_GROUPS, tt, HEAD_PAD)).reshape(rows, HEAD_PAD)
    sin = jnp.broadcast_to(sin_ref[...], (N_GROUPS, tt, HEAD_PAD)).reshape(rows, HEAD_PAD)
    o = Q_RANK + KV_RANK
    k_rope = pa[:, o:o + HEAD_PAD] * cos + pa[:, o + HEAD_PAD:o + 2 * HEAD_PAD] * sin
    qa = _dot(cqn, wqa_ref[...])
    qb = _dot(cqn, wqb_ref[...])
    k = _dot(ckvn, wk_ref[...])
    v = _dot(ckvn, wv_ref[...])
    ones_col = (lax.broadcasted_iota(jnp.int32, (1, HEAD_PAD), 1) == V_DIM).astype(F32)
    for hd in range(N_HEADS):
        sl = slice(hd * HEAD_PAD, (hd + 1) * HEAD_PAD)
        qht = ((qa[:, sl] * cos + qb[:, sl] * sin) * Q_SCALE).T
        vht = (v[:, sl] + ones_col).T
        for b in range(N_GROUPS):
            qt_ref[b, hd] = qht[:, b * tt:(b + 1) * tt].astype(BF16)
            vt_ref[b, hd, 0] = vht[:VT_ROWS, b * tt:(b + 1) * tt].astype(BF16)
        k_ref[:, hd] = (k[:, sl] + k_rope).astype(BF16).reshape(N_GROUPS, tt, HEAD_PAD)
    u_hi = u.astype(BF16)
    u_lo = (u - u_hi.astype(F32)).astype(BF16)
    perm = perm_ref[...]
    sub = N_GROUPS * PERM_TT
    for s in range(tt // PERM_TT):
        pick = lambda a: jnp.concatenate(
            [a[b * tt + s * PERM_TT:b * tt + (s + 1) * PERM_TT] for b in range(N_GROUPS)], axis=0)
        u_ref[s * sub:(s + 1) * sub, :] = _dot(perm, pick(u_hi)) + _dot(perm, pick(u_lo))


def _pre_call(x, cos_t, sin_t, w, perm):
    nb, lr, _ = x.shape
    tt = PRE_TT
    rows = nb * tt
    n_t = lr // tt
    tab_nb = cos_t.shape[0]
    kt_sub = ATT_BK // tt
    full = lambda a: pl.BlockSpec(a.shape, lambda j: (0,) * a.ndim)
    weights = [w["n1"], w["wa"], w["wu"], w["qg"], w["wqa"], w["wqb"], w["kvg"], w["wk"], w["wv"], perm]
    in_specs = [
        pl.BlockSpec((nb, tt, D_MODEL), lambda j: (0, j, 0)),
        pl.BlockSpec((tab_nb, tt, HEAD_PAD), lambda j: (0, j, 0)),
        pl.BlockSpec((tab_nb, tt, HEAD_PAD), lambda j: (0, j, 0)),
    ] + [full(a) for a in weights]
    out_shape = [
        jax.ShapeDtypeStruct((nb, N_HEADS, HEAD_PAD, lr), BF16),
        jax.ShapeDtypeStruct((nb, N_HEADS, lr, HEAD_PAD), BF16),
        jax.ShapeDtypeStruct((nb, N_HEADS, lr // ATT_BK, VT_ROWS, ATT_BK), BF16),
        jax.ShapeDtypeStruct((lr * nb, SSM_W), F32),
    ]
    out_specs = [
        pl.BlockSpec((nb, N_HEADS, HEAD_PAD, tt), lambda j: (0, 0, 0, j)),
        pl.BlockSpec((nb, N_HEADS, tt, HEAD_PAD), lambda j: (0, 0, j, 0)),
        pl.BlockSpec((nb, N_HEADS, 1, VT_ROWS, tt), lambda j: (0, 0, j // kt_sub, 0, j % kt_sub)),
        pl.BlockSpec((rows, SSM_W), lambda j: (j, 0)),
    ]
    return pl.pallas_call(
        functools.partial(_pre_kernel, tt=tt),
        grid=(n_t,),
        in_specs=in_specs,
        out_specs=out_specs,
        out_shape=out_shape,
        compiler_params=pltpu.CompilerParams(
            dimension_semantics=("arbitrary",), vmem_limit_bytes=_vmem_limit(56 << 20)),
        name="pre",
    )(x, cos_t, sin_t, *weights)


def _attn_kernel(qt_ref, k_ref, vt_ref, o_ref, s_ref, p_ref, alpha_ref, m_ref, acc_ref, *, n_seg, n_kb):
    bq, bk = ATT_BQ, ATT_BK
    m_ref[...] = jnp.full(m_ref.shape, -0.5 * float(np.finfo(np.float32).max), F32)
    acc_ref[...] = jnp.zeros(acc_ref.shape, F32)

    n = n_seg * n_kb

    def scores(c, slot):
        r0 = (c % n_kb) * bk
        if not isinstance(c, int):
            r0 = pl.multiple_of(r0, bk)
        for j in range(2):
            s_ref[slot, j] = _dot(k_ref[0, c // n_kb, j, pl.ds(r0, bk), :], qt_ref[0, 0, j])

    def softmax(slot):
        for j in range(2):
            for lt in range(bq // LANES):
                lanes = slice(lt * LANES, (lt + 1) * LANES)
                blk = s_ref[slot, j, :, lanes]
                m_old = m_ref[j, :, lanes]
                m_new = jnp.maximum(m_old, jnp.max(blk, axis=0, keepdims=True))
                alpha_ref[slot, j, :, lanes] = jnp.exp2(m_old - m_new)
                m_ref[j, :, lanes] = m_new
                p_ref[slot, j, :, lanes] = jnp.exp2(blk - m_new).astype(BF16)

    def weighted_values(c, slot):
        for j in range(2):
            pv = _dot(vt_ref[0, c // n_kb, j, c % n_kb], p_ref[slot, j])
            acc_ref[j] = acc_ref[j] * alpha_ref[slot, j] + pv

    steady = max(n - 2, 0)
    unroll = min(ATT_UNROLL, max(steady - steady % 2, 2))

    def stage(t, parity, do_scores=True, do_softmax=True, do_values=True):
        if do_values:
            weighted_values(t - 2, parity)
        if do_softmax:
            softmax(1 - parity)
        if do_scores:
            scores(t, parity)

    for t in range(2):
        stage(t, t % 2, t < n, 1 <= t <= n, False)
    n_iter = steady // unroll
    if n_iter:
        def body(k, _):
            for i in range(unroll):
                stage(2 + unroll * k + i, i % 2)
            return 0
        lax.fori_loop(0, n_iter, body, 0)
    for t in range(2 + unroll * n_iter, n + 2):
        stage(t, t % 2, t < n, t <= n, True)
    outs = [acc_ref[j][:V_DIM] / acc_ref[j][V_DIM:V_DIM + 1] for j in range(2)]
    o_ref[0] = jnp.concatenate(outs, axis=0).T


def _attn_call(qt, k, vt):
    n_seq, n_seg, _, lr, _ = k.shape
    n_kb = lr // ATT_BK
    n_qb = lr // ATT_BQ
    resident = dict(pipeline_mode=pl.Buffered(1)) if n_seg > 1 else {}
    scratch = [
        pltpu.VMEM((2, 2, ATT_BK, ATT_BQ), F32),
        pltpu.VMEM((2, 2, ATT_BK, ATT_BQ), BF16),
        pltpu.VMEM((2, 2, 1, ATT_BQ), F32),
        pltpu.VMEM((2, 1, ATT_BQ), F32),
        pltpu.VMEM((2, VT_ROWS, ATT_BQ), F32),
    ]
    in_specs = [
        pl.BlockSpec((1, 1, 2, HEAD_PAD, ATT_BQ), lambda s, p, i: (s, i // n_qb, p, 0, i % n_qb)),
        pl.BlockSpec((1, n_seg, 2, lr, HEAD_PAD), lambda s, p, i: (s, 0, p, 0, 0), **resident),
        pl.BlockSpec((1, n_seg, 2, n_kb, VT_ROWS, ATT_BK), lambda s, p, i: (s, 0, p, 0, 0, 0), **resident),
    ]
    kv_bytes = 2 * n_seg * lr * (HEAD_PAD + VT_ROWS) * 2 * (1 if n_seg > 1 else 2)
    return pl.pallas_call(
        functools.partial(_attn_kernel, n_seg=n_seg, n_kb=n_kb),
        grid=(n_seq, N_PAIRS, n_seg * n_qb),
        in_specs=in_specs,
        out_specs=pl.BlockSpec((1, ATT_BQ, HEAD_PAD), lambda s, p, i: (s, i, p)),
        out_shape=jax.ShapeDtypeStruct((n_seq, n_seg * lr, N_HEADS * V_DIM), F32),
        scratch_shapes=scratch,
        compiler_params=pltpu.CompilerParams(
            dimension_semantics=("arbitrary", "arbitrary", "arbitrary"),
            vmem_limit_bytes=_vmem_limit(kv_bytes + (24 << 20))),
        name="attn",
    )(qt, k, vt)


def _cstep(xre, xim, are, aim, bre, bim):
    return are * xre - aim * xim + bre, are * xim + aim * xre + bim


def _ssm_kernel(uf_ref, ub_ref, bmat_ref, cmat_ref, a_ref, init_ref, *rest, tt, with_y):
    if with_y:
        yf_ref, yb_ref, fin_ref, state_ref, bu_ref, xs_ref = rest
    else:
        fin_ref, state_ref, bu_ref = rest
    j = pl.program_id(0)
    rows = N_GROUPS * tt
    n_pairs = tt // 2
    S = STATES_PER_BLOCK

    @pl.when(j == 0)
    def _():
        state_ref[...] = init_ref[...]

    for q in range(SSM_LANE_BLOCKS):
        lanes = slice(q * LANES, (q + 1) * LANES)
        slot = q % 2
        bu_ref[slot, 0] = _dot(uf_ref[:, lanes].astype(BF16), bmat_ref[0, q])
        bu_ref[slot, 1] = _dot(ub_ref[:, lanes].astype(BF16), bmat_ref[1, q])
        a = [jnp.broadcast_to(a_ref[d, q], (N_GROUPS, 2 * S)) for d in range(2)]
        are = [a[d][:, :S] for d in range(2)]
        aim = [a[d][:, S:] for d in range(2)]
        fwd = (state_ref[0, q, :, :S], state_ref[0, q, :, S:])
        bwd = (state_ref[1, q, :, :S], state_ref[1, q, :, S:])
        for k in range(n_pairs):
            r0 = k * 2 * N_GROUPS
            blk = bu_ref[slot, 0, r0:r0 + 2 * N_GROUPS, :]
            f1 = _cstep(*fwd, are[0], aim[0], blk[:N_GROUPS, :S], blk[:N_GROUPS, S:])
            fwd = _cstep(*f1, are[0], aim[0], blk[N_GROUPS:, :S], blk[N_GROUPS:, S:])
            r1 = rows - (k + 1) * 2 * N_GROUPS
            blk = bu_ref[slot, 1, r1:r1 + 2 * N_GROUPS, :]
            b1 = _cstep(*bwd, are[1], aim[1], blk[N_GROUPS:, :S], blk[N_GROUPS:, S:])
            bwd = _cstep(*b1, are[1], aim[1], blk[:N_GROUPS, :S], blk[:N_GROUPS, S:])
            if with_y:
                xs_ref[slot, 0, r0:r0 + 2 * N_GROUPS, :] = jnp.concatenate(
                    [jnp.concatenate(f1, axis=1), jnp.concatenate(fwd, axis=1)], axis=0).astype(BF16)
                xs_ref[slot, 1, r1:r1 + 2 * N_GROUPS, :] = jnp.concatenate(
                    [jnp.concatenate(bwd, axis=1), jnp.concatenate(b1, axis=1)], axis=0).astype(BF16)
        state_ref[0, q] = jnp.concatenate(fwd, axis=1)
        state_ref[1, q] = jnp.concatenate(bwd, axis=1)
        if with_y:
            yf_ref[:, lanes] = _dot(xs_ref[slot, 0], cmat_ref[0, q])
            yb_ref[:, lanes] = _dot(xs_ref[slot, 1], cmat_ref[1, q])

    @pl.when(j == pl.num_programs(0) - 1)
    def _():
        fin_ref[...] = state_ref[...]


def _ssm_call(u, bmat, cmat, a, init, with_y):
    n_rows = u.shape[0]
    tt = SSM_TT
    rows = N_GROUPS * tt
    n_t = n_rows // rows
    full = lambda arr: pl.BlockSpec(arr.shape, lambda j: (0,) * arr.ndim)
    in_specs = [
        pl.BlockSpec((rows, SSM_W), lambda j: (j, 0)),
        pl.BlockSpec((rows, SSM_W), lambda j: (n_t - 1 - j, 0)),
        full(bmat), full(cmat), full(a), full(init),
    ]
    state_shape = (2, SSM_LANE_BLOCKS, N_GROUPS, 2 * STATES_PER_BLOCK)
    out_shape = [jax.ShapeDtypeStruct(state_shape, F32)]
    out_specs = [pl.BlockSpec(state_shape, lambda j: (0, 0, 0, 0))]
    scratch = [pltpu.VMEM(state_shape, F32), pltpu.VMEM((2, 2, rows, 2 * STATES_PER_BLOCK), F32)]
    if with_y:
        out_shape = [jax.ShapeDtypeStruct((n_rows, SSM_W), F32)] * 2 + out_shape
        out_specs = [pl.BlockSpec((rows, SSM_W), lambda j: (j, 0)),
                     pl.BlockSpec((rows, SSM_W), lambda j: (n_t - 1 - j, 0))] + out_specs
        scratch.append(pltpu.VMEM((2, 2, rows, 2 * STATES_PER_BLOCK), BF16))
    return pl.pallas_call(
        functools.partial(_ssm_kernel, tt=tt, with_y=with_y),
        grid=(n_t,),
        in_specs=in_specs,
        out_specs=out_specs,
        out_shape=out_shape,
        scratch_shapes=scratch,
        compiler_params=pltpu.CompilerParams(
            dimension_semantics=("arbitrary",), vmem_limit_bytes=_vmem_limit(48 << 20)),
        name="ssm" if with_y else "ssm_ends",
    )(u, u, bmat, cmat, a, init)


def _carry_kernel(e_ref, as_ref, i_ref):
    S = STATES_PER_BLOCK
    row = lax.broadcasted_iota(jnp.int32, (N_GROUPS, S), 0)
    for d in range(2):
        first = 0 if d == 0 else N_GROUPS - 1
        shift = 1 if d == 0 else N_GROUPS - 1
        for q in range(SSM_LANE_BLOCKS):
            e = e_ref[d, q]
            a = jnp.broadcast_to(as_ref[d, q], (N_GROUPS, 2 * S))
            ere, eim, are, aim = e[:, :S], e[:, S:], a[:, :S], a[:, S:]
            ire = jnp.zeros((N_GROUPS, S), F32)
            iim = jnp.zeros((N_GROUPS, S), F32)
            for _ in range(N_GROUPS - 1):
                tre, tim = _cstep(ire, iim, are, aim, ere, eim)
                ire = jnp.where(row == first, 0.0, pltpu.roll(tre, shift, 0))
                iim = jnp.where(row == first, 0.0, pltpu.roll(tim, shift, 0))
            i_ref[d, q] = jnp.concatenate([ire, iim], axis=1)


def _carry_call(ends, a_seg):
    return pl.pallas_call(
        _carry_kernel,
        out_shape=jax.ShapeDtypeStruct(ends.shape, F32),
        name="ssm_carry",
    )(ends, a_seg)


def _ssm_post_kernel(yf_ref, yb_ref, u_ref, dsk_ref, wglu_ref, bglu_ref, sg_ref, permt_ref, o_ref, *, tt):
    y = yf_ref[...] + yb_ref[...] + dsk_ref[...] * u_ref[...]
    g = jax.nn.gelu(y)
    z = _dot(g.astype(BF16), wglu_ref[...]) + bglu_ref[...]
    s = g * jax.nn.sigmoid(z)
    sn = _rms(s, sg_ref[...]).astype(BF16)
    permt = permt_ref[...]
    sub = N_GROUPS * PERM_TT
    for k in range(tt // PERM_TT):
        blk = _dot(permt, sn[k * sub:(k + 1) * sub]).astype(BF16)
        o_ref[:, k * PERM_TT:(k + 1) * PERM_TT, :] = blk.reshape(N_GROUPS, PERM_TT, SSM_W)


def _ssm_post_call(yf, yb, u, w, permt):
    n_rows = u.shape[0]
    tt = PRE_TT
    rows = N_GROUPS * tt
    n_t = n_rows // rows
    lr = n_rows // N_GROUPS
    full = lambda a: pl.BlockSpec(a.shape, lambda j: (0,) * a.ndim)
    row_spec = pl.BlockSpec((rows, SSM_W), lambda j: (j, 0))
    weights = [w["dsk"], w["wglu"], w["bglu"], w["sg"], permt]
    return pl.pallas_call(
        functools.partial(_ssm_post_kernel, tt=tt),
        grid=(n_t,),
        in_specs=[row_spec, row_spec, row_spec] + [full(a) for a in weights],
        out_specs=pl.BlockSpec((N_GROUPS, tt, SSM_W), lambda j: (0, j, 0)),
        out_shape=jax.ShapeDtypeStruct((N_GROUPS, lr, SSM_W), BF16),
        compiler_params=pltpu.CompilerParams(
            dimension_semantics=("arbitrary",), vmem_limit_bytes=_vmem_limit(40 << 20)),
        name="ssm_post",
    )(yf, yb, u, *weights)


def _post_kernel(x_ref, a_ref, sn_ref, ag_ref, wo_ref, n2_ref, w1_ref, w2_ref, fg_ref, o_ref):
    an = _rms(a_ref[0], ag_ref[...]).astype(BF16)
    mixed = jnp.concatenate([an, sn_ref[0]], axis=-1)
    x1 = x_ref[0] + _dot(mixed, wo_ref[...])
    h2 = _rms(x1, n2_ref[...]).astype(BF16)
    acc = jnp.zeros_like(x1)
    for c in range(D_FF // FF_CHUNK):
        hid = _dot(h2, w1_ref[:, c * FF_CHUNK:(c + 1) * FF_CHUNK])
        hid = jnp.square(jnp.maximum(hid, 0.0)).astype(BF16)
        acc = acc + _dot(hid, w2_ref[c * FF_CHUNK:(c + 1) * FF_CHUNK, :])
    o_ref[0] = _rms(x1 + acc, fg_ref[...])


def _post_call(x, a, sn, w):
    nb, lr, _ = x.shape
    tm = POST_TM
    const = lambda arr: pl.BlockSpec(arr.shape, lambda b, i: (0,) * arr.ndim, pipeline_mode=pl.Buffered(1))
    weights = [w["ag"], w["wo"], w["n2"], w["w1"], w["w2"], w["fg"]]
    return pl.pallas_call(
        _post_kernel,
        grid=(nb, lr // tm),
        in_specs=[
            pl.BlockSpec((1, tm, D_MODEL), lambda b, i: (b, i, 0)),
            pl.BlockSpec((1, tm, N_HEADS * V_DIM), lambda b, i: (b, i, 0)),
            pl.BlockSpec((1, tm, SSM_W), lambda b, i: (b, i, 0)),
        ] + [const(arr) for arr in weights],
        out_specs=pl.BlockSpec((1, tm, D_MODEL), lambda b, i: (b, i, 0)),
        out_shape=jax.ShapeDtypeStruct((nb, lr, D_MODEL), F32),
        compiler_params=pltpu.CompilerParams(
            dimension_semantics=("arbitrary", "arbitrary"), vmem_limit_bytes=_vmem_limit(48 << 20)),
        name="post",
    )(x, a, sn, *weights)


def _rope_tables(length):
    inv = ROPE_THETA ** (-jnp.arange(0, QK_ROPE, 2, dtype=F32) / QK_ROPE)
    zeros = lambda n: jnp.zeros((n,), F32)
    inv_lanes = jnp.concatenate([zeros(QK_NOPE), inv, inv, zeros(HEAD_PAD - QK_NOPE - QK_ROPE)])
    n_hi = -(-length // ROPE_BLOCK)
    a_hi = (jnp.arange(n_hi, dtype=F32) * ROPE_BLOCK)[:, None] * inv_lanes
    a_lo = jnp.arange(ROPE_BLOCK, dtype=F32)[:, None] * inv_lanes
    ch, sh = jnp.cos(a_hi)[:, None], jnp.sin(a_hi)[:, None]
    cl, sl = jnp.cos(a_lo)[None], jnp.sin(a_lo)[None]
    cos_t = (ch * cl - sh * sl).reshape(n_hi * ROPE_BLOCK, HEAD_PAD)[:length]
    sin_t = (sh * cl + ch * sl).reshape(n_hi * ROPE_BLOCK, HEAD_PAD)[:length]
    return cos_t, sin_t


def _rot_half_cols(w):
    half = QK_ROPE // 2
    return jnp.concatenate([-w[..., half:], w[..., :half]], axis=-1)


def _pack_weights(norm1_g, w_in, q_norm_g, w_uq, kv_norm_g, w_ukv, d_skip, w_glu, b_glu,
                  attn_out_g, ssm_out_g, w_out, norm2_g, w_mlp1, w_mlp2, final_g):
    row = lambda g: g.reshape(1, -1).astype(F32)
    o = Q_RANK + KV_RANK
    w_kr = w_in[:, o:o + QK_ROPE]
    zk = jnp.zeros((D_MODEL, QK_NOPE), F32)
    zp = jnp.zeros((D_MODEL, HEAD_PAD - QK_NOPE - QK_ROPE), F32)
    wa = jnp.concatenate([w_in[:, :o], zk, w_kr, zp, zk, _rot_half_cols(w_kr), zp], axis=1)
    wu = w_in[:, o + QK_ROPE:]
    wq = w_uq.reshape(Q_RANK, N_HEADS, QK_NOPE + QK_ROPE)
    zq = jnp.zeros((Q_RANK, N_HEADS, HEAD_PAD - QK_NOPE - QK_ROPE), F32)
    wqa = jnp.concatenate([wq, zq], axis=-1).reshape(Q_RANK, N_HEADS * HEAD_PAD)
    wqb = jnp.concatenate([jnp.zeros((Q_RANK, N_HEADS, QK_NOPE), F32), _rot_half_cols(wq[..., QK_NOPE:]), zq],
                          axis=-1).reshape(Q_RANK, N_HEADS * HEAD_PAD)
    wkv = w_ukv.reshape(KV_RANK, N_HEADS, QK_NOPE + V_DIM)
    zh = jnp.zeros((KV_RANK, N_HEADS, HEAD_PAD - QK_NOPE), F32)
    wk = jnp.concatenate([wkv[..., :QK_NOPE], zh], axis=-1).reshape(KV_RANK, N_HEADS * HEAD_PAD)
    zv = jnp.zeros((KV_RANK, N_HEADS, HEAD_PAD - V_DIM), F32)
    wv = jnp.concatenate([wkv[..., QK_NOPE:], zv], axis=-1).reshape(KV_RANK, N_HEADS * HEAD_PAD)
    bf = lambda a: a.astype(BF16)
    return dict(
        n1=row(norm1_g), wa=bf(wa), wu=bf(wu), qg=row(q_norm_g), wqa=bf(wqa), wqb=bf(wqb),
        kvg=row(kv_norm_g), wk=bf(wk), wv=bf(wv),
        dsk=row(d_skip), wglu=bf(w_glu), bglu=row(b_glu), sg=row(ssm_out_g),
        ag=row(attn_out_g), wo=bf(w_out), n2=row(norm2_g), w1=bf(w_mlp1), w2=bf(w_mlp2), fg=row(final_g))


def _pack_ssm(lam_re, lam_im, log_dt, b_re, b_im, c_re, c_im, seg_len):
    cmul = lambda xr, xi, yr, yi: (xr * yr - xi * yi, xr * yi + xi * yr)
    lam_re, lam_im = lam_re.astype(F32), lam_im.astype(F32)
    dt = jnp.exp(log_dt.astype(F32))[..., None]
    mag = jnp.exp(lam_re * dt)
    a_re, a_im = mag * jnp.cos(lam_im * dt), mag * jnp.sin(lam_im * dt)
    den = lam_re * lam_re + lam_im * lam_im
    k_re = ((a_re - 1.0) * lam_re + a_im * lam_im) / den
    k_im = (a_im * lam_re - (a_re - 1.0) * lam_im) / den
    bb_re, bb_im = cmul(k_re[..., None], k_im[..., None], b_re.astype(F32), b_im.astype(F32))
    p_re, p_im = a_re, a_im
    s_re, s_im = jnp.ones_like(a_re), jnp.zeros_like(a_re)
    n = seg_len
    while n:
        if n & 1:
            s_re, s_im = cmul(s_re, s_im, p_re, p_im)
        p_re, p_im = cmul(p_re, p_im, p_re, p_im)
        n >>= 1
    eye = jnp.eye(GROUPS_PER_BLOCK, dtype=F32)
    nq, gb = SSM_LANE_BLOCKS, GROUPS_PER_BLOCK

    def b_block(part):
        p = part.reshape(2, nq, gb, SSM_STATE, SSM_GROUP)
        return jnp.einsum("dqgnh,gk->dqghkn", p, eye).reshape(2, nq, LANES, STATES_PER_BLOCK)

    def c_block(part):
        p = part.reshape(2, nq, gb, SSM_GROUP, SSM_STATE)
        return jnp.einsum("dqghn,gk->dqgnkh", p, eye).reshape(2, nq, STATES_PER_BLOCK, LANES)

    def a_block(zr, zi):
        shape = (2, nq, 1, STATES_PER_BLOCK)
        return jnp.concatenate([zr.reshape(shape), zi.reshape(shape)], axis=-1)

    bmat = jnp.concatenate([b_block(bb_re), b_block(bb_im)], axis=-1).astype(BF16)
    cmat = jnp.concatenate([c_block(c_re.astype(F32)), -c_block(c_im.astype(F32))], axis=-2).astype(BF16)
    return bmat, cmat, a_block(a_re, a_im), a_block(s_re, s_im)


def _perm_matrix(tt):
    rows = N_GROUPS * tt
    dst = np.arange(rows)
    src = (dst % N_GROUPS) * tt + dst // N_GROUPS
    p = np.zeros((rows, rows), np.float32)
    p[dst, src] = 1.0
    return jnp.asarray(p, BF16)


def _trunk(x, rope, segmented, w, ssm):
    nb, lr, _ = x.shape
    bmat, cmat, a_blk, a_seg = ssm
    cos_t, sin_t = rope
    perm = _perm_matrix(PERM_TT)
    qt, k, vt, u = _pre_call(x, cos_t, sin_t, w, perm)
    seq_seg = (lambda a: a[None]) if segmented else (lambda a: a[:, None])
    att = _attn_call(seq_seg(qt), seq_seg(k), seq_seg(vt)).reshape(nb, lr, N_HEADS * V_DIM)
    init = jnp.zeros((2, SSM_LANE_BLOCKS, N_GROUPS, 2 * STATES_PER_BLOCK), F32)
    if segmented:
        (ends,) = _ssm_call(u, bmat, cmat, a_blk, init, with_y=False)
        init = _carry_call(ends, a_seg)
    yf, yb, _ = _ssm_call(u, bmat, cmat, a_blk, init, with_y=True)
    sn = _ssm_post_call(yf, yb, u, w, perm.T)
    return _post_call(x, att, sn, w)


def kernel(x_prompt, x_sample, norm1_g, w_in, q_norm_g, w_uq, kv_norm_g, w_ukv, lam_re, lam_im, log_dt,
           b_re, b_im, c_re, c_im, d_skip, w_glu, b_glu, attn_out_g, ssm_out_g, w_out, norm2_g, w_mlp1,
           w_mlp2, final_g):
    assert norm1_g.shape[0] == 1, "single-layer trunk"
    w = _pack_weights(norm1_g[0], w_in[0], q_norm_g[0], w_uq[0], kv_norm_g[0], w_ukv[0], d_skip[0], w_glu[0],
                      b_glu[0], attn_out_g[0], ssm_out_g[0], w_out[0], norm2_g[0], w_mlp1[0], w_mlp2[0], final_g)
    bp, lp, _ = x_prompt.shape
    bs, ls, _ = x_sample.shape
    assert bp == N_GROUPS and bs == 1 and ls % N_GROUPS == 0
    seg = ls // N_GROUPS
    ssm = _pack_ssm(lam_re[0], lam_im[0], log_dt[0], b_re[0], b_im[0], c_re[0], c_im[0], seg)
    cos_t, sin_t = _rope_tables(max(lp, ls))
    rope_p = (cos_t[None, :lp], sin_t[None, :lp])
    rope_s = (cos_t[:ls].reshape(N_GROUPS, seg, HEAD_PAD), sin_t[:ls].reshape(N_GROUPS, seg, HEAD_PAD))
    y_prompt = _trunk(x_prompt, rope_p, False, w, ssm)
    y_sample = _trunk(x_sample.reshape(N_GROUPS, seg, D_MODEL), rope_s, True, w, ssm)
    return y_prompt, y_sample.reshape(bs, ls, D_MODEL)
```

```python
import functools
import math

import jax
import jax.numpy as jnp
import numpy as np
from jax import lax
from jax.experimental import pallas as pl
from jax.experimental.pallas import tpu as pltpu

F32 = jnp.float32
BF16 = jnp.bfloat16

D_MODEL = 1024
N_HEADS = 8
QK_NOPE = 64
QK_ROPE = 32
V_DIM = 64
Q_RANK = 256
KV_RANK = 128
SSM_W = 512
SSM_GROUP = 16
SSM_GROUPS = 32
SSM_STATE = 64
D_FF = 4096
EPS = 1e-6
ROPE_THETA = 10000.0
ROPE_BLOCK = 128

LANES = 128
SUBLANES = 8
VMEM_LIMIT_CAP = 60000 * 1024

N_GROUPS = SUBLANES
HEAD_PAD = LANES
BF16_SUBLANES = 2 * SUBLANES
VT_ROWS = -(-(V_DIM + 1) // BF16_SUBLANES) * BF16_SUBLANES
N_PAIRS = N_HEADS // 2
SSM_LANE_BLOCKS = SSM_W // LANES
GROUPS_PER_BLOCK = LANES // SSM_GROUP
STATES_PER_BLOCK = GROUPS_PER_BLOCK * SSM_STATE

PRE_TT = 128
PERM_TT = 32
ATT_BQ = 512
ATT_BK = 256
ATT_UNROLL = 10
Q_SCALE = math.log2(math.e) / math.sqrt(QK_NOPE + QK_ROPE)
SSM_TT = 64
POST_TM = 512
FF_CHUNK = 1024


def _vmem_limit(nbytes):
    return int(min(VMEM_LIMIT_CAP, nbytes))


def _rms(x, g):
    return x * lax.rsqrt(jnp.mean(x * x, axis=-1, keepdims=True) + EPS) * g


def _dot(a, b):
    return jnp.dot(a, b, preferred_element_type=F32)


def _pre_kernel(x_ref, cos_ref, sin_ref, n1_ref, wa_ref, wu_ref, qg_ref, wqa_ref, wqb_ref,
                kvg_ref, wk_ref, wv_ref, perm_ref, qt_ref, k_ref, vt_ref, u_ref, *, tt):
    rows = N_GROUPS * tt
    x = x_ref[...].reshape(rows, D_MODEL)
    h = _rms(x, n1_ref[...]).astype(BF16)
    pa = _dot(h, wa_ref[...])
    u = _dot(h, wu_ref[...])
    cqn = _rms(pa[:, :Q_RANK], qg_ref[...]).astype(BF16)
    ckvn = _rms(pa[:, Q_RANK:Q_RANK + KV_RANK], kvg_ref[...]).astype(BF16)
    cos = jnp.broadcast_to(cos_ref[...], (N_GROUPS, tt, HEAD_PAD)).reshape(rows, HEAD_PAD)
    sin = jnp.broadcast_to(sin_ref[...], (N_GROUPS, tt, HEAD_PAD)).reshape(rows, HEAD_PAD)
    o = Q_RANK + KV_RANK
    k_rope = pa[:, o:o + HEAD_PAD] * cos + pa[:, o + HEAD_PAD:o + 2 * HEAD_PAD] * sin
    qa = _dot(cqn, wqa_ref[...])
    qb = _dot(cqn, wqb_ref[...])
    k = _dot(ckvn, wk_ref[...])
    v = _dot(ckvn, wv_ref[...])
    ones_col = (lax.broadcasted_iota(jnp.int32, (1, HEAD_PAD), 1) == V_DIM).astype(F32)
    for hd in range(N_HEADS):
        sl = slice(hd * HEAD_PAD, (hd + 1) * HEAD_PAD)
        qht = ((qa[:, sl] * cos + qb[:, sl] * sin) * Q_SCALE).T
        vht = (v[:, sl] + ones_col).T
        for b in range(N_GROUPS):
            qt_ref[b, hd] = qht[:, b * tt:(b + 1) * tt].astype(BF16)
            vt_ref[b, hd, 0] = vht[:VT_ROWS, b * tt:(b + 1) * tt].astype(BF16)
        k_ref[:, hd] = (k[:, sl] + k_rope).astype(BF16).reshape(N_GROUPS, tt, HEAD_PAD)
    u_hi = u.astype(BF16)
    u_lo = (u - u_hi.astype(F32)).astype(BF16)
    perm = perm_ref[...]
    sub = N_GROUPS * PERM_TT
    for s in range(tt // PERM_TT):
        pick = lambda a: jnp.concatenate(
            [a[b * tt + s * PERM_TT:b * tt + (s + 1) * PERM_TT] for b in range(N_GROUPS)], axis=0)
        u_ref[s * sub:(s + 1) * sub, :] = _dot(perm, pick(u_hi)) + _dot(perm, pick(u_lo))


def _pre_call(x, cos_t, sin_t, w, perm):
    nb, lr, _ = x.shape
    tt = PRE_TT
    rows = nb * tt
    n_t = lr // tt
    tab_nb = cos_t.shape[0]
    kt_sub = ATT_BK // tt
    full = lambda a: pl.BlockSpec(a.shape, lambda j: (0,) * a.ndim)
    weights = [w["n1"], w["wa"], w["wu"], w["qg"], w["wqa"], w["wqb"], w["kvg"], w["wk"], w["wv"], perm]
    in_specs = [
        pl.BlockSpec((nb, tt, D_MODEL), lambda j: (0, j, 0)),
        pl.BlockSpec((tab_nb, tt, HEAD_PAD), lambda j: (0, j, 0)),
        pl.BlockSpec((tab_nb, tt, HEAD_PAD), lambda j: (0, j, 0)),
    ] + [full(a) for a in weights]
    out_shape = [
        jax.ShapeDtypeStruct((nb, N_HEADS, HEAD_PAD, lr), BF16),
        jax.ShapeDtypeStruct((nb, N_HEADS, lr, HEAD_PAD), BF16),
        jax.ShapeDtypeStruct((nb, N_HEADS, lr // ATT_BK, VT_ROWS, ATT_BK), BF16),
        jax.ShapeDtypeStruct((lr * nb, SSM_W), F32),
    ]
    out_specs = [
        pl.BlockSpec((nb, N_HEADS, HEAD_PAD, tt), lambda j: (0, 0, 0, j)),
        pl.BlockSpec((nb, N_HEADS, tt, HEAD_PAD), lambda j: (0, 0, j, 0)),
        pl.BlockSpec((nb, N_HEADS, 1, VT_ROWS, tt), lambda j: (0, 0, j // kt_sub, 0, j % kt_sub)),
        pl.BlockSpec((rows, SSM_W), lambda j: (j, 0)),
    ]
    return pl.pallas_call(
        functools.partial(_pre_kernel, tt=tt),
        grid=(n_t,),
        in_specs=in_specs,
        out_specs=out_specs,
        out_shape=out_shape,
        compiler_params=pltpu.CompilerParams(
            dimension_semantics=("arbitrary",), vmem_limit_bytes=_vmem_limit(56 << 20)),
        name="pre",
    )(x, cos_t, sin_t, *weights)


def _attn_kernel(qt_ref, k_ref, vt_ref, o_ref, s_ref, p_ref, alpha_ref, m_ref, acc_ref, *, n_seg, n_kb):
    bq, bk = ATT_BQ, ATT_BK
    m_ref[...] = jnp.full(m_ref.shape, -0.5 * float(np.finfo(np.float32).max), F32)
    acc_ref[...] = jnp.zeros(acc_ref.shape, F32)

    n = n_seg * n_kb

    def scores(c, slot):
        r0 = (c % n_kb) * bk
        if not isinstance(c, int):
            r0 = pl.multiple_of(r0, bk)
        for j in range(2):
            s_ref[slot, j, :, :bq] = _dot(k_ref[0, c // n_kb, j, pl.ds(r0, bk), :], qt_ref[0, 0, j])

    def softmax(slot):
        for j in range(2):
            for lt in range(bq // LANES):
                lanes = slice(lt * LANES, (lt + 1) * LANES)
                blk = s_ref[slot, j, :, lanes]
                m_old = m_ref[j, :, lanes]
                m_new = jnp.maximum(m_old, jnp.max(blk, axis=0, keepdims=True))
                alpha_ref[slot, j, :, lanes] = jnp.exp2(m_old - m_new)
                m_ref[j, :, lanes] = m_new
                p_ref[slot, j, :, lanes] = jnp.exp2(blk - m_new).astype(BF16)

    def weighted_values(c, slot):
        for j in range(2):
            pv = _dot(vt_ref[0, c // n_kb, j, c % n_kb], p_ref[slot, j, :, :bq])
            acc_ref[j] = acc_ref[j] * alpha_ref[slot, j] + pv

    steady = max(n - 2, 0)
    unroll = min(ATT_UNROLL, max(steady - steady % 2, 2))

    def stage(t, parity, do_scores=True, do_softmax=True, do_values=True):
        if do_values:
            weighted_values(t - 2, parity)
        if do_softmax:
            softmax(1 - parity)
        if do_scores:
            scores(t, parity)

    for t in range(2):
        stage(t, t % 2, t < n, 1 <= t <= n, False)
    n_iter = steady // unroll
    if n_iter:
        def body(k, _):
            for i in range(unroll):
                stage(2 + unroll * k + i, i % 2)
            return 0
        lax.fori_loop(0, n_iter, body, 0)
    for t in range(2 + unroll * n_iter, n + 2):
        stage(t, t % 2, t < n, t <= n, True)
    outs = [acc_ref[j][:V_DIM] / acc_ref[j][V_DIM:V_DIM + 1] for j in range(2)]
    o_ref[0] = jnp.concatenate(outs, axis=0).T


def _attn_call(qt, k, vt):
    n_seq, n_seg, _, lr, _ = k.shape
    n_kb = lr // ATT_BK
    n_qb = lr // ATT_BQ
    resident = dict(pipeline_mode=pl.Buffered(1)) if n_seg > 1 else {}
    scratch = [
        pltpu.VMEM((2, 2, ATT_BK, ATT_BQ + LANES), F32),
        pltpu.VMEM((2, 2, ATT_BK, ATT_BQ + LANES), BF16),
        pltpu.VMEM((2, 2, 1, ATT_BQ), F32),
        pltpu.VMEM((2, 1, ATT_BQ), F32),
        pltpu.VMEM((2, VT_ROWS, ATT_BQ), F32),
    ]
    in_specs = [
        pl.BlockSpec((1, 1, 2, HEAD_PAD, ATT_BQ), lambda s, p, i: (s, i // n_qb, p, 0, i % n_qb)),
        pl.BlockSpec((1, n_seg, 2, lr, HEAD_PAD), lambda s, p, i: (s, 0, p, 0, 0), **resident),
        pl.BlockSpec((1, n_seg, 2, n_kb, VT_ROWS, ATT_BK), lambda s, p, i: (s, 0, p, 0, 0, 0), **resident),
    ]
    kv_bytes = 2 * n_seg * lr * (HEAD_PAD + VT_ROWS) * 2 * (1 if n_seg > 1 else 2)
    return pl.pallas_call(
        functools.partial(_attn_kernel, n_seg=n_seg, n_kb=n_kb),
        grid=(n_seq, N_PAIRS, n_seg * n_qb),
        in_specs=in_specs,
        out_specs=pl.BlockSpec((1, ATT_BQ, HEAD_PAD), lambda s, p, i: (s, i, p)),
        out_shape=jax.ShapeDtypeStruct((n_seq, n_seg * lr, N_HEADS * V_DIM), F32),
        scratch_shapes=scratch,
        compiler_params=pltpu.CompilerParams(
            dimension_semantics=("arbitrary", "arbitrary", "arbitrary"),
            vmem_limit_bytes=_vmem_limit(kv_bytes + (24 << 20))),
        name="attn",
    )(qt, k, vt)


def _cstep(xre, xim, are, aim, bre, bim):
    return are * xre - aim * xim + bre, are * xim + aim * xre + bim


def _ssm_kernel(uf_ref, ub_ref, bmat_ref, cmat_ref, a_ref, init_ref, *rest, tt, with_y):
    if with_y:
        yf_ref, yb_ref, fin_ref, state_ref, bu_ref, xs_ref = rest
    else:
        fin_ref, state_ref, bu_ref = rest
    j = pl.program_id(0)
    rows = N_GROUPS * tt
    n_pairs = tt // 2
    S = STATES_PER_BLOCK

    @pl.when(j == 0)
    def _():
        state_ref[...] = init_ref[...]

    for q in range(SSM_LANE_BLOCKS):
        lanes = slice(q * LANES, (q + 1) * LANES)
        slot = q % 2
        bu_ref[slot, 0] = _dot(uf_ref[:, lanes].astype(BF16), bmat_ref[0, q])
        bu_ref[slot, 1] = _dot(ub_ref[:, lanes].astype(BF16), bmat_ref[1, q])
        a = [jnp.broadcast_to(a_ref[d, q], (N_GROUPS, 2 * S)) for d in range(2)]
        are = [a[d][:, :S] for d in range(2)]
        aim = [a[d][:, S:] for d in range(2)]
        fwd = (state_ref[0, q, :, :S], state_ref[0, q, :, S:])
        bwd = (state_ref[1, q, :, :S], state_ref[1, q, :, S:])
        for k in range(n_pairs):
            r0 = k * 2 * N_GROUPS
            blk = bu_ref[slot, 0, r0:r0 + 2 * N_GROUPS, :]
            f1 = _cstep(*fwd, are[0], aim[0], blk[:N_GROUPS, :S], blk[:N_GROUPS, S:])
            fwd = _cstep(*f1, are[0], aim[0], blk[N_GROUPS:, :S], blk[N_GROUPS:, S:])
            r1 = rows - (k + 1) * 2 * N_GROUPS
            blk = bu_ref[slot, 1, r1:r1 + 2 * N_GROUPS, :]
            b1 = _cstep(*bwd, are[1], aim[1], blk[N_GROUPS:, :S], blk[N_GROUPS:, S:])
            bwd = _cstep(*b1, are[1], aim[1], blk[:N_GROUPS, :S], blk[:N_GROUPS, S:])
            if with_y:
                xs_ref[slot, 0, r0:r0 + 2 * N_GROUPS, :] = jnp.concatenate(
                    [jnp.concatenate(f1, axis=1), jnp.concatenate(fwd, axis=1)], axis=0).astype(BF16)
                xs_ref[slot, 1, r1:r1 + 2 * N_GROUPS, :] = jnp.concatenate(
                    [jnp.concatenate(bwd, axis=1), jnp.concatenate(b1, axis=1)], axis=0).astype(BF16)
        state_ref[0, q] = jnp.concatenate(fwd, axis=1)
        state_ref[1, q] = jnp.concatenate(bwd, axis=1)
        if with_y:
            yf_ref[:, lanes] = _dot(xs_ref[slot, 0], cmat_ref[0, q])
            yb_ref[:, lanes] = _dot(xs_ref[slot, 1], cmat_ref[1, q])

    @pl.when(j == pl.num_programs(0) - 1)
    def _():
        fin_ref[...] = state_ref[...]


def _ssm_call(u, bmat, cmat, a, init, with_y):
    n_rows = u.shape[0]
    tt = SSM_TT
    rows = N_GROUPS * tt
    n_t = n_rows // rows
    full = lambda arr: pl.BlockSpec(arr.shape, lambda j: (0,) * arr.ndim)
    in_specs = [
        pl.BlockSpec((rows, SSM_W), lambda j: (j, 0)),
        pl.BlockSpec((rows, SSM_W), lambda j: (n_t - 1 - j, 0)),
        full(bmat), full(cmat), full(a), full(init),
    ]
    state_shape = (2, SSM_LANE_BLOCKS, N_GROUPS, 2 * STATES_PER_BLOCK)
    out_shape = [jax.ShapeDtypeStruct(state_shape, F32)]
    out_specs = [pl.BlockSpec(state_shape, lambda j: (0, 0, 0, 0))]
    scratch = [pltpu.VMEM(state_shape, F32), pltpu.VMEM((2, 2, rows, 2 * STATES_PER_BLOCK), F32)]
    if with_y:
        out_shape = [jax.ShapeDtypeStruct((n_rows, SSM_W), F32)] * 2 + out_shape
        out_specs = [pl.BlockSpec((rows, SSM_W), lambda j: (j, 0)),
                     pl.BlockSpec((rows, SSM_W), lambda j: (n_t - 1 - j, 0))] + out_specs
        scratch.append(pltpu.VMEM((2, 2, rows, 2 * STATES_PER_BLOCK), BF16))
    return pl.pallas_call(
        functools.partial(_ssm_kernel, tt=tt, with_y=with_y),
        grid=(n_t,),
        in_specs=in_specs,
        out_specs=out_specs,
        out_shape=out_shape,
        scratch_shapes=scratch,
        compiler_params=pltpu.CompilerParams(
            dimension_semantics=("arbitrary",), vmem_limit_bytes=_vmem_limit(48 << 20)),
        name="ssm" if with_y else "ssm_ends",
    )(u, u, bmat, cmat, a, init)


def _carry_kernel(e_ref, as_ref, i_ref):
    S = STATES_PER_BLOCK
    row = lax.broadcasted_iota(jnp.int32, (N_GROUPS, S), 0)
    for d in range(2):
        first = 0 if d == 0 else N_GROUPS - 1
        shift = 1 if d == 0 else N_GROUPS - 1
        for q in range(SSM_LANE_BLOCKS):
            e = e_ref[d, q]
            a = jnp.broadcast_to(as_ref[d, q], (N_GROUPS, 2 * S))
            ere, eim, are, aim = e[:, :S], e[:, S:], a[:, :S], a[:, S:]
            ire = jnp.zeros((N_GROUPS, S), F32)
            iim = jnp.zeros((N_GROUPS, S), F32)
            for _ in range(N_GROUPS - 1):
                tre, tim = _cstep(ire, iim, are, aim, ere, eim)
                ire = jnp.where(row == first, 0.0, pltpu.roll(tre, shift, 0))
                iim = jnp.where(row == first, 0.0, pltpu.roll(tim, shift, 0))
            i_ref[d, q] = jnp.concatenate([ire, iim], axis=1)


def _carry_call(ends, a_seg):
    return pl.pallas_call(
        _carry_kernel,
        out_shape=jax.ShapeDtypeStruct(ends.shape, F32),
        name="ssm_carry",
    )(ends, a_seg)


def _ssm_post_kernel(yf_ref, yb_ref, u_ref, dsk_ref, wglu_ref, bglu_ref, sg_ref, permt_ref, o_ref, *, tt):
    y = yf_ref[...] + yb_ref[...] + dsk_ref[...] * u_ref[...]
    g = jax.nn.gelu(y)
    z = _dot(g.astype(BF16), wglu_ref[...]) + bglu_ref[...]
    s = g * jax.nn.sigmoid(z)
    sn = _rms(s, sg_ref[...]).astype(BF16)
    permt = permt_ref[...]
    sub = N_GROUPS * PERM_TT
    for k in range(tt // PERM_TT):
        blk = _dot(permt, sn[k * sub:(k + 1) * sub]).astype(BF16)
        o_ref[:, k * PERM_TT:(k + 1) * PERM_TT, :] = blk.reshape(N_GROUPS, PERM_TT, SSM_W)


def _ssm_post_call(yf, yb, u, w, permt):
    n_rows = u.shape[0]
    tt = PRE_TT
    rows = N_GROUPS * tt
    n_t = n_rows // rows
    lr = n_rows // N_GROUPS
    full = lambda a: pl.BlockSpec(a.shape, lambda j: (0,) * a.ndim)
    row_spec = pl.BlockSpec((rows, SSM_W), lambda j: (j, 0))
    weights = [w["dsk"], w["wglu"], w["bglu"], w["sg"], permt]
    return pl.pallas_call(
        functools.partial(_ssm_post_kernel, tt=tt),
        grid=(n_t,),
        in_specs=[row_spec, row_spec, row_spec] + [full(a) for a in weights],
        out_specs=pl.BlockSpec((N_GROUPS, tt, SSM_W), lambda j: (0, j, 0)),
        out_shape=jax.ShapeDtypeStruct((N_GROUPS, lr, SSM_W), BF16),
        compiler_params=pltpu.CompilerParams(
            dimension_semantics=("arbitrary",), vmem_limit_bytes=_vmem_limit(40 << 20)),
        name="ssm_post",
    )(yf, yb, u, *weights)


def _post_kernel(x_ref, a_ref, sn_ref, ag_ref, wo_ref, n2_ref, w1_ref, w2_ref, fg_ref, o_ref):
    an = _rms(a_ref[0], ag_ref[...]).astype(BF16)
    mixed = jnp.concatenate([an, sn_ref[0]], axis=-1)
    x1 = x_ref[0] + _dot(mixed, wo_ref[...])
    h2 = _rms(x1, n2_ref[...]).astype(BF16)
    acc = jnp.zeros_like(x1)
    for c in range(D_FF // FF_CHUNK):
        hid = _dot(h2, w1_ref[:, c * FF_CHUNK:(c + 1) * FF_CHUNK])
        hid = jnp.square(jnp.maximum(hid, 0.0)).astype(BF16)
        acc = acc + _dot(hid, w2_ref[c * FF_CHUNK:(c + 1) * FF_CHUNK, :])
    o_ref[0] = _rms(x1 + acc, fg_ref[...])


def _post_call(x, a, sn, w):
    nb, lr, _ = x.shape
    tm = POST_TM
    const = lambda arr: pl.BlockSpec(arr.shape, lambda b, i: (0,) * arr.ndim, pipeline_mode=pl.Buffered(1))
    weights = [w["ag"], w["wo"], w["n2"], w["w1"], w["w2"], w["fg"]]
    return pl.pallas_call(
        _post_kernel,
        grid=(nb, lr // tm),
        in_specs=[
            pl.BlockSpec((1, tm, D_MODEL), lambda b, i: (b, i, 0)),
            pl.BlockSpec((1, tm, N_HEADS * V_DIM), lambda b, i: (b, i, 0)),
            pl.BlockSpec((1, tm, SSM_W), lambda b, i: (b, i, 0)),
        ] + [const(arr) for arr in weights],
        out_specs=pl.BlockSpec((1, tm, D_MODEL), lambda b, i: (b, i, 0)),
        out_shape=jax.ShapeDtypeStruct((nb, lr, D_MODEL), F32),
        compiler_params=pltpu.CompilerParams(
            dimension_semantics=("arbitrary", "arbitrary"), vmem_limit_bytes=_vmem_limit(48 << 20)),
        name="post",
    )(x, a, sn, *weights)


def _rope_tables(length):
    inv = ROPE_THETA ** (-jnp.arange(0, QK_ROPE, 2, dtype=F32) / QK_ROPE)
    zeros = lambda n: jnp.zeros((n,), F32)
    inv_lanes = jnp.concatenate([zeros(QK_NOPE), inv, inv, zeros(HEAD_PAD - QK_NOPE - QK_ROPE)])
    n_hi = -(-length // ROPE_BLOCK)
    a_hi = (jnp.arange(n_hi, dtype=F32) * ROPE_BLOCK)[:, None] * inv_lanes
    a_lo = jnp.arange(ROPE_BLOCK, dtype=F32)[:, None] * inv_lanes
    ch, sh = jnp.cos(a_hi)[:, None], jnp.sin(a_hi)[:, None]
    cl, sl = jnp.cos(a_lo)[None], jnp.sin(a_lo)[None]
    cos_t = (ch * cl - sh * sl).reshape(n_hi * ROPE_BLOCK, HEAD_PAD)[:length]
    sin_t = (sh * cl + ch * sl).reshape(n_hi * ROPE_BLOCK, HEAD_PAD)[:length]
    return cos_t, sin_t


def _rot_half_cols(w):
    half = QK_ROPE // 2
    return jnp.concatenate([-w[..., half:], w[..., :half]], axis=-1)


def _pack_weights(norm1_g, w_in, q_norm_g, w_uq, kv_norm_g, w_ukv, d_skip, w_glu, b_glu,
                  attn_out_g, ssm_out_g, w_out, norm2_g, w_mlp1, w_mlp2, final_g):
    row = lambda g: g.reshape(1, -1).astype(F32)
    o = Q_RANK + KV_RANK
    w_kr = w_in[:, o:o + QK_ROPE]
    zk = jnp.zeros((D_MODEL, QK_NOPE), F32)
    zp = jnp.zeros((D_MODEL, HEAD_PAD - QK_NOPE - QK_ROPE), F32)
    wa = jnp.concatenate([w_in[:, :o], zk, w_kr, zp, zk, _rot_half_cols(w_kr), zp], axis=1)
    wu = w_in[:, o + QK_ROPE:]
    wq = w_uq.reshape(Q_RANK, N_HEADS, QK_NOPE + QK_ROPE)
    zq = jnp.zeros((Q_RANK, N_HEADS, HEAD_PAD - QK_NOPE - QK_ROPE), F32)
    wqa = jnp.concatenate([wq, zq], axis=-1).reshape(Q_RANK, N_HEADS * HEAD_PAD)
    wqb = jnp.concatenate([jnp.zeros((Q_RANK, N_HEADS, QK_NOPE), F32), _rot_half_cols(wq[..., QK_NOPE:]), zq],
                          axis=-1).reshape(Q_RANK, N_HEADS * HEAD_PAD)
    wkv = w_ukv.reshape(KV_RANK, N_HEADS, QK_NOPE + V_DIM)
    zh = jnp.zeros((KV_RANK, N_HEADS, HEAD_PAD - QK_NOPE), F32)
    wk = jnp.concatenate([wkv[..., :QK_NOPE], zh], axis=-1).reshape(KV_RANK, N_HEADS * HEAD_PAD)
    zv = jnp.zeros((KV_RANK, N_HEADS, HEAD_PAD - V_DIM), F32)
    wv = jnp.concatenate([wkv[..., QK_NOPE:], zv], axis=-1).reshape(KV_RANK, N_HEADS * HEAD_PAD)
    bf = lambda a: a.astype(BF16)
    return dict(
        n1=row(norm1_g), wa=bf(wa), wu=bf(wu), qg=row(q_norm_g), wqa=bf(wqa), wqb=bf(wqb),
        kvg=row(kv_norm_g), wk=bf(wk), wv=bf(wv),
        dsk=row(d_skip), wglu=bf(w_glu), bglu=row(b_glu), sg=row(ssm_out_g),
        ag=row(attn_out_g), wo=bf(w_out), n2=row(norm2_g), w1=bf(w_mlp1), w2=bf(w_mlp2), fg=row(final_g))


def _pack_ssm(lam_re, lam_im, log_dt, b_re, b_im, c_re, c_im, seg_len):
    cmul = lambda xr, xi, yr, yi: (xr * yr - xi * yi, xr * yi + xi * yr)
    lam_re, lam_im = lam_re.astype(F32), lam_im.astype(F32)
    dt = jnp.exp(log_dt.astype(F32))[..., None]
    mag = jnp.exp(lam_re * dt)
    a_re, a_im = mag * jnp.cos(lam_im * dt), mag * jnp.sin(lam_im * dt)
    den = lam_re * lam_re + lam_im * lam_im
    k_re = ((a_re - 1.0) * lam_re + a_im * lam_im) / den
    k_im = (a_im * lam_re - (a_re - 1.0) * lam_im) / den
    bb_re, bb_im = cmul(k_re[..., None], k_im[..., None], b_re.astype(F32), b_im.astype(F32))
    p_re, p_im = a_re, a_im
    s_re, s_im = jnp.ones_like(a_re), jnp.zeros_like(a_re)
    n = seg_len
    while n:
        if n & 1:
            s_re, s_im = cmul(s_re, s_im, p_re, p_im)
        p_re, p_im = cmul(p_re, p_im, p_re, p_im)
        n >>= 1
    eye = jnp.eye(GROUPS_PER_BLOCK, dtype=F32)
    nq, gb = SSM_LANE_BLOCKS, GROUPS_PER_BLOCK

    def b_block(part):
        p = part.reshape(2, nq, gb, SSM_STATE, SSM_GROUP)
        return jnp.einsum("dqgnh,gk->dqghkn", p, eye).reshape(2, nq, LANES, STATES_PER_BLOCK)

    def c_block(part):
        p = part.reshape(2, nq, gb, SSM_GROUP, SSM_STATE)
        return jnp.einsum("dqghn,gk->dqgnkh", p, eye).reshape(2, nq, STATES_PER_BLOCK, LANES)

    def a_block(zr, zi):
        shape = (2, nq, 1, STATES_PER_BLOCK)
        return jnp.concatenate([zr.reshape(shape), zi.reshape(shape)], axis=-1)

    bmat = jnp.concatenate([b_block(bb_re), b_block(bb_im)], axis=-1).astype(BF16)
    cmat = jnp.concatenate([c_block(c_re.astype(F32)), -c_block(c_im.astype(F32))], axis=-2).astype(BF16)
    return bmat, cmat, a_block(a_re, a_im), a_block(s_re, s_im)


def _perm_matrix(tt):
    rows = N_GROUPS * tt
    dst = np.arange(rows)
    src = (dst % N_GROUPS) * tt + dst // N_GROUPS
    p = np.zeros((rows, rows), np.float32)
    p[dst, src] = 1.0
    return jnp.asarray(p, BF16)


def _trunk(x, rope, segmented, w, ssm):
    nb, lr, _ = x.shape
    bmat, cmat, a_blk, a_seg = ssm
    cos_t, sin_t = rope
    perm = _perm_matrix(PERM_TT)
    qt, k, vt, u = _pre_call(x, cos_t, sin_t, w, perm)
    seq_seg = (lambda a: a[None]) if segmented else (lambda a: a[:, None])
    att = _attn_call(seq_seg(qt), seq_seg(k), seq_seg(vt)).reshape(nb, lr, N_HEADS * V_DIM)
    init = jnp.zeros((2, SSM_LANE_BLOCKS, N_GROUPS, 2 * STATES_PER_BLOCK), F32)
    if segmented:
        (ends,) = _ssm_call(u, bmat, cmat, a_blk, init, with_y=False)
        init = _carry_call(ends, a_seg)
    yf, yb, _ = _ssm_call(u, bmat, cmat, a_blk, init, with_y=True)
    sn = _ssm_post_call(yf, yb, u, w, perm.T)
    return _post_call(x, att, sn, w)


def kernel(x_prompt, x_sample, norm1_g, w_in, q_norm_g, w_uq, kv_norm_g, w_ukv, lam_re, lam_im, log_dt,
           b_re, b_im, c_re, c_im, d_skip, w_glu, b_glu, attn_out_g, ssm_out_g, w_out, norm2_g, w_mlp1,
           w_mlp2, final_g):
    assert norm1_g.shape[0] == 1, "single-layer trunk"
    w = _pack_weights(norm1_g[0], w_in[0], q_norm_g[0], w_uq[0], kv_norm_g[0], w_ukv[0], d_skip[0], w_glu[0],
                      b_glu[0], attn_out_g[0], ssm_out_g[0], w_out[0], norm2_g[0], w_mlp1[0], w_mlp2[0], final_g)
    bp, lp, _ = x_prompt.shape
    bs, ls, _ = x_sample.shape
    assert bp == N_GROUPS and bs == 1 and ls % N_GROUPS == 0
    seg = ls // N_GROUPS
    ssm = _pack_ssm(lam_re[0], lam_im[0], log_dt[0], b_re[0], b_im[0], c_re[0], c_im[0], seg)
    cos_t, sin_t = _rope_tables(max(lp, ls))
    rope_p = (cos_t[None, :lp], sin_t[None, :lp])
    rope_s = (cos_t[:ls].reshape(N_GROUPS, seg, HEAD_PAD), sin_t[:ls].reshape(N_GROUPS, seg, HEAD_PAD))
    y_prompt = _trunk(x_prompt, rope_p, False, w, ssm)
    y_sample = _trunk(x_sample.reshape(N_GROUPS, seg, D_MODEL), rope_s, True, w, ssm)
    return y_prompt, y_sample.reshape(bs, ls, D_MODEL)
```

```python
import functools
import math

import jax
import jax.numpy as jnp
import numpy as np
from jax import lax
from jax.experimental import pallas as pl
from jax.experimental.pallas import tpu as pltpu

F32 = jnp.float32
BF16 = jnp.bfloat16

D_MODEL = 1024
N_HEADS = 8
QK_NOPE = 64
QK_ROPE = 32
V_DIM = 64
Q_RANK = 256
KV_RANK = 128
SSM_W = 512
SSM_GROUP = 16
SSM_GROUPS = 32
SSM_STATE = 64
D_FF = 4096
EPS = 1e-6
ROPE_THETA = 10000.0
ROPE_BLOCK = 128

LANES = 128
SUBLANES = 8
VMEM_LIMIT_CAP = 60000 * 1024

N_GROUPS = SUBLANES
HEAD_PAD = LANES
BF16_SUBLANES = 2 * SUBLANES
VT_ROWS = -(-(V_DIM + 1) // BF16_SUBLANES) * BF16_SUBLANES
N_PAIRS = N_HEADS // 2
SSM_LANE_BLOCKS = SSM_W // LANES
GROUPS_PER_BLOCK = LANES // SSM_GROUP
STATES_PER_BLOCK = GROUPS_PER_BLOCK * SSM_STATE

PRE_TT = 128
PERM_TT = 32
ATT_BQ = 512
ATT_BK = 256
ATT_UNROLL = 10
Q_SCALE = math.log2(math.e) / math.sqrt(QK_NOPE + QK_ROPE)
SSM_TT = 64
POST_TM = 512
FF_CHUNK = 1024


def _nbytes(shape, dtype):
    return math.prod(shape) * jnp.dtype(dtype).itemsize


def _vmem_limit(pipelined, resident, temporaries):
    total = lambda items: sum(_nbytes(s, d) for s, d in items)
    return int(min(VMEM_LIMIT_CAP, 2 * total(pipelined) + total(resident) + total(temporaries)))


def _rms(x, g):
    return x * lax.rsqrt(jnp.mean(x * x, axis=-1, keepdims=True) + EPS) * g


def _dot(a, b):
    return jnp.dot(a, b, preferred_element_type=F32)


def _pre_kernel(x_ref, cos_ref, sin_ref, n1_ref, wa_ref, wu_ref, qg_ref, wqa_ref, wqb_ref,
                kvg_ref, wk_ref, wv_ref, perm_ref, qt_ref, k_ref, vt_ref, u_ref, *, tt):
    rows = N_GROUPS * tt
    x = x_ref[...].reshape(rows, D_MODEL)
    h = _rms(x, n1_ref[...]).astype(BF16)
    pa = _dot(h, wa_ref[...])
    u = _dot(h, wu_ref[...])
    cqn = _rms(pa[:, :Q_RANK], qg_ref[...]).astype(BF16)
    ckvn = _rms(pa[:, Q_RANK:Q_RANK + KV_RANK], kvg_ref[...]).astype(BF16)
    cos = jnp.broadcast_to(cos_ref[...], (N_GROUPS, tt, HEAD_PAD)).reshape(rows, HEAD_PAD)
    sin = jnp.broadcast_to(sin_ref[...], (N_GROUPS, tt, HEAD_PAD)).reshape(rows, HEAD_PAD)
    o = Q_RANK + KV_RANK
    k_rope = pa[:, o:o + HEAD_PAD] * cos + pa[:, o + HEAD_PAD:o + 2 * HEAD_PAD] * sin
    qa = _dot(cqn, wqa_ref[...])
    qb = _dot(cqn, wqb_ref[...])
    k = _dot(ckvn, wk_ref[...])
    v = _dot(ckvn, wv_ref[...])
    ones_col = (lax.broadcasted_iota(jnp.int32, (1, HEAD_PAD), 1) == V_DIM).astype(F32)
    for hd in range(N_HEADS):
        sl = slice(hd * HEAD_PAD, (hd + 1) * HEAD_PAD)
        qht = ((qa[:, sl] * cos + qb[:, sl] * sin) * Q_SCALE).T
        vht = (v[:, sl] + ones_col).T
        for b in range(N_GROUPS):
            qt_ref[b, hd] = qht[:, b * tt:(b + 1) * tt].astype(BF16)
            vt_ref[b, hd, 0] = vht[:VT_ROWS, b * tt:(b + 1) * tt].astype(BF16)
        k_ref[:, hd] = (k[:, sl] + k_rope).astype(BF16).reshape(N_GROUPS, tt, HEAD_PAD)
    u_hi = u.astype(BF16)
    u_lo = (u - u_hi.astype(F32)).astype(BF16)
    perm = perm_ref[...]
    sub = N_GROUPS * PERM_TT
    for s in range(tt // PERM_TT):
        pick = lambda a: jnp.concatenate(
            [a[b * tt + s * PERM_TT:b * tt + (s + 1) * PERM_TT] for b in range(N_GROUPS)], axis=0)
        u_ref[s * sub:(s + 1) * sub, :] = _dot(perm, pick(u_hi)) + _dot(perm, pick(u_lo))


def _pre_call(x, cos_t, sin_t, w, perm):
    nb, lr, _ = x.shape
    tt = PRE_TT
    rows = nb * tt
    n_t = lr // tt
    tab_nb = cos_t.shape[0]
    kt_sub = ATT_BK // tt
    wa_cols = w["wa"].shape[1]
    full = lambda a: pl.BlockSpec(a.shape, lambda j: (0,) * a.ndim)
    weights = [w["n1"], w["wa"], w["wu"], w["qg"], w["wqa"], w["wqb"], w["kvg"], w["wk"], w["wv"], perm]
    in_specs = [
        pl.BlockSpec((nb, tt, D_MODEL), lambda j: (0, j, 0)),
        pl.BlockSpec((tab_nb, tt, HEAD_PAD), lambda j: (0, j, 0)),
        pl.BlockSpec((tab_nb, tt, HEAD_PAD), lambda j: (0, j, 0)),
    ] + [full(a) for a in weights]
    out_shape = [
        jax.ShapeDtypeStruct((nb, N_HEADS, HEAD_PAD, lr), BF16),
        jax.ShapeDtypeStruct((nb, N_HEADS, lr, HEAD_PAD), BF16),
        jax.ShapeDtypeStruct((nb, N_HEADS, lr // ATT_BK, VT_ROWS, ATT_BK), BF16),
        jax.ShapeDtypeStruct((lr * nb, SSM_W), F32),
    ]
    out_specs = [
        pl.BlockSpec((nb, N_HEADS, HEAD_PAD, tt), lambda j: (0, 0, 0, j)),
        pl.BlockSpec((nb, N_HEADS, tt, HEAD_PAD), lambda j: (0, 0, j, 0)),
        pl.BlockSpec((nb, N_HEADS, 1, VT_ROWS, tt), lambda j: (0, 0, j // kt_sub, 0, j % kt_sub)),
        pl.BlockSpec((rows, SSM_W), lambda j: (j, 0)),
    ]
    blocks = ([((nb, tt, D_MODEL), F32)] + [((tab_nb, tt, HEAD_PAD), F32)] * 2
              + [(a.shape, a.dtype) for a in weights]
              + [(spec.block_shape, o.dtype) for spec, o in zip(out_specs, out_shape)])
    temps = [((rows, wa_cols + SSM_W + 4 * N_HEADS * HEAD_PAD), F32)]
    return pl.pallas_call(
        functools.partial(_pre_kernel, tt=tt),
        grid=(n_t,),
        in_specs=in_specs,
        out_specs=out_specs,
        out_shape=out_shape,
        compiler_params=pltpu.CompilerParams(
            dimension_semantics=("arbitrary",), vmem_limit_bytes=_vmem_limit(blocks, [], temps)),
        name="pre",
    )(x, cos_t, sin_t, *weights)


def _attn_kernel(qt_ref, k_ref, vt_ref, o_ref, s_ref, p_ref, alpha_ref, m_ref, acc_ref, *, n_seg, n_kb):
    bq, bk = ATT_BQ, ATT_BK
    m_ref[...] = jnp.full(m_ref.shape, -0.5 * float(np.finfo(np.float32).max), F32)
    acc_ref[...] = jnp.zeros(acc_ref.shape, F32)

    n = n_seg * n_kb

    def scores(c, slot):
        r0 = (c % n_kb) * bk
        if not isinstance(c, int):
            r0 = pl.multiple_of(r0, bk)
        for j in range(2):
            s_ref[slot, j, :, :bq] = _dot(k_ref[0, c // n_kb, j, pl.ds(r0, bk), :], qt_ref[0, 0, j])

    def softmax(slot):
        for j in range(2):
            for lt in range(bq // LANES):
                lanes = slice(lt * LANES, (lt + 1) * LANES)
                blk = s_ref[slot, j, :, lanes]
                m_old = m_ref[j, :, lanes]
                m_new = jnp.maximum(m_old, jnp.max(blk, axis=0, keepdims=True))
                alpha_ref[slot, j, :, lanes] = jnp.exp2(m_old - m_new)
                m_ref[j, :, lanes] = m_new
                p_ref[slot, j, :, lanes] = jnp.exp2(blk - m_new).astype(BF16)

    def weighted_values(c, slot):
        for j in range(2):
            pv = _dot(vt_ref[0, c // n_kb, j, c % n_kb], p_ref[slot, j, :, :bq])
            acc_ref[j] = acc_ref[j] * alpha_ref[slot, j] + pv

    steady = max(n - 2, 0)
    unroll = min(ATT_UNROLL, max(steady - steady % 2, 2))

    def stage(t, parity, do_scores=True, do_softmax=True, do_values=True):
        if do_values:
            weighted_values(t - 2, parity)
        if do_softmax:
            softmax(1 - parity)
        if do_scores:
            scores(t, parity)

    for t in range(2):
        stage(t, t % 2, t < n, 1 <= t <= n, False)
    n_iter = steady // unroll
    if n_iter:
        def body(k, _):
            for i in range(unroll):
                stage(2 + unroll * k + i, i % 2)
            return 0
        lax.fori_loop(0, n_iter, body, 0)
    for t in range(2 + unroll * n_iter, n + 2):
        stage(t, t % 2, t < n, t <= n, True)
    outs = [acc_ref[j][:V_DIM] / acc_ref[j][V_DIM:V_DIM + 1] for j in range(2)]
    o_ref[0] = jnp.concatenate(outs, axis=0).T


def _attn_call(qt, k, vt):
    n_seq, n_seg, _, lr, _ = k.shape
    n_kb = lr // ATT_BK
    n_qb = lr // ATT_BQ
    resident = dict(pipeline_mode=pl.Buffered(1)) if n_seg > 1 else {}
    scratch = [
        pltpu.VMEM((2, 2, ATT_BK, ATT_BQ + LANES), F32),
        pltpu.VMEM((2, 2, ATT_BK, ATT_BQ + LANES), BF16),
        pltpu.VMEM((2, 2, 1, ATT_BQ), F32),
        pltpu.VMEM((2, 1, ATT_BQ), F32),
        pltpu.VMEM((2, VT_ROWS, ATT_BQ), F32),
    ]
    in_specs = [
        pl.BlockSpec((1, 1, 2, HEAD_PAD, ATT_BQ), lambda s, p, i: (s, i // n_qb, p, 0, i % n_qb)),
        pl.BlockSpec((1, n_seg, 2, lr, HEAD_PAD), lambda s, p, i: (s, 0, p, 0, 0), **resident),
        pl.BlockSpec((1, n_seg, 2, n_kb, VT_ROWS, ATT_BK), lambda s, p, i: (s, 0, p, 0, 0, 0), **resident),
    ]
    kv_blocks = [((n_seg, 2, lr, HEAD_PAD), BF16), ((n_seg, 2, lr, VT_ROWS), BF16)]
    io_blocks = [((2, HEAD_PAD, ATT_BQ), BF16), ((ATT_BQ, HEAD_PAD), F32)]
    scratch_items = [(s.shape, s.dtype) for s in scratch]
    temps = scratch_items[:2] * 6
    limit = _vmem_limit(io_blocks + ([] if n_seg > 1 else kv_blocks),
                        scratch_items + (kv_blocks if n_seg > 1 else []), temps)
    return pl.pallas_call(
        functools.partial(_attn_kernel, n_seg=n_seg, n_kb=n_kb),
        grid=(n_seq, N_PAIRS, n_seg * n_qb),
        in_specs=in_specs,
        out_specs=pl.BlockSpec((1, ATT_BQ, HEAD_PAD), lambda s, p, i: (s, i, p)),
        out_shape=jax.ShapeDtypeStruct((n_seq, n_seg * lr, N_HEADS * V_DIM), F32),
        scratch_shapes=scratch,
        compiler_params=pltpu.CompilerParams(
            dimension_semantics=("arbitrary", "arbitrary", "arbitrary"),
            vmem_limit_bytes=limit),
        name="attn",
    )(qt, k, vt)


def _cstep(xre, xim, are, aim, bre, bim):
    return are * xre - aim * xim + bre, are * xim + aim * xre + bim


def _ssm_kernel(uf_ref, ub_ref, bmat_ref, cmat_ref, a_ref, init_ref, *rest, tt, with_y):
    if with_y:
        yf_ref, yb_ref, fin_ref, state_ref, bu_ref, xs_ref = rest
    else:
        fin_ref, state_ref, bu_ref = rest
    j = pl.program_id(0)
    rows = N_GROUPS * tt
    n_pairs = tt // 2
    S = STATES_PER_BLOCK

    @pl.when(j == 0)
    def _():
        state_ref[...] = init_ref[...]

    for q in range(SSM_LANE_BLOCKS):
        lanes = slice(q * LANES, (q + 1) * LANES)
        slot = q % 2
        bu_ref[slot, 0] = _dot(uf_ref[:, lanes].astype(BF16), bmat_ref[0, q])
        bu_ref[slot, 1] = _dot(ub_ref[:, lanes].astype(BF16), bmat_ref[1, q])
        a = [jnp.broadcast_to(a_ref[d, q], (N_GROUPS, 2 * S)) for d in range(2)]
        are = [a[d][:, :S] for d in range(2)]
        aim = [a[d][:, S:] for d in range(2)]
        fwd = (state_ref[0, q, :, :S], state_ref[0, q, :, S:])
        bwd = (state_ref[1, q, :, :S], state_ref[1, q, :, S:])
        for k in range(n_pairs):
            r0 = k * 2 * N_GROUPS
            blk = bu_ref[slot, 0, r0:r0 + 2 * N_GROUPS, :]
            f1 = _cstep(*fwd, are[0], aim[0], blk[:N_GROUPS, :S], blk[:N_GROUPS, S:])
            fwd = _cstep(*f1, are[0], aim[0], blk[N_GROUPS:, :S], blk[N_GROUPS:, S:])
            r1 = rows - (k + 1) * 2 * N_GROUPS
            blk = bu_ref[slot, 1, r1:r1 + 2 * N_GROUPS, :]
            b1 = _cstep(*bwd, are[1], aim[1], blk[N_GROUPS:, :S], blk[N_GROUPS:, S:])
            bwd = _cstep(*b1, are[1], aim[1], blk[:N_GROUPS, :S], blk[:N_GROUPS, S:])
            if with_y:
                xs_ref[slot, 0, r0:r0 + 2 * N_GROUPS, :] = jnp.concatenate(
                    [jnp.concatenate(f1, axis=1), jnp.concatenate(fwd, axis=1)], axis=0).astype(BF16)
                xs_ref[slot, 1, r1:r1 + 2 * N_GROUPS, :] = jnp.concatenate(
                    [jnp.concatenate(bwd, axis=1), jnp.concatenate(b1, axis=1)], axis=0).astype(BF16)
        state_ref[0, q] = jnp.concatenate(fwd, axis=1)
        state_ref[1, q] = jnp.concatenate(bwd, axis=1)
        if with_y:
            yf_ref[:, lanes] = _dot(xs_ref[slot, 0], cmat_ref[0, q])
            yb_ref[:, lanes] = _dot(xs_ref[slot, 1], cmat_ref[1, q])

    @pl.when(j == pl.num_programs(0) - 1)
    def _():
        fin_ref[...] = state_ref[...]


def _ssm_call(u, bmat, cmat, a, init, with_y):
    n_rows = u.shape[0]
    tt = SSM_TT
    rows = N_GROUPS * tt
    n_t = n_rows // rows
    full = lambda arr: pl.BlockSpec(arr.shape, lambda j: (0,) * arr.ndim)
    in_specs = [
        pl.BlockSpec((rows, SSM_W), lambda j: (j, 0)),
        pl.BlockSpec((rows, SSM_W), lambda j: (n_t - 1 - j, 0)),
        full(bmat), full(cmat), full(a), full(init),
    ]
    state_shape = (2, SSM_LANE_BLOCKS, N_GROUPS, 2 * STATES_PER_BLOCK)
    out_shape = [jax.ShapeDtypeStruct(state_shape, F32)]
    out_specs = [pl.BlockSpec(state_shape, lambda j: (0, 0, 0, 0))]
    scratch = [pltpu.VMEM(state_shape, F32), pltpu.VMEM((2, 2, rows, 2 * STATES_PER_BLOCK), F32)]
    if with_y:
        out_shape = [jax.ShapeDtypeStruct((n_rows, SSM_W), F32)] * 2 + out_shape
        out_specs = [pl.BlockSpec((rows, SSM_W), lambda j: (j, 0)),
                     pl.BlockSpec((rows, SSM_W), lambda j: (n_t - 1 - j, 0))] + out_specs
        scratch.append(pltpu.VMEM((2, 2, rows, 2 * STATES_PER_BLOCK), BF16))
    blocks = ([((rows, SSM_W), F32)] * 2 + [(arr.shape, arr.dtype) for arr in (bmat, cmat, a, init)]
              + [(spec.block_shape, o.dtype) for spec, o in zip(out_specs, out_shape)])
    scratch_items = [(s.shape, s.dtype) for s in scratch]
    temps = [((2, rows, 2 * STATES_PER_BLOCK), F32)] * 2
    return pl.pallas_call(
        functools.partial(_ssm_kernel, tt=tt, with_y=with_y),
        grid=(n_t,),
        in_specs=in_specs,
        out_specs=out_specs,
        out_shape=out_shape,
        scratch_shapes=scratch,
        compiler_params=pltpu.CompilerParams(
            dimension_semantics=("arbitrary",), vmem_limit_bytes=_vmem_limit(blocks, scratch_items, temps)),
        name="ssm" if with_y else "ssm_ends",
    )(u, u, bmat, cmat, a, init)


def _carry_kernel(e_ref, as_ref, i_ref):
    S = STATES_PER_BLOCK
    row = lax.broadcasted_iota(jnp.int32, (N_GROUPS, S), 0)
    for d in range(2):
        first = 0 if d == 0 else N_GROUPS - 1
        shift = 1 if d == 0 else N_GROUPS - 1
        for q in range(SSM_LANE_BLOCKS):
            e = e_ref[d, q]
            a = jnp.broadcast_to(as_ref[d, q], (N_GROUPS, 2 * S))
            ere, eim, are, aim = e[:, :S], e[:, S:], a[:, :S], a[:, S:]
            ire = jnp.zeros((N_GROUPS, S), F32)
            iim = jnp.zeros((N_GROUPS, S), F32)
            for _ in range(N_GROUPS - 1):
                tre, tim = _cstep(ire, iim, are, aim, ere, eim)
                ire = jnp.where(row == first, 0.0, pltpu.roll(tre, shift, 0))
                iim = jnp.where(row == first, 0.0, pltpu.roll(tim, shift, 0))
            i_ref[d, q] = jnp.concatenate([ire, iim], axis=1)


def _carry_call(ends, a_seg):
    return pl.pallas_call(
        _carry_kernel,
        out_shape=jax.ShapeDtypeStruct(ends.shape, F32),
        name="ssm_carry",
    )(ends, a_seg)


def _ssm_post_kernel(yf_ref, yb_ref, u_ref, dsk_ref, wglu_ref, bglu_ref, sg_ref, permt_ref, o_ref, *, tt):
    y = yf_ref[...] + yb_ref[...] + dsk_ref[...] * u_ref[...]
    g = jax.nn.gelu(y)
    z = _dot(g.astype(BF16), wglu_ref[...]) + bglu_ref[...]
    s = g * jax.nn.sigmoid(z)
    sn = _rms(s, sg_ref[...]).astype(BF16)
    permt = permt_ref[...]
    sub = N_GROUPS * PERM_TT
    for k in range(tt // PERM_TT):
        blk = _dot(permt, sn[k * sub:(k + 1) * sub]).astype(BF16)
        o_ref[:, k * PERM_TT:(k + 1) * PERM_TT, :] = blk.reshape(N_GROUPS, PERM_TT, SSM_W)


def _ssm_post_call(yf, yb, u, w, permt):
    n_rows = u.shape[0]
    tt = PRE_TT
    rows = N_GROUPS * tt
    n_t = n_rows // rows
    lr = n_rows // N_GROUPS
    full = lambda a: pl.BlockSpec(a.shape, lambda j: (0,) * a.ndim)
    row_spec = pl.BlockSpec((rows, SSM_W), lambda j: (j, 0))
    weights = [w["dsk"], w["wglu"], w["bglu"], w["sg"], permt]
    blocks = ([((rows, SSM_W), F32)] * 3 + [(a.shape, a.dtype) for a in weights]
              + [((N_GROUPS, tt, SSM_W), BF16)])
    temps = [((rows, SSM_W), F32)] * 5
    return pl.pallas_call(
        functools.partial(_ssm_post_kernel, tt=tt),
        grid=(n_t,),
        in_specs=[row_spec, row_spec, row_spec] + [full(a) for a in weights],
        out_specs=pl.BlockSpec((N_GROUPS, tt, SSM_W), lambda j: (0, j, 0)),
        out_shape=jax.ShapeDtypeStruct((N_GROUPS, lr, SSM_W), BF16),
        compiler_params=pltpu.CompilerParams(
            dimension_semantics=("arbitrary",), vmem_limit_bytes=_vmem_limit(blocks, [], temps)),
        name="ssm_post",
    )(yf, yb, u, *weights)


def _post_kernel(x_ref, a_ref, sn_ref, ag_ref, wo_ref, n2_ref, w1_ref, w2_ref, fg_ref, o_ref):
    an = _rms(a_ref[0], ag_ref[...]).astype(BF16)
    mixed = jnp.concatenate([an, sn_ref[0]], axis=-1)
    x1 = x_ref[0] + _dot(mixed, wo_ref[...])
    h2 = _rms(x1, n2_ref[...]).astype(BF16)
    acc = jnp.zeros_like(x1)
    for c in range(D_FF // FF_CHUNK):
        hid = _dot(h2, w1_ref[:, c * FF_CHUNK:(c + 1) * FF_CHUNK])
        hid = jnp.square(jnp.maximum(hid, 0.0)).astype(BF16)
        acc = acc + _dot(hid, w2_ref[c * FF_CHUNK:(c + 1) * FF_CHUNK, :])
    o_ref[0] = _rms(x1 + acc, fg_ref[...])


def _post_call(x, a, sn, w):
    nb, lr, _ = x.shape
    tm = POST_TM
    const = lambda arr: pl.BlockSpec(arr.shape, lambda b, i: (0,) * arr.ndim, pipeline_mode=pl.Buffered(1))
    weights = [w["ag"], w["wo"], w["n2"], w["w1"], w["w2"], w["fg"]]
    blocks = [((tm, D_MODEL), F32), ((tm, N_HEADS * V_DIM), F32), ((tm, SSM_W), BF16), ((tm, D_MODEL), F32)]
    temps = [((tm, D_MODEL), F32)] * 3 + [((tm, FF_CHUNK), F32), ((tm, FF_CHUNK), BF16)]
    limit = _vmem_limit(blocks, [(a.shape, a.dtype) for a in weights], temps)
    return pl.pallas_call(
        _post_kernel,
        grid=(nb, lr // tm),
        in_specs=[
            pl.BlockSpec((1, tm, D_MODEL), lambda b, i: (b, i, 0)),
            pl.BlockSpec((1, tm, N_HEADS * V_DIM), lambda b, i: (b, i, 0)),
            pl.BlockSpec((1, tm, SSM_W), lambda b, i: (b, i, 0)),
        ] + [const(arr) for arr in weights],
        out_specs=pl.BlockSpec((1, tm, D_MODEL), lambda b, i: (b, i, 0)),
        out_shape=jax.ShapeDtypeStruct((nb, lr, D_MODEL), F32),
        compiler_params=pltpu.CompilerParams(
            dimension_semantics=("arbitrary", "arbitrary"), vmem_limit_bytes=limit),
        name="post",
    )(x, a, sn, *weights)


def _rope_tables(length):
    inv = ROPE_THETA ** (-jnp.arange(0, QK_ROPE, 2, dtype=F32) / QK_ROPE)
    zeros = lambda n: jnp.zeros((n,), F32)
    inv_lanes = jnp.concatenate([zeros(QK_NOPE), inv, inv, zeros(HEAD_PAD - QK_NOPE - QK_ROPE)])
    n_hi = -(-length // ROPE_BLOCK)
    a_hi = (jnp.arange(n_hi, dtype=F32) * ROPE_BLOCK)[:, None] * inv_lanes
    a_lo = jnp.arange(ROPE_BLOCK, dtype=F32)[:, None] * inv_lanes
    ch, sh = jnp.cos(a_hi)[:, None], jnp.sin(a_hi)[:, None]
    cl, sl = jnp.cos(a_lo)[None], jnp.sin(a_lo)[None]
    cos_t = (ch * cl - sh * sl).reshape(n_hi * ROPE_BLOCK, HEAD_PAD)[:length]
    sin_t = (sh * cl + ch * sl).reshape(n_hi * ROPE_BLOCK, HEAD_PAD)[:length]
    return cos_t, sin_t


def _rot_half_cols(w):
    half = QK_ROPE // 2
    return jnp.concatenate([-w[..., half:], w[..., :half]], axis=-1)


def _pack_weights(norm1_g, w_in, q_norm_g, w_uq, kv_norm_g, w_ukv, d_skip, w_glu, b_glu,
                  attn_out_g, ssm_out_g, w_out, norm2_g, w_mlp1, w_mlp2, final_g):
    row = lambda g: g.reshape(1, -1).astype(F32)
    o = Q_RANK + KV_RANK
    w_kr = w_in[:, o:o + QK_ROPE]
    zk = jnp.zeros((D_MODEL, QK_NOPE), F32)
    zp = jnp.zeros((D_MODEL, HEAD_PAD - QK_NOPE - QK_ROPE), F32)
    wa = jnp.concatenate([w_in[:, :o], zk, w_kr, zp, zk, _rot_half_cols(w_kr), zp], axis=1)
    wu = w_in[:, o + QK_ROPE:]
    wq = w_uq.reshape(Q_RANK, N_HEADS, QK_NOPE + QK_ROPE)
    zq = jnp.zeros((Q_RANK, N_HEADS, HEAD_PAD - QK_NOPE - QK_ROPE), F32)
    wqa = jnp.concatenate([wq, zq], axis=-1).reshape(Q_RANK, N_HEADS * HEAD_PAD)
    wqb = jnp.concatenate([jnp.zeros((Q_RANK, N_HEADS, QK_NOPE), F32), _rot_half_cols(wq[..., QK_NOPE:]), zq],
                          axis=-1).reshape(Q_RANK, N_HEADS * HEAD_PAD)
    wkv = w_ukv.reshape(KV_RANK, N_HEADS, QK_NOPE + V_DIM)
    zh = jnp.zeros((KV_RANK, N_HEADS, HEAD_PAD - QK_NOPE), F32)
    wk = jnp.concatenate([wkv[..., :QK_NOPE], zh], axis=-1).reshape(KV_RANK, N_HEADS * HEAD_PAD)
    zv = jnp.zeros((KV_RANK, N_HEADS, HEAD_PAD - V_DIM), F32)
    wv = jnp.concatenate([wkv[..., QK_NOPE:], zv], axis=-1).reshape(KV_RANK, N_HEADS * HEAD_PAD)
    bf = lambda a: a.astype(BF16)
    return dict(
        n1=row(norm1_g), wa=bf(wa), wu=bf(wu), qg=row(q_norm_g), wqa=bf(wqa), wqb=bf(wqb),
        kvg=row(kv_norm_g), wk=bf(wk), wv=bf(wv),
        dsk=row(d_skip), wglu=bf(w_glu), bglu=row(b_glu), sg=row(ssm_out_g),
        ag=row(attn_out_g), wo=bf(w_out), n2=row(norm2_g), w1=bf(w_mlp1), w2=bf(w_mlp2), fg=row(final_g))


def _pack_ssm(lam_re, lam_im, log_dt, b_re, b_im, c_re, c_im, seg_len):
    cmul = lambda xr, xi, yr, yi: (xr * yr - xi * yi, xr * yi + xi * yr)
    lam_re, lam_im = lam_re.astype(F32), lam_im.astype(F32)
    dt = jnp.exp(log_dt.astype(F32))[..., None]
    mag = jnp.exp(lam_re * dt)
    a_re, a_im = mag * jnp.cos(lam_im * dt), mag * jnp.sin(lam_im * dt)
    den = lam_re * lam_re + lam_im * lam_im
    k_re = ((a_re - 1.0) * lam_re + a_im * lam_im) / den
    k_im = (a_im * lam_re - (a_re - 1.0) * lam_im) / den
    bb_re, bb_im = cmul(k_re[..., None], k_im[..., None], b_re.astype(F32), b_im.astype(F32))
    p_re, p_im = a_re, a_im
    s_re, s_im = jnp.ones_like(a_re), jnp.zeros_like(a_re)
    n = seg_len
    while n:
        if n & 1:
            s_re, s_im = cmul(s_re, s_im, p_re, p_im)
        p_re, p_im = cmul(p_re, p_im, p_re, p_im)
        n >>= 1
    eye = jnp.eye(GROUPS_PER_BLOCK, dtype=F32)
    nq, gb = SSM_LANE_BLOCKS, GROUPS_PER_BLOCK

    def b_block(part):
        p = part.reshape(2, nq, gb, SSM_STATE, SSM_GROUP)
        return jnp.einsum("dqgnh,gk->dqghkn", p, eye).reshape(2, nq, LANES, STATES_PER_BLOCK)

    def c_block(part):
        p = part.reshape(2, nq, gb, SSM_GROUP, SSM_STATE)
        return jnp.einsum("dqghn,gk->dqgnkh", p, eye).reshape(2, nq, STATES_PER_BLOCK, LANES)

    def a_block(zr, zi):
        shape = (2, nq, 1, STATES_PER_BLOCK)
        return jnp.concatenate([zr.reshape(shape), zi.reshape(shape)], axis=-1)

    bmat = jnp.concatenate([b_block(bb_re), b_block(bb_im)], axis=-1).astype(BF16)
    cmat = jnp.concatenate([c_block(c_re.astype(F32)), -c_block(c_im.astype(F32))], axis=-2).astype(BF16)
    return bmat, cmat, a_block(a_re, a_im), a_block(s_re, s_im)


def _perm_matrix(tt):
    rows = N_GROUPS * tt
    dst = np.arange(rows)
    src = (dst % N_GROUPS) * tt + dst // N_GROUPS
    p = np.zeros((rows, rows), np.float32)
    p[dst, src] = 1.0
    return jnp.asarray(p, BF16)


def _trunk(x, rope, segmented, w, ssm):
    nb, lr, _ = x.shape
    bmat, cmat, a_blk, a_seg = ssm
    cos_t, sin_t = rope
    perm = _perm_matrix(PERM_TT)
    qt, k, vt, u = _pre_call(x, cos_t, sin_t, w, perm)
    seq_seg = (lambda a: a[None]) if segmented else (lambda a: a[:, None])
    att = _attn_call(seq_seg(qt), seq_seg(k), seq_seg(vt)).reshape(nb, lr, N_HEADS * V_DIM)
    init = jnp.zeros((2, SSM_LANE_BLOCKS, N_GROUPS, 2 * STATES_PER_BLOCK), F32)
    if segmented:
        (ends,) = _ssm_call(u, bmat, cmat, a_blk, init, with_y=False)
        init = _carry_call(ends, a_seg)
    yf, yb, _ = _ssm_call(u, bmat, cmat, a_blk, init, with_y=True)
    sn = _ssm_post_call(yf, yb, u, w, perm.T)
    return _post_call(x, att, sn, w)


def kernel(x_prompt, x_sample, norm1_g, w_in, q_norm_g, w_uq, kv_norm_g, w_ukv, lam_re, lam_im, log_dt,
           b_re, b_im, c_re, c_im, d_skip, w_glu, b_glu, attn_out_g, ssm_out_g, w_out, norm2_g, w_mlp1,
           w_mlp2, final_g):
    assert norm1_g.shape[0] == 1, "single-layer trunk"
    w = _pack_weights(norm1_g[0], w_in[0], q_norm_g[0], w_uq[0], kv_norm_g[0], w_ukv[0], d_skip[0], w_glu[0],
                      b_glu[0], attn_out_g[0], ssm_out_g[0], w_out[0], norm2_g[0], w_mlp1[0], w_mlp2[0], final_g)
    bp, lp, _ = x_prompt.shape
    bs, ls, _ = x_sample.shape
    assert bp == N_GROUPS and bs == 1 and ls % N_GROUPS == 0
    seg = ls // N_GROUPS
    ssm = _pack_ssm(lam_re[0], lam_im[0], log_dt[0], b_re[0], b_im[0], c_re[0], c_im[0], seg)
    cos_t, sin_t = _rope_tables(max(lp, ls))
    rope_p = (cos_t[None, :lp], sin_t[None, :lp])
    rope_s = (cos_t[:ls].reshape(N_GROUPS, seg, HEAD_PAD), sin_t[:ls].reshape(N_GROUPS, seg, HEAD_PAD))
    y_prompt = _trunk(x_prompt, rope_p, False, w, ssm)
    y_sample = _trunk(x_sample.reshape(N_GROUPS, seg, D_MODEL), rope_s, True, w, ssm)
    return y_prompt, y_sample.reshape(bs, ls, D_MODEL)
```

```python
import functools
import math

import jax
import jax.numpy as jnp
import numpy as np
from jax import lax
from jax.experimental import pallas as pl
from jax.experimental.pallas import tpu as pltpu

F32 = jnp.float32
BF16 = jnp.bfloat16

D_MODEL = 1024
N_HEADS = 8
QK_NOPE = 64
QK_ROPE = 32
V_DIM = 64
Q_RANK = 256
KV_RANK = 128
SSM_W = 512
SSM_GROUP = 16
SSM_GROUPS = 32
SSM_STATE = 64
D_FF = 4096
EPS = 1e-6
ROPE_THETA = 10000.0
ROPE_BLOCK = 128

LANES = 128
SUBLANES = 8
VMEM_LIMIT_CAP = 60000 * 1024

N_GROUPS = SUBLANES
HEAD_PAD = LANES
BF16_SUBLANES = 2 * SUBLANES
VT_ROWS = -(-(V_DIM + 1) // BF16_SUBLANES) * BF16_SUBLANES
N_PAIRS = N_HEADS // 2
SSM_LANE_BLOCKS = SSM_W // LANES
GROUPS_PER_BLOCK = LANES // SSM_GROUP
STATES_PER_BLOCK = GROUPS_PER_BLOCK * SSM_STATE

PRE_TT = 128
PERM_TT = 32
ATT_BQ = 512
ATT_BK = 256
ATT_UNROLL = 10
Q_SCALE = math.log2(math.e) / math.sqrt(QK_NOPE + QK_ROPE)
SSM_TT = 64
POST_TM = 512
FF_CHUNK = 1024


def _vmem_limit(nbytes):
    return int(min(VMEM_LIMIT_CAP, nbytes))


def _rms(x, g):
    return x * lax.rsqrt(jnp.mean(x * x, axis=-1, keepdims=True) + EPS) * g


def _dot(a, b):
    return jnp.dot(a, b, preferred_element_type=F32)


def _pre_kernel(x_ref, cos_ref, sin_ref, n1_ref, wa_ref, wu_ref, qg_ref, wqa_ref, wqb_ref,
                kvg_ref, wk_ref, wv_ref, perm_ref, qt_ref, k_ref, vt_ref, u_ref, *, tt):
    rows = N_GROUPS * tt
    x = x_ref[...].reshape(rows, D_MODEL)
    h = _rms(x, n1_ref[...]).astype(BF16)
    pa = _dot(h, wa_ref[...])
    u = _dot(h, wu_ref[...])
    cqn = _rms(pa[:, :Q_RANK], qg_ref[...]).astype(BF16)
    ckvn = _rms(pa[:, Q_RANK:Q_RANK + KV_RANK], kvg_ref[...]).astype(BF16)
    cos = jnp.broadcast_to(cos_ref[...], (N_GROUPS, tt, HEAD_PAD)).reshape(rows, HEAD_PAD)
    sin = jnp.broadcast_to(sin_ref[...], (N_GROUPS, tt, HEAD_PAD)).reshape(rows, HEAD_PAD)
    o = Q_RANK + KV_RANK
    kr = pa[:, o:o + HEAD_PAD]
    k_rope = kr * cos + pltpu.roll(kr, HEAD_PAD - QK_ROPE, 1) * sin
    qa = _dot(cqn, wqa_ref[...])
    qb = _dot(cqn, wqb_ref[...])
    k = _dot(ckvn, wk_ref[...])
    v = _dot(ckvn, wv_ref[...])
    ones_col = (lax.broadcasted_iota(jnp.int32, (1, HEAD_PAD), 1) == V_DIM).astype(F32)
    for hd in range(N_HEADS):
        sl = slice(hd * HEAD_PAD, (hd + 1) * HEAD_PAD)
        qht = ((qa[:, sl] * cos + qb[:, sl] * sin) * Q_SCALE).T
        vht = (v[:, sl] + ones_col).T
        for b in range(N_GROUPS):
            qt_ref[b, hd] = qht[:, b * tt:(b + 1) * tt].astype(BF16)
            vt_ref[b, hd, 0] = vht[:VT_ROWS, b * tt:(b + 1) * tt].astype(BF16)
        k_ref[:, hd] = (k[:, sl] + k_rope).astype(BF16).reshape(N_GROUPS, tt, HEAD_PAD)
    u_hi = u.astype(BF16)
    u_lo = (u - u_hi.astype(F32)).astype(BF16)
    perm = perm_ref[...]
    sub = N_GROUPS * PERM_TT
    for s in range(tt // PERM_TT):
        pick = lambda a: jnp.concatenate(
            [a[b * tt + s * PERM_TT:b * tt + (s + 1) * PERM_TT] for b in range(N_GROUPS)], axis=0)
        u_ref[s * sub:(s + 1) * sub, :] = _dot(perm, pick(u_hi)) + _dot(perm, pick(u_lo))


def _pre_call(x, cos_t, sin_t, w, perm):
    nb, lr, _ = x.shape
    tt = PRE_TT
    rows = nb * tt
    n_t = lr // tt
    tab_nb = cos_t.shape[0]
    kt_sub = ATT_BK // tt
    full = lambda a: pl.BlockSpec(a.shape, lambda j: (0,) * a.ndim)
    weights = [w["n1"], w["wa"], w["wu"], w["qg"], w["wqa"], w["wqb"], w["kvg"], w["wk"], w["wv"], perm]
    in_specs = [
        pl.BlockSpec((nb, tt, D_MODEL), lambda j: (0, j, 0)),
        pl.BlockSpec((tab_nb, tt, HEAD_PAD), lambda j: (0, j, 0)),
        pl.BlockSpec((tab_nb, tt, HEAD_PAD), lambda j: (0, j, 0)),
    ] + [full(a) for a in weights]
    out_shape = [
        jax.ShapeDtypeStruct((nb, N_HEADS, HEAD_PAD, lr), BF16),
        jax.ShapeDtypeStruct((nb, N_HEADS, lr, HEAD_PAD), BF16),
        jax.ShapeDtypeStruct((nb, N_HEADS, lr // ATT_BK, VT_ROWS, ATT_BK), BF16),
        jax.ShapeDtypeStruct((lr * nb, SSM_W), F32),
    ]
    out_specs = [
        pl.BlockSpec((nb, N_HEADS, HEAD_PAD, tt), lambda j: (0, 0, 0, j)),
        pl.BlockSpec((nb, N_HEADS, tt, HEAD_PAD), lambda j: (0, 0, j, 0)),
        pl.BlockSpec((nb, N_HEADS, 1, VT_ROWS, tt), lambda j: (0, 0, j // kt_sub, 0, j % kt_sub)),
        pl.BlockSpec((rows, SSM_W), lambda j: (j, 0)),
    ]
    return pl.pallas_call(
        functools.partial(_pre_kernel, tt=tt),
        grid=(n_t,),
        in_specs=in_specs,
        out_specs=out_specs,
        out_shape=out_shape,
        compiler_params=pltpu.CompilerParams(
            dimension_semantics=("arbitrary",), vmem_limit_bytes=_vmem_limit(56 << 20)),
        name="pre",
    )(x, cos_t, sin_t, *weights)


def _attn_kernel(qt_ref, k_ref, vt_ref, o_ref, s_ref, p_ref, alpha_ref, m_ref, acc_ref, *, n_seg, n_kb):
    bq, bk = ATT_BQ, ATT_BK
    m_ref[...] = jnp.full(m_ref.shape, -0.5 * float(np.finfo(np.float32).max), F32)
    acc_ref[...] = jnp.zeros(acc_ref.shape, F32)

    n = n_seg * n_kb

    def scores(c, slot):
        r0 = (c % n_kb) * bk
        if not isinstance(c, int):
            r0 = pl.multiple_of(r0, bk)
        for j in range(2):
            s_ref[slot, j, :, :bq] = _dot(k_ref[0, c // n_kb, j, pl.ds(r0, bk), :], qt_ref[0, 0, j])

    def softmax(slot):
        for j in range(2):
            for lt in range(bq // LANES):
                lanes = slice(lt * LANES, (lt + 1) * LANES)
                blk = s_ref[slot, j, :, lanes]
                m_old = m_ref[j, :, lanes]
                m_new = jnp.maximum(m_old, jnp.max(blk, axis=0, keepdims=True))
                alpha_ref[slot, j, :, lanes] = jnp.exp2(m_old - m_new)
                m_ref[j, :, lanes] = m_new
                p_ref[slot, j, :, lanes] = jnp.exp2(blk - m_new).astype(BF16)

    def weighted_values(c, slot):
        for j in range(2):
            pv = _dot(vt_ref[0, c // n_kb, j, c % n_kb], p_ref[slot, j, :, :bq])
            acc_ref[j] = acc_ref[j] * alpha_ref[slot, j] + pv

    steady = max(n - 2, 0)
    unroll = min(ATT_UNROLL, max(steady - steady % 2, 2))

    def stage(t, parity, do_scores=True, do_softmax=True, do_values=True):
        if do_values:
            weighted_values(t - 2, parity)
        if do_softmax:
            softmax(1 - parity)
        if do_scores:
            scores(t, parity)

    for t in range(2):
        stage(t, t % 2, t < n, 1 <= t <= n, False)
    n_iter = steady // unroll
    if n_iter:
        def body(k, _):
            for i in range(unroll):
                stage(2 + unroll * k + i, i % 2)
            return 0
        lax.fori_loop(0, n_iter, body, 0)
    for t in range(2 + unroll * n_iter, n + 2):
        stage(t, t % 2, t < n, t <= n, True)
    outs = [acc_ref[j][:V_DIM] / acc_ref[j][V_DIM:V_DIM + 1] for j in range(2)]
    o_ref[0] = jnp.concatenate(outs, axis=0).T


def _attn_call(qt, k, vt):
    n_seq, n_seg, _, lr, _ = k.shape
    n_kb = lr // ATT_BK
    n_qb = lr // ATT_BQ
    scratch = [
        pltpu.VMEM((2, 2, ATT_BK, ATT_BQ + LANES), F32),
        pltpu.VMEM((2, 2, ATT_BK, ATT_BQ + LANES), BF16),
        pltpu.VMEM((2, 2, 1, ATT_BQ), F32),
        pltpu.VMEM((2, 1, ATT_BQ), F32),
        pltpu.VMEM((2, VT_ROWS, ATT_BQ), F32),
    ]
    in_specs = [
        pl.BlockSpec((1, 1, 2, HEAD_PAD, ATT_BQ), lambda s, p, i: (s, i // n_qb, p, 0, i % n_qb)),
        pl.BlockSpec((1, n_seg, 2, lr, HEAD_PAD), lambda s, p, i: (s, 0, p, 0, 0)),
        pl.BlockSpec((1, n_seg, 2, n_kb, VT_ROWS, ATT_BK), lambda s, p, i: (s, 0, p, 0, 0, 0)),
    ]
    kv_bytes = 2 * n_seg * lr * (HEAD_PAD + VT_ROWS) * 2 * 2
    return pl.pallas_call(
        functools.partial(_attn_kernel, n_seg=n_seg, n_kb=n_kb),
        grid=(n_seq, N_PAIRS, n_seg * n_qb),
        in_specs=in_specs,
        out_specs=pl.BlockSpec((1, ATT_BQ, HEAD_PAD), lambda s, p, i: (s, i, p)),
        out_shape=jax.ShapeDtypeStruct((n_seq, n_seg * lr, N_HEADS * V_DIM), F32),
        scratch_shapes=scratch,
        compiler_params=pltpu.CompilerParams(
            dimension_semantics=("arbitrary", "arbitrary", "arbitrary"),
            vmem_limit_bytes=_vmem_limit(kv_bytes + (24 << 20))),
        name="attn",
    )(qt, k, vt)


def _cstep(xre, xim, are, aim, bre, bim):
    return are * xre - aim * xim + bre, are * xim + aim * xre + bim


def _ssm_kernel(uf_ref, ub_ref, bmat_ref, cmat_ref, a_ref, init_ref, *rest, tt, with_y):
    if with_y:
        yf_ref, yb_ref, fin_ref, state_ref, bu_ref, xs_ref = rest
    else:
        fin_ref, state_ref, bu_ref = rest
    j = pl.program_id(0)
    rows = N_GROUPS * tt
    n_pairs = tt // 2
    S = STATES_PER_BLOCK

    @pl.when(j == 0)
    def _():
        state_ref[...] = init_ref[...]

    for q in range(SSM_LANE_BLOCKS):
        lanes = slice(q * LANES, (q + 1) * LANES)
        slot = q % 2
        bu_ref[slot, 0] = _dot(uf_ref[:, lanes].astype(BF16), bmat_ref[0, q])
        bu_ref[slot, 1] = _dot(ub_ref[:, lanes].astype(BF16), bmat_ref[1, q])
        a = [jnp.broadcast_to(a_ref[d, q], (N_GROUPS, 2 * S)) for d in range(2)]
        are = [a[d][:, :S] for d in range(2)]
        aim = [a[d][:, S:] for d in range(2)]
        fwd = (state_ref[0, q, :, :S], state_ref[0, q, :, S:])
        bwd = (state_ref[1, q, :, :S], state_ref[1, q, :, S:])
        for k in range(n_pairs):
            r0 = k * 2 * N_GROUPS
            blk = bu_ref[slot, 0, r0:r0 + 2 * N_GROUPS, :]
            f1 = _cstep(*fwd, are[0], aim[0], blk[:N_GROUPS, :S], blk[:N_GROUPS, S:])
            fwd = _cstep(*f1, are[0], aim[0], blk[N_GROUPS:, :S], blk[N_GROUPS:, S:])
            r1 = rows - (k + 1) * 2 * N_GROUPS
            blk = bu_ref[slot, 1, r1:r1 + 2 * N_GROUPS, :]
            b1 = _cstep(*bwd, are[1], aim[1], blk[N_GROUPS:, :S], blk[N_GROUPS:, S:])
            bwd = _cstep(*b1, are[1], aim[1], blk[:N_GROUPS, :S], blk[:N_GROUPS, S:])
            if with_y:
                xs_ref[slot, 0, r0:r0 + 2 * N_GROUPS, :] = jnp.concatenate(
                    [jnp.concatenate(f1, axis=1), jnp.concatenate(fwd, axis=1)], axis=0).astype(BF16)
                xs_ref[slot, 1, r1:r1 + 2 * N_GROUPS, :] = jnp.concatenate(
                    [jnp.concatenate(bwd, axis=1), jnp.concatenate(b1, axis=1)], axis=0).astype(BF16)
        state_ref[0, q] = jnp.concatenate(fwd, axis=1)
        state_ref[1, q] = jnp.concatenate(bwd, axis=1)
        if with_y:
            yf_ref[:, lanes] = _dot(xs_ref[slot, 0], cmat_ref[0, q])
            yb_ref[:, lanes] = _dot(xs_ref[slot, 1], cmat_ref[1, q])

    @pl.when(j == pl.num_programs(0) - 1)
    def _():
        fin_ref[...] = state_ref[...]


def _ssm_call(u, bmat, cmat, a, init, with_y):
    n_rows = u.shape[0]
    tt = SSM_TT
    rows = N_GROUPS * tt
    n_t = n_rows // rows
    full = lambda arr: pl.BlockSpec(arr.shape, lambda j: (0,) * arr.ndim)
    in_specs = [
        pl.BlockSpec((rows, SSM_W), lambda j: (j, 0)),
        pl.BlockSpec((rows, SSM_W), lambda j: (n_t - 1 - j, 0)),
        full(bmat), full(cmat), full(a), full(init),
    ]
    state_shape = (2, SSM_LANE_BLOCKS, N_GROUPS, 2 * STATES_PER_BLOCK)
    out_shape = [jax.ShapeDtypeStruct(state_shape, F32)]
    out_specs = [pl.BlockSpec(state_shape, lambda j: (0, 0, 0, 0))]
    scratch = [pltpu.VMEM(state_shape, F32), pltpu.VMEM((2, 2, rows, 2 * STATES_PER_BLOCK), F32)]
    if with_y:
        out_shape = [jax.ShapeDtypeStruct((n_rows, SSM_W), F32)] * 2 + out_shape
        out_specs = [pl.BlockSpec((rows, SSM_W), lambda j: (j, 0)),
                     pl.BlockSpec((rows, SSM_W), lambda j: (n_t - 1 - j, 0))] + out_specs
        scratch.append(pltpu.VMEM((2, 2, rows, 2 * STATES_PER_BLOCK), BF16))
    return pl.pallas_call(
        functools.partial(_ssm_kernel, tt=tt, with_y=with_y),
        grid=(n_t,),
        in_specs=in_specs,
        out_specs=out_specs,
        out_shape=out_shape,
        scratch_shapes=scratch,
        compiler_params=pltpu.CompilerParams(
            dimension_semantics=("arbitrary",), vmem_limit_bytes=_vmem_limit(48 << 20)),
        name="ssm" if with_y else "ssm_ends",
    )(u, u, bmat, cmat, a, init)


def _carry_kernel(e_ref, as_ref, i_ref):
    S = STATES_PER_BLOCK
    row = lax.broadcasted_iota(jnp.int32, (N_GROUPS, S), 0)
    for d in range(2):
        first = 0 if d == 0 else N_GROUPS - 1
        shift = 1 if d == 0 else N_GROUPS - 1
        for q in range(SSM_LANE_BLOCKS):
            e = e_ref[d, q]
            a = jnp.broadcast_to(as_ref[d, q], (N_GROUPS, 2 * S))
            ere, eim, are, aim = e[:, :S], e[:, S:], a[:, :S], a[:, S:]
            ire = jnp.zeros((N_GROUPS, S), F32)
            iim = jnp.zeros((N_GROUPS, S), F32)
            for _ in range(N_GROUPS - 1):
                tre, tim = _cstep(ire, iim, are, aim, ere, eim)
                ire = jnp.where(row == first, 0.0, pltpu.roll(tre, shift, 0))
                iim = jnp.where(row == first, 0.0, pltpu.roll(tim, shift, 0))
            i_ref[d, q] = jnp.concatenate([ire, iim], axis=1)


def _carry_call(ends, a_seg):
    return pl.pallas_call(
        _carry_kernel,
        out_shape=jax.ShapeDtypeStruct(ends.shape, F32),
        name="ssm_carry",
    )(ends, a_seg)


def _ssm_post_kernel(yf_ref, yb_ref, u_ref, dsk_ref, wglu_ref, bglu_ref, sg_ref, permt_ref, o_ref, *, tt):
    y = yf_ref[...] + yb_ref[...] + dsk_ref[...] * u_ref[...]
    g = jax.nn.gelu(y)
    z = _dot(g.astype(BF16), wglu_ref[...]) + bglu_ref[...]
    s = g * jax.nn.sigmoid(z)
    sn = _rms(s, sg_ref[...]).astype(BF16)
    permt = permt_ref[...]
    sub = N_GROUPS * PERM_TT
    for k in range(tt // PERM_TT):
        blk = _dot(permt, sn[k * sub:(k + 1) * sub]).astype(BF16)
        o_ref[:, k * PERM_TT:(k + 1) * PERM_TT, :] = blk.reshape(N_GROUPS, PERM_TT, SSM_W)


def _ssm_post_call(yf, yb, u, w, permt):
    n_rows = u.shape[0]
    tt = PRE_TT
    rows = N_GROUPS * tt
    n_t = n_rows // rows
    lr = n_rows // N_GROUPS
    full = lambda a: pl.BlockSpec(a.shape, lambda j: (0,) * a.ndim)
    row_spec = pl.BlockSpec((rows, SSM_W), lambda j: (j, 0))
    weights = [w["dsk"], w["wglu"], w["bglu"], w["sg"], permt]
    return pl.pallas_call(
        functools.partial(_ssm_post_kernel, tt=tt),
        grid=(n_t,),
        in_specs=[row_spec, row_spec, row_spec] + [full(a) for a in weights],
        out_specs=pl.BlockSpec((N_GROUPS, tt, SSM_W), lambda j: (0, j, 0)),
        out_shape=jax.ShapeDtypeStruct((N_GROUPS, lr, SSM_W), BF16),
        compiler_params=pltpu.CompilerParams(
            dimension_semantics=("arbitrary",), vmem_limit_bytes=_vmem_limit(40 << 20)),
        name="ssm_post",
    )(yf, yb, u, *weights)


def _post_kernel(x_ref, a_ref, sn_ref, ag_ref, wo_ref, n2_ref, w1_ref, w2_ref, fg_ref, o_ref):
    an = _rms(a_ref[0], ag_ref[...]).astype(BF16)
    mixed = jnp.concatenate([an, sn_ref[0]], axis=-1)
    x1 = x_ref[0] + _dot(mixed, wo_ref[...])
    h2 = _rms(x1, n2_ref[...]).astype(BF16)
    acc = jnp.zeros_like(x1)
    for c in range(D_FF // FF_CHUNK):
        hid = _dot(h2, w1_ref[:, c * FF_CHUNK:(c + 1) * FF_CHUNK])
        hid = jnp.square(jnp.maximum(hid, 0.0)).astype(BF16)
        acc = acc + _dot(hid, w2_ref[c * FF_CHUNK:(c + 1) * FF_CHUNK, :])
    o_ref[0] = _rms(x1 + acc, fg_ref[...])


def _post_call(x, a, sn, w):
    nb, lr, _ = x.shape
    tm = POST_TM
    const = lambda arr: pl.BlockSpec(arr.shape, lambda b, i: (0,) * arr.ndim, pipeline_mode=pl.Buffered(1))
    weights = [w["ag"], w["wo"], w["n2"], w["w1"], w["w2"], w["fg"]]
    return pl.pallas_call(
        _post_kernel,
        grid=(nb, lr // tm),
        in_specs=[
            pl.BlockSpec((1, tm, D_MODEL), lambda b, i: (b, i, 0)),
            pl.BlockSpec((1, tm, N_HEADS * V_DIM), lambda b, i: (b, i, 0)),
            pl.BlockSpec((1, tm, SSM_W), lambda b, i: (b, i, 0)),
        ] + [const(arr) for arr in weights],
        out_specs=pl.BlockSpec((1, tm, D_MODEL), lambda b, i: (b, i, 0)),
        out_shape=jax.ShapeDtypeStruct((nb, lr, D_MODEL), F32),
        compiler_params=pltpu.CompilerParams(
            dimension_semantics=("arbitrary", "arbitrary"), vmem_limit_bytes=_vmem_limit(48 << 20)),
        name="post",
    )(x, a, sn, *weights)


def _rope_tables(length):
    inv = ROPE_THETA ** (-jnp.arange(0, QK_ROPE, 2, dtype=F32) / QK_ROPE)
    zeros = lambda n: jnp.zeros((n,), F32)
    inv_lanes = jnp.concatenate([zeros(QK_NOPE), inv, inv, zeros(HEAD_PAD - QK_NOPE - QK_ROPE)])
    n_hi = -(-length // ROPE_BLOCK)
    a_hi = (jnp.arange(n_hi, dtype=F32) * ROPE_BLOCK)[:, None] * inv_lanes
    a_lo = jnp.arange(ROPE_BLOCK, dtype=F32)[:, None] * inv_lanes
    ch, sh = jnp.cos(a_hi)[:, None], jnp.sin(a_hi)[:, None]
    cl, sl = jnp.cos(a_lo)[None], jnp.sin(a_lo)[None]
    used = (jnp.arange(HEAD_PAD) < QK_NOPE + QK_ROPE).astype(F32)
    cos_t = ((ch * cl - sh * sl) * used).reshape(n_hi * ROPE_BLOCK, HEAD_PAD)[:length]
    sin_t = (sh * cl + ch * sl).reshape(n_hi * ROPE_BLOCK, HEAD_PAD)[:length]
    return cos_t, sin_t


def _rot_half_cols(w):
    half = QK_ROPE // 2
    return jnp.concatenate([-w[..., half:], w[..., :half]], axis=-1)


def _pack_weights(norm1_g, w_in, q_norm_g, w_uq, kv_norm_g, w_ukv, d_skip, w_glu, b_glu,
                  attn_out_g, ssm_out_g, w_out, norm2_g, w_mlp1, w_mlp2, final_g):
    row = lambda g: g.reshape(1, -1).astype(F32)
    o = Q_RANK + KV_RANK
    w_kr = w_in[:, o:o + QK_ROPE]
    zk = jnp.zeros((D_MODEL, QK_NOPE), F32)
    wa = jnp.concatenate([w_in[:, :o], zk, w_kr, _rot_half_cols(w_kr)], axis=1)
    wu = w_in[:, o + QK_ROPE:]
    wq = w_uq.reshape(Q_RANK, N_HEADS, QK_NOPE + QK_ROPE)
    zq = jnp.zeros((Q_RANK, N_HEADS, HEAD_PAD - QK_NOPE - QK_ROPE), F32)
    wqa = jnp.concatenate([wq, zq], axis=-1).reshape(Q_RANK, N_HEADS * HEAD_PAD)
    wqb = jnp.concatenate([jnp.zeros((Q_RANK, N_HEADS, QK_NOPE), F32), _rot_half_cols(wq[..., QK_NOPE:]), zq],
                          axis=-1).reshape(Q_RANK, N_HEADS * HEAD_PAD)
    wkv = w_ukv.reshape(KV_RANK, N_HEADS, QK_NOPE + V_DIM)
    zh = jnp.zeros((KV_RANK, N_HEADS, HEAD_PAD - QK_NOPE), F32)
    wk = jnp.concatenate([wkv[..., :QK_NOPE], zh], axis=-1).reshape(KV_RANK, N_HEADS * HEAD_PAD)
    zv = jnp.zeros((KV_RANK, N_HEADS, HEAD_PAD - V_DIM), F32)
    wv = jnp.concatenate([wkv[..., QK_NOPE:], zv], axis=-1).reshape(KV_RANK, N_HEADS * HEAD_PAD)
    bf = lambda a: a.astype(BF16)
    return dict(
        n1=row(norm1_g), wa=bf(wa), wu=bf(wu), qg=row(q_norm_g), wqa=bf(wqa), wqb=bf(wqb),
        kvg=row(kv_norm_g), wk=bf(wk), wv=bf(wv),
        dsk=row(d_skip), wglu=bf(w_glu), bglu=row(b_glu), sg=row(ssm_out_g),
        ag=row(attn_out_g), wo=bf(w_out), n2=row(norm2_g), w1=bf(w_mlp1), w2=bf(w_mlp2), fg=row(final_g))


def _pack_ssm(lam_re, lam_im, log_dt, b_re, b_im, c_re, c_im, seg_len):
    cmul = lambda xr, xi, yr, yi: (xr * yr - xi * yi, xr * yi + xi * yr)
    lam_re, lam_im = lam_re.astype(F32), lam_im.astype(F32)
    dt = jnp.exp(log_dt.astype(F32))[..., None]
    mag = jnp.exp(lam_re * dt)
    a_re, a_im = mag * jnp.cos(lam_im * dt), mag * jnp.sin(lam_im * dt)
    den = lam_re * lam_re + lam_im * lam_im
    k_re = ((a_re - 1.0) * lam_re + a_im * lam_im) / den
    k_im = (a_im * lam_re - (a_re - 1.0) * lam_im) / den
    bb_re, bb_im = cmul(k_re[..., None], k_im[..., None], b_re.astype(F32), b_im.astype(F32))
    p_re, p_im = a_re, a_im
    s_re, s_im = jnp.ones_like(a_re), jnp.zeros_like(a_re)
    n = seg_len
    while n:
        if n & 1:
            s_re, s_im = cmul(s_re, s_im, p_re, p_im)
        p_re, p_im = cmul(p_re, p_im, p_re, p_im)
        n >>= 1
    eye = jnp.eye(GROUPS_PER_BLOCK, dtype=F32)
    nq, gb = SSM_LANE_BLOCKS, GROUPS_PER_BLOCK

    def b_block(part):
        p = part.reshape(2, nq, gb, SSM_STATE, SSM_GROUP)
        return jnp.einsum("dqgnh,gk->dqghkn", p, eye).reshape(2, nq, LANES, STATES_PER_BLOCK)

    def c_block(part):
        p = part.reshape(2, nq, gb, SSM_GROUP, SSM_STATE)
        return jnp.einsum("dqghn,gk->dqgnkh", p, eye).reshape(2, nq, STATES_PER_BLOCK, LANES)

    def a_block(zr, zi):
        shape = (2, nq, 1, STATES_PER_BLOCK)
        return jnp.concatenate([zr.reshape(shape), zi.reshape(shape)], axis=-1)

    bmat = jnp.concatenate([b_block(bb_re), b_block(bb_im)], axis=-1).astype(BF16)
    cmat = jnp.concatenate([c_block(c_re.astype(F32)), -c_block(c_im.astype(F32))], axis=-2).astype(BF16)
    return bmat, cmat, a_block(a_re, a_im), a_block(s_re, s_im)


def _perm_matrix(tt):
    rows = N_GROUPS * tt
    dst = np.arange(rows)
    src = (dst % N_GROUPS) * tt + dst // N_GROUPS
    p = np.zeros((rows, rows), np.float32)
    p[dst, src] = 1.0
    return jnp.asarray(p, BF16)


def _trunk(x, rope, segmented, w, ssm):
    nb, lr, _ = x.shape
    bmat, cmat, a_blk, a_seg = ssm
    cos_t, sin_t = rope
    perm = _perm_matrix(PERM_TT)
    qt, k, vt, u = _pre_call(x, cos_t, sin_t, w, perm)
    seq_seg = (lambda a: a[None]) if segmented else (lambda a: a[:, None])
    att = _attn_call(seq_seg(qt), seq_seg(k), seq_seg(vt)).reshape(nb, lr, N_HEADS * V_DIM)
    init = jnp.zeros((2, SSM_LANE_BLOCKS, N_GROUPS, 2 * STATES_PER_BLOCK), F32)
    if segmented:
        (ends,) = _ssm_call(u, bmat, cmat, a_blk, init, with_y=False)
        init = _carry_call(ends, a_seg)
    yf, yb, _ = _ssm_call(u, bmat, cmat, a_blk, init, with_y=True)
    sn = _ssm_post_call(yf, yb, u, w, perm.T)
    return _post_call(x, att, sn, w)


def kernel(x_prompt, x_sample, norm1_g, w_in, q_norm_g, w_uq, kv_norm_g, w_ukv, lam_re, lam_im, log_dt,
           b_re, b_im, c_re, c_im, d_skip, w_glu, b_glu, attn_out_g, ssm_out_g, w_out, norm2_g, w_mlp1,
           w_mlp2, final_g):
    assert norm1_g.shape[0] == 1, "single-layer trunk"
    w = _pack_weights(norm1_g[0], w_in[0], q_norm_g[0], w_uq[0], kv_norm_g[0], w_ukv[0], d_skip[0], w_glu[0],
                      b_glu[0], attn_out_g[0], ssm_out_g[0], w_out[0], norm2_g[0], w_mlp1[0], w_mlp2[0], final_g)
    bp, lp, _ = x_prompt.shape
    bs, ls, _ = x_sample.shape
    assert bp == N_GROUPS and bs == 1 and ls % N_GROUPS == 0
    seg = ls // N_GROUPS
    ssm = _pack_ssm(lam_re[0], lam_im[0], log_dt[0], b_re[0], b_im[0], c_re[0], c_im[0], seg)
    cos_t, sin_t = _rope_tables(max(lp, ls))
    rope_p = (cos_t[None, :lp], sin_t[None, :lp])
    rope_s = (cos_t[:ls].reshape(N_GROUPS, seg, HEAD_PAD), sin_t[:ls].reshape(N_GROUPS, seg, HEAD_PAD))
    y_prompt = _trunk(x_prompt, rope_p, False, w, ssm)
    y_sample = _trunk(x_sample.reshape(N_GROUPS, seg, D_MODEL), rope_s, True, w, ssm)
    return y_prompt, y_sample.reshape(bs, ls, D_MODEL)
```

```python
import functools
import math

import jax
import jax.numpy as jnp
import numpy as np
from jax import lax
from jax.experimental import pallas as pl
from jax.experimental.pallas import tpu as pltpu

F32 = jnp.float32
BF16 = jnp.bfloat16

D_MODEL = 1024
N_HEADS = 8
QK_NOPE = 64
QK_ROPE = 32
V_DIM = 64
Q_RANK = 256
KV_RANK = 128
SSM_W = 512
SSM_GROUP = 16
SSM_GROUPS = 32
SSM_STATE = 64
D_FF = 4096
EPS = 1e-6
ROPE_THETA = 10000.0
ROPE_BLOCK = 128

LANES = 128
SUBLANES = 8
VMEM_LIMIT_CAP = 60000 * 1024

N_GROUPS = SUBLANES
HEAD_PAD = LANES
BF16_SUBLANES = 2 * SUBLANES
VT_ROWS = -(-(V_DIM + 1) // BF16_SUBLANES) * BF16_SUBLANES
N_PAIRS = N_HEADS // 2
SSM_LANE_BLOCKS = SSM_W // LANES
GROUPS_PER_BLOCK = LANES // SSM_GROUP
STATES_PER_BLOCK = GROUPS_PER_BLOCK * SSM_STATE

PRE_TT = 128
PERM_TT = 32
ATT_BQ = 512
ATT_BK = 256
ATT_UNROLL = 10
Q_SCALE = math.log2(math.e) / math.sqrt(QK_NOPE + QK_ROPE)
SSM_TT = 64
POST_TM = 512
FF_CHUNK = 1024


def _vmem_limit(nbytes):
    return int(min(VMEM_LIMIT_CAP, nbytes))


def _rms(x, g):
    return x * lax.rsqrt(jnp.mean(x * x, axis=-1, keepdims=True) + EPS) * g


def _dot(a, b):
    return jnp.dot(a, b, preferred_element_type=F32)


def _pre_kernel(x_ref, cos_ref, sin_ref, n1_ref, wa_ref, wu_ref, qg_ref, wqa_ref, wqb_ref,
                kvg_ref, wkv_ref, perm_ref, qt_ref, k_ref, vt_ref, u_ref, *, tt):
    rows = N_GROUPS * tt
    x = x_ref[...].reshape(rows, D_MODEL)
    h = _rms(x, n1_ref[...]).astype(BF16)
    pa = _dot(h, wa_ref[...])
    u = _dot(h, wu_ref[...])
    cqn = _rms(pa[:, :Q_RANK], qg_ref[...]).astype(BF16)
    ckvn = _rms(pa[:, Q_RANK:Q_RANK + KV_RANK], kvg_ref[...]).astype(BF16)
    cos = jnp.broadcast_to(cos_ref[...], (N_GROUPS, tt, HEAD_PAD)).reshape(rows, HEAD_PAD)
    sin = jnp.broadcast_to(sin_ref[...], (N_GROUPS, tt, HEAD_PAD)).reshape(rows, HEAD_PAD)
    o = Q_RANK + KV_RANK
    kr = pa[:, o:o + HEAD_PAD]
    k_rope = kr * cos + pltpu.roll(kr, HEAD_PAD - QK_ROPE, 1) * sin
    qa = _dot(cqn, wqa_ref[...])
    qb = _dot(cqn, wqb_ref[...])
    kv = _dot(ckvn, wkv_ref[...])
    nope_lanes = lax.broadcasted_iota(jnp.int32, (1, HEAD_PAD), 1) < QK_NOPE
    vt_tail = (lax.broadcasted_iota(jnp.int32, (VT_ROWS - V_DIM, tt), 0) == 0).astype(F32)
    for hd in range(N_HEADS):
        sl = slice(hd * HEAD_PAD, (hd + 1) * HEAD_PAD)
        qht = ((qa[:, sl] * cos + qb[:, sl] * sin) * Q_SCALE).T
        kvt = kv[:, sl].T
        for b in range(N_GROUPS):
            cols = slice(b * tt, (b + 1) * tt)
            qt_ref[b, hd] = qht[:, cols].astype(BF16)
            vt_ref[b, hd, 0] = jnp.concatenate([kvt[QK_NOPE:, cols], vt_tail], axis=0).astype(BF16)
        kh = jnp.where(nope_lanes, kv[:, sl], 0.0) + k_rope
        k_ref[:, hd] = kh.astype(BF16).reshape(N_GROUPS, tt, HEAD_PAD)
    u_hi = u.astype(BF16)
    u_lo = (u - u_hi.astype(F32)).astype(BF16)
    perm = perm_ref[...]
    sub = N_GROUPS * PERM_TT
    for s in range(tt // PERM_TT):
        pick = lambda a: jnp.concatenate(
            [a[b * tt + s * PERM_TT:b * tt + (s + 1) * PERM_TT] for b in range(N_GROUPS)], axis=0)
        u_ref[s * sub:(s + 1) * sub, :] = _dot(perm, pick(u_hi)) + _dot(perm, pick(u_lo))


def _pre_call(x, cos_t, sin_t, w, perm):
    nb, lr, _ = x.shape
    tt = PRE_TT
    rows = nb * tt
    n_t = lr // tt
    tab_nb = cos_t.shape[0]
    kt_sub = ATT_BK // tt
    full = lambda a: pl.BlockSpec(a.shape, lambda j: (0,) * a.ndim)
    weights = [w["n1"], w["wa"], w["wu"], w["qg"], w["wqa"], w["wqb"], w["kvg"], w["wkv"], perm]
    in_specs = [
        pl.BlockSpec((nb, tt, D_MODEL), lambda j: (0, j, 0)),
        pl.BlockSpec((tab_nb, tt, HEAD_PAD), lambda j: (0, j, 0)),
        pl.BlockSpec((tab_nb, tt, HEAD_PAD), lambda j: (0, j, 0)),
    ] + [full(a) for a in weights]
    out_shape = [
        jax.ShapeDtypeStruct((nb, N_HEADS, HEAD_PAD, lr), BF16),
        jax.ShapeDtypeStruct((nb, N_HEADS, lr, HEAD_PAD), BF16),
        jax.ShapeDtypeStruct((nb, N_HEADS, lr // ATT_BK, VT_ROWS, ATT_BK), BF16),
        jax.ShapeDtypeStruct((lr * nb, SSM_W), F32),
    ]
    out_specs = [
        pl.BlockSpec((nb, N_HEADS, HEAD_PAD, tt), lambda j: (0, 0, 0, j)),
        pl.BlockSpec((nb, N_HEADS, tt, HEAD_PAD), lambda j: (0, 0, j, 0)),
        pl.BlockSpec((nb, N_HEADS, 1, VT_ROWS, tt), lambda j: (0, 0, j // kt_sub, 0, j % kt_sub)),
        pl.BlockSpec((rows, SSM_W), lambda j: (j, 0)),
    ]
    return pl.pallas_call(
        functools.partial(_pre_kernel, tt=tt),
        grid=(n_t,),
        in_specs=in_specs,
        out_specs=out_specs,
        out_shape=out_shape,
        compiler_params=pltpu.CompilerParams(
            dimension_semantics=("arbitrary",), vmem_limit_bytes=_vmem_limit(56 << 20)),
        name="pre",
    )(x, cos_t, sin_t, *weights)


def _attn_kernel(qt_ref, k_ref, vt_ref, o_ref, s_ref, p_ref, alpha_ref, m_ref, acc_ref, *, n_seg, n_kb):
    bq, bk = ATT_BQ, ATT_BK
    m_ref[...] = jnp.full(m_ref.shape, -0.5 * float(np.finfo(np.float32).max), F32)
    acc_ref[...] = jnp.zeros(acc_ref.shape, F32)

    n = n_seg * n_kb

    def scores(c, slot):
        r0 = (c % n_kb) * bk
        if not isinstance(c, int):
            r0 = pl.multiple_of(r0, bk)
        for j in range(2):
            s_ref[slot, j, :, :bq] = _dot(k_ref[0, c // n_kb, j, pl.ds(r0, bk), :], qt_ref[0, 0, j])

    def softmax(slot):
        for j in range(2):
            for lt in range(bq // LANES):
                lanes = slice(lt * LANES, (lt + 1) * LANES)
                blk = s_ref[slot, j, :, lanes]
                m_old = m_ref[j, :, lanes]
                m_new = jnp.maximum(m_old, jnp.max(blk, axis=0, keepdims=True))
                alpha_ref[slot, j, :, lanes] = jnp.exp2(m_old - m_new)
                m_ref[j, :, lanes] = m_new
                p_ref[slot, j, :, lanes] = jnp.exp2(blk - m_new).astype(BF16)

    def weighted_values(c, slot):
        for j in range(2):
            pv = _dot(vt_ref[0, c // n_kb, j, c % n_kb], p_ref[slot, j, :, :bq])
            acc_ref[j] = acc_ref[j] * alpha_ref[slot, j] + pv

    steady = max(n - 2, 0)
    unroll = min(ATT_UNROLL, max(steady - steady % 2, 2))

    def stage(t, parity, do_scores=True, do_softmax=True, do_values=True):
        if do_values:
            weighted_values(t - 2, parity)
        if do_softmax:
            softmax(1 - parity)
        if do_scores:
            scores(t, parity)

    for t in range(2):
        stage(t, t % 2, t < n, 1 <= t <= n, False)
    n_iter = steady // unroll
    if n_iter:
        def body(k, _):
            for i in range(unroll):
                stage(2 + unroll * k + i, i % 2)
            return 0
        lax.fori_loop(0, n_iter, body, 0)
    for t in range(2 + unroll * n_iter, n + 2):
        stage(t, t % 2, t < n, t <= n, True)
    outs = [acc_ref[j][:V_DIM] / acc_ref[j][V_DIM:V_DIM + 1] for j in range(2)]
    o_ref[0] = jnp.concatenate(outs, axis=0).T


def _attn_call(qt, k, vt):
    n_seq, n_seg, _, lr, _ = k.shape
    n_kb = lr // ATT_BK
    n_qb = lr // ATT_BQ
    scratch = [
        pltpu.VMEM((2, 2, ATT_BK, ATT_BQ + LANES), F32),
        pltpu.VMEM((2, 2, ATT_BK, ATT_BQ + LANES), BF16),
        pltpu.VMEM((2, 2, 1, ATT_BQ), F32),
        pltpu.VMEM((2, 1, ATT_BQ), F32),
        pltpu.VMEM((2, VT_ROWS, ATT_BQ), F32),
    ]
    in_specs = [
        pl.BlockSpec((1, 1, 2, HEAD_PAD, ATT_BQ), lambda s, p, i: (s, i // n_qb, p, 0, i % n_qb)),
        pl.BlockSpec((1, n_seg, 2, lr, HEAD_PAD), lambda s, p, i: (s, 0, p, 0, 0)),
        pl.BlockSpec((1, n_seg, 2, n_kb, VT_ROWS, ATT_BK), lambda s, p, i: (s, 0, p, 0, 0, 0)),
    ]
    kv_bytes = 2 * n_seg * lr * (HEAD_PAD + VT_ROWS) * 2 * 2
    return pl.pallas_call(
        functools.partial(_attn_kernel, n_seg=n_seg, n_kb=n_kb),
        grid=(n_seq, N_PAIRS, n_seg * n_qb),
        in_specs=in_specs,
        out_specs=pl.BlockSpec((1, ATT_BQ, HEAD_PAD), lambda s, p, i: (s, i, p)),
        out_shape=jax.ShapeDtypeStruct((n_seq, n_seg * lr, N_HEADS * V_DIM), F32),
        scratch_shapes=scratch,
        compiler_params=pltpu.CompilerParams(
            dimension_semantics=("arbitrary", "arbitrary", "arbitrary"),
            vmem_limit_bytes=_vmem_limit(kv_bytes + (24 << 20))),
        name="attn",
    )(qt, k, vt)


def _cstep(xre, xim, are, aim, bre, bim):
    return are * xre - aim * xim + bre, are * xim + aim * xre + bim


def _ssm_kernel(uf_ref, ub_ref, bmat_ref, cmat_ref, a_ref, init_ref, *rest, tt, with_y):
    if with_y:
        yf_ref, yb_ref, fin_ref, state_ref, bu_ref, xs_ref = rest
    else:
        fin_ref, state_ref, bu_ref = rest
    j = pl.program_id(0)
    rows = N_GROUPS * tt
    n_pairs = tt // 2
    S = STATES_PER_BLOCK

    @pl.when(j == 0)
    def _():
        state_ref[...] = init_ref[...]

    for q in range(SSM_LANE_BLOCKS):
        lanes = slice(q * LANES, (q + 1) * LANES)
        slot = q % 2
        bu_ref[slot, 0] = _dot(uf_ref[:, lanes].astype(BF16), bmat_ref[0, q])
        bu_ref[slot, 1] = _dot(ub_ref[:, lanes].astype(BF16), bmat_ref[1, q])
        a = [jnp.broadcast_to(a_ref[d, q], (N_GROUPS, 2 * S)) for d in range(2)]
        are = [a[d][:, :S] for d in range(2)]
        aim = [a[d][:, S:] for d in range(2)]
        fwd = (state_ref[0, q, :, :S], state_ref[0, q, :, S:])
        bwd = (state_ref[1, q, :, :S], state_ref[1, q, :, S:])
        for k in range(n_pairs):
            r0 = k * 2 * N_GROUPS
            blk = bu_ref[slot, 0, r0:r0 + 2 * N_GROUPS, :]
            f1 = _cstep(*fwd, are[0], aim[0], blk[:N_GROUPS, :S], blk[:N_GROUPS, S:])
            fwd = _cstep(*f1, are[0], aim[0], blk[N_GROUPS:, :S], blk[N_GROUPS:, S:])
            r1 = rows - (k + 1) * 2 * N_GROUPS
            blk = bu_ref[slot, 1, r1:r1 + 2 * N_GROUPS, :]
            b1 = _cstep(*bwd, are[1], aim[1], blk[N_GROUPS:, :S], blk[N_GROUPS:, S:])
            bwd = _cstep(*b1, are[1], aim[1], blk[:N_GROUPS, :S], blk[:N_GROUPS, S:])
            if with_y:
                xs_ref[slot, 0, r0:r0 + 2 * N_GROUPS, :] = jnp.concatenate(
                    [jnp.concatenate(f1, axis=1), jnp.concatenate(fwd, axis=1)], axis=0).astype(BF16)
                xs_ref[slot, 1, r1:r1 + 2 * N_GROUPS, :] = jnp.concatenate(
                    [jnp.concatenate(bwd, axis=1), jnp.concatenate(b1, axis=1)], axis=0).astype(BF16)
        state_ref[0, q] = jnp.concatenate(fwd, axis=1)
        state_ref[1, q] = jnp.concatenate(bwd, axis=1)
        if with_y:
            yf_ref[:, lanes] = _dot(xs_ref[slot, 0], cmat_ref[0, q])
            yb_ref[:, lanes] = _dot(xs_ref[slot, 1], cmat_ref[1, q])

    @pl.when(j == pl.num_programs(0) - 1)
    def _():
        fin_ref[...] = state_ref[...]


def _ssm_call(u, bmat, cmat, a, init, with_y):
    n_rows = u.shape[0]
    tt = SSM_TT
    rows = N_GROUPS * tt
    n_t = n_rows // rows
    full = lambda arr: pl.BlockSpec(arr.shape, lambda j: (0,) * arr.ndim)
    in_specs = [
        pl.BlockSpec((rows, SSM_W), lambda j: (j, 0)),
        pl.BlockSpec((rows, SSM_W), lambda j: (n_t - 1 - j, 0)),
        full(bmat), full(cmat), full(a), full(init),
    ]
    state_shape = (2, SSM_LANE_BLOCKS, N_GROUPS, 2 * STATES_PER_BLOCK)
    out_shape = [jax.ShapeDtypeStruct(state_shape, F32)]
    out_specs = [pl.BlockSpec(state_shape, lambda j: (0, 0, 0, 0))]
    scratch = [pltpu.VMEM(state_shape, F32), pltpu.VMEM((2, 2, rows, 2 * STATES_PER_BLOCK), F32)]
    if with_y:
        out_shape = [jax.ShapeDtypeStruct((n_rows, SSM_W), F32)] * 2 + out_shape
        out_specs = [pl.BlockSpec((rows, SSM_W), lambda j: (j, 0)),
                     pl.BlockSpec((rows, SSM_W), lambda j: (n_t - 1 - j, 0))] + out_specs
        scratch.append(pltpu.VMEM((2, 2, rows, 2 * STATES_PER_BLOCK), BF16))
    return pl.pallas_call(
        functools.partial(_ssm_kernel, tt=tt, with_y=with_y),
        grid=(n_t,),
        in_specs=in_specs,
        out_specs=out_specs,
        out_shape=out_shape,
        scratch_shapes=scratch,
        compiler_params=pltpu.CompilerParams(
            dimension_semantics=("arbitrary",), vmem_limit_bytes=_vmem_limit(48 << 20)),
        name="ssm" if with_y else "ssm_ends",
    )(u, u, bmat, cmat, a, init)


def _carry_kernel(e_ref, as_ref, i_ref):
    S = STATES_PER_BLOCK
    row = lax.broadcasted_iota(jnp.int32, (N_GROUPS, S), 0)
    for d in range(2):
        first = 0 if d == 0 else N_GROUPS - 1
        shift = 1 if d == 0 else N_GROUPS - 1
        for q in range(SSM_LANE_BLOCKS):
            e = e_ref[d, q]
            a = jnp.broadcast_to(as_ref[d, q], (N_GROUPS, 2 * S))
            ere, eim, are, aim = e[:, :S], e[:, S:], a[:, :S], a[:, S:]
            ire = jnp.zeros((N_GROUPS, S), F32)
            iim = jnp.zeros((N_GROUPS, S), F32)
            for _ in range(N_GROUPS - 1):
                tre, tim = _cstep(ire, iim, are, aim, ere, eim)
                ire = jnp.where(row == first, 0.0, pltpu.roll(tre, shift, 0))
                iim = jnp.where(row == first, 0.0, pltpu.roll(tim, shift, 0))
            i_ref[d, q] = jnp.concatenate([ire, iim], axis=1)


def _carry_call(ends, a_seg):
    return pl.pallas_call(
        _carry_kernel,
        out_shape=jax.ShapeDtypeStruct(ends.shape, F32),
        name="ssm_carry",
    )(ends, a_seg)


def _ssm_post_kernel(yf_ref, yb_ref, u_ref, dsk_ref, wglu_ref, bglu_ref, sg_ref, permt_ref, o_ref, *, tt):
    y = yf_ref[...] + yb_ref[...] + dsk_ref[...] * u_ref[...]
    g = jax.nn.gelu(y)
    z = _dot(g.astype(BF16), wglu_ref[...]) + bglu_ref[...]
    s = g * jax.nn.sigmoid(z)
    sn = _rms(s, sg_ref[...]).astype(BF16)
    permt = permt_ref[...]
    sub = N_GROUPS * PERM_TT
    for k in range(tt // PERM_TT):
        blk = _dot(permt, sn[k * sub:(k + 1) * sub]).astype(BF16)
        o_ref[:, k * PERM_TT:(k + 1) * PERM_TT, :] = blk.reshape(N_GROUPS, PERM_TT, SSM_W)


def _ssm_post_call(yf, yb, u, w, permt):
    n_rows = u.shape[0]
    tt = PRE_TT
    rows = N_GROUPS * tt
    n_t = n_rows // rows
    lr = n_rows // N_GROUPS
    full = lambda a: pl.BlockSpec(a.shape, lambda j: (0,) * a.ndim)
    row_spec = pl.BlockSpec((rows, SSM_W), lambda j: (j, 0))
    weights = [w["dsk"], w["wglu"], w["bglu"], w["sg"], permt]
    return pl.pallas_call(
        functools.partial(_ssm_post_kernel, tt=tt),
        grid=(n_t,),
        in_specs=[row_spec, row_spec, row_spec] + [full(a) for a in weights],
        out_specs=pl.BlockSpec((N_GROUPS, tt, SSM_W), lambda j: (0, j, 0)),
        out_shape=jax.ShapeDtypeStruct((N_GROUPS, lr, SSM_W), BF16),
        compiler_params=pltpu.CompilerParams(
            dimension_semantics=("arbitrary",), vmem_limit_bytes=_vmem_limit(40 << 20)),
        name="ssm_post",
    )(yf, yb, u, *weights)


def _post_kernel(x_ref, a_ref, sn_ref, ag_ref, wo_ref, n2_ref, w1_ref, w2_ref, fg_ref, o_ref):
    an = _rms(a_ref[0], ag_ref[...]).astype(BF16)
    mixed = jnp.concatenate([an, sn_ref[0]], axis=-1)
    x1 = x_ref[0] + _dot(mixed, wo_ref[...])
    h2 = _rms(x1, n2_ref[...]).astype(BF16)
    acc = jnp.zeros_like(x1)
    for c in range(D_FF // FF_CHUNK):
        hid = _dot(h2, w1_ref[:, c * FF_CHUNK:(c + 1) * FF_CHUNK])
        hid = jnp.square(jnp.maximum(hid, 0.0)).astype(BF16)
        acc = acc + _dot(hid, w2_ref[c * FF_CHUNK:(c + 1) * FF_CHUNK, :])
    o_ref[0] = _rms(x1 + acc, fg_ref[...])


def _post_call(x, a, sn, w):
    nb, lr, _ = x.shape
    tm = POST_TM
    const = lambda arr: pl.BlockSpec(arr.shape, lambda b, i: (0,) * arr.ndim, pipeline_mode=pl.Buffered(1))
    weights = [w["ag"], w["wo"], w["n2"], w["w1"], w["w2"], w["fg"]]
    return pl.pallas_call(
        _post_kernel,
        grid=(nb, lr // tm),
        in_specs=[
            pl.BlockSpec((1, tm, D_MODEL), lambda b, i: (b, i, 0)),
            pl.BlockSpec((1, tm, N_HEADS * V_DIM), lambda b, i: (b, i, 0)),
            pl.BlockSpec((1, tm, SSM_W), lambda b, i: (b, i, 0)),
        ] + [const(arr) for arr in weights],
        out_specs=pl.BlockSpec((1, tm, D_MODEL), lambda b, i: (b, i, 0)),
        out_shape=jax.ShapeDtypeStruct((nb, lr, D_MODEL), F32),
        compiler_params=pltpu.CompilerParams(
            dimension_semantics=("arbitrary", "arbitrary"), vmem_limit_bytes=_vmem_limit(48 << 20)),
        name="post",
    )(x, a, sn, *weights)


def _rope_tables(length):
    inv = ROPE_THETA ** (-jnp.arange(0, QK_ROPE, 2, dtype=F32) / QK_ROPE)
    zeros = lambda n: jnp.zeros((n,), F32)
    inv_lanes = jnp.concatenate([zeros(QK_NOPE), inv, inv, zeros(HEAD_PAD - QK_NOPE - QK_ROPE)])
    n_hi = -(-length // ROPE_BLOCK)
    a_hi = (jnp.arange(n_hi, dtype=F32) * ROPE_BLOCK)[:, None] * inv_lanes
    a_lo = jnp.arange(ROPE_BLOCK, dtype=F32)[:, None] * inv_lanes
    ch, sh = jnp.cos(a_hi)[:, None], jnp.sin(a_hi)[:, None]
    cl, sl = jnp.cos(a_lo)[None], jnp.sin(a_lo)[None]
    used = (jnp.arange(HEAD_PAD) < QK_NOPE + QK_ROPE).astype(F32)
    cos_t = ((ch * cl - sh * sl) * used).reshape(n_hi * ROPE_BLOCK, HEAD_PAD)[:length]
    sin_t = (sh * cl + ch * sl).reshape(n_hi * ROPE_BLOCK, HEAD_PAD)[:length]
    return cos_t, sin_t


def _rot_half_cols(w):
    half = QK_ROPE // 2
    return jnp.concatenate([-w[..., half:], w[..., :half]], axis=-1)


def _pack_weights(norm1_g, w_in, q_norm_g, w_uq, kv_norm_g, w_ukv, d_skip, w_glu, b_glu,
                  attn_out_g, ssm_out_g, w_out, norm2_g, w_mlp1, w_mlp2, final_g):
    row = lambda g: g.reshape(1, -1).astype(F32)
    o = Q_RANK + KV_RANK
    w_kr = w_in[:, o:o + QK_ROPE]
    zk = jnp.zeros((D_MODEL, QK_NOPE), F32)
    wa = jnp.concatenate([w_in[:, :o], zk, w_kr, _rot_half_cols(w_kr)], axis=1)
    wu = w_in[:, o + QK_ROPE:]
    wq = w_uq.reshape(Q_RANK, N_HEADS, QK_NOPE + QK_ROPE)
    zq = jnp.zeros((Q_RANK, N_HEADS, HEAD_PAD - QK_NOPE - QK_ROPE), F32)
    wqa = jnp.concatenate([wq, zq], axis=-1).reshape(Q_RANK, N_HEADS * HEAD_PAD)
    wqb = jnp.concatenate([jnp.zeros((Q_RANK, N_HEADS, QK_NOPE), F32), _rot_half_cols(wq[..., QK_NOPE:]), zq],
                          axis=-1).reshape(Q_RANK, N_HEADS * HEAD_PAD)
    assert QK_NOPE + V_DIM == HEAD_PAD
    bf = lambda a: a.astype(BF16)
    return dict(
        n1=row(norm1_g), wa=bf(wa), wu=bf(wu), qg=row(q_norm_g), wqa=bf(wqa), wqb=bf(wqb),
        kvg=row(kv_norm_g), wkv=bf(w_ukv),
        dsk=row(d_skip), wglu=bf(w_glu), bglu=row(b_glu), sg=row(ssm_out_g),
        ag=row(attn_out_g), wo=bf(w_out), n2=row(norm2_g), w1=bf(w_mlp1), w2=bf(w_mlp2), fg=row(final_g))


def _pack_ssm(lam_re, lam_im, log_dt, b_re, b_im, c_re, c_im, seg_len):
    cmul = lambda xr, xi, yr, yi: (xr * yr - xi * yi, xr * yi + xi * yr)
    lam_re, lam_im = lam_re.astype(F32), lam_im.astype(F32)
    dt = jnp.exp(log_dt.astype(F32))[..., None]
    mag = jnp.exp(lam_re * dt)
    a_re, a_im = mag * jnp.cos(lam_im * dt), mag * jnp.sin(lam_im * dt)
    den = lam_re * lam_re + lam_im * lam_im
    k_re = ((a_re - 1.0) * lam_re + a_im * lam_im) / den
    k_im = (a_im * lam_re - (a_re - 1.0) * lam_im) / den
    bb_re, bb_im = cmul(k_re[..., None], k_im[..., None], b_re.astype(F32), b_im.astype(F32))
    p_re, p_im = a_re, a_im
    s_re, s_im = jnp.ones_like(a_re), jnp.zeros_like(a_re)
    n = seg_len
    while n:
        if n & 1:
            s_re, s_im = cmul(s_re, s_im, p_re, p_im)
        p_re, p_im = cmul(p_re, p_im, p_re, p_im)
        n >>= 1
    eye = jnp.eye(GROUPS_PER_BLOCK, dtype=F32)
    nq, gb = SSM_LANE_BLOCKS, GROUPS_PER_BLOCK

    def b_block(part):
        p = part.reshape(2, nq, gb, SSM_STATE, SSM_GROUP)
        return jnp.einsum("dqgnh,gk->dqghkn", p, eye).reshape(2, nq, LANES, STATES_PER_BLOCK)

    def c_block(part):
        p = part.reshape(2, nq, gb, SSM_GROUP, SSM_STATE)
        return jnp.einsum("dqghn,gk->dqgnkh", p, eye).reshape(2, nq, STATES_PER_BLOCK, LANES)

    def a_block(zr, zi):
        shape = (2, nq, 1, STATES_PER_BLOCK)
        return jnp.concatenate([zr.reshape(shape), zi.reshape(shape)], axis=-1)

    bmat = jnp.concatenate([b_block(bb_re), b_block(bb_im)], axis=-1).astype(BF16)
    cmat = jnp.concatenate([c_block(c_re.astype(F32)), -c_block(c_im.astype(F32))], axis=-2).astype(BF16)
    return bmat, cmat, a_block(a_re, a_im), a_block(s_re, s_im)


def _perm_matrix(tt):
    rows = N_GROUPS * tt
    dst = np.arange(rows)
    src = (dst % N_GROUPS) * tt + dst // N_GROUPS
    p = np.zeros((rows, rows), np.float32)
    p[dst, src] = 1.0
    return jnp.asarray(p, BF16)


def _trunk(x, rope, segmented, w, ssm):
    nb, lr, _ = x.shape
    bmat, cmat, a_blk, a_seg = ssm
    cos_t, sin_t = rope
    perm = _perm_matrix(PERM_TT)
    qt, k, vt, u = _pre_call(x, cos_t, sin_t, w, perm)
    seq_seg = (lambda a: a[None]) if segmented else (lambda a: a[:, None])
    att = _attn_call(seq_seg(qt), seq_seg(k), seq_seg(vt)).reshape(nb, lr, N_HEADS * V_DIM)
    init = jnp.zeros((2, SSM_LANE_BLOCKS, N_GROUPS, 2 * STATES_PER_BLOCK), F32)
    if segmented:
        (ends,) = _ssm_call(u, bmat, cmat, a_blk, init, with_y=False)
        init = _carry_call(ends, a_seg)
    yf, yb, _ = _ssm_call(u, bmat, cmat, a_blk, init, with_y=True)
    sn = _ssm_post_call(yf, yb, u, w, perm.T)
    return _post_call(x, att, sn, w)


def kernel(x_prompt, x_sample, norm1_g, w_in, q_norm_g, w_uq, kv_norm_g, w_ukv, lam_re, lam_im, log_dt,
           b_re, b_im, c_re, c_im, d_skip, w_glu, b_glu, attn_out_g, ssm_out_g, w_out, norm2_g, w_mlp1,
           w_mlp2, final_g):
    assert norm1_g.shape[0] == 1, "single-layer trunk"
    w = _pack_weights(norm1_g[0], w_in[0], q_norm_g[0], w_uq[0], kv_norm_g[0], w_ukv[0], d_skip[0], w_glu[0],
                      b_glu[0], attn_out_g[0], ssm_out_g[0], w_out[0], norm2_g[0], w_mlp1[0], w_mlp2[0], final_g)
    bp, lp, _ = x_prompt.shape
    bs, ls, _ = x_sample.shape
    assert bp == N_GROUPS and bs == 1 and ls % N_GROUPS == 0
    seg = ls // N_GROUPS
    ssm = _pack_ssm(lam_re[0], lam_im[0], log_dt[0], b_re[0], b_im[0], c_re[0], c_im[0], seg)
    cos_t, sin_t = _rope_tables(max(lp, ls))
    rope_p = (cos_t[None, :lp], sin_t[None, :lp])
    rope_s = (cos_t[:ls].reshape(N_GROUPS, seg, HEAD_PAD), sin_t[:ls].reshape(N_GROUPS, seg, HEAD_PAD))
    y_prompt = _trunk(x_prompt, rope_p, False, w, ssm)
    y_sample = _trunk(x_sample.reshape(N_GROUPS, seg, D_MODEL), rope_s, True, w, ssm)
    return y_prompt, y_sample.reshape(bs, ls, D_MODEL)
```

```python
import functools
import math

import jax
import jax.numpy as jnp
import numpy as np
from jax import lax
from jax.experimental import pallas as pl
from jax.experimental.pallas import tpu as pltpu

F32 = jnp.float32
BF16 = jnp.bfloat16

D_MODEL = 1024
N_HEADS = 8
QK_NOPE = 64
QK_ROPE = 32
V_DIM = 64
Q_RANK = 256
KV_RANK = 128
SSM_W = 512
SSM_GROUP = 16
SSM_GROUPS = 32
SSM_STATE = 64
D_FF = 4096
EPS = 1e-6
ROPE_THETA = 10000.0
ROPE_BLOCK = 128

LANES = 128
SUBLANES = 8
VMEM_LIMIT_CAP = 60000 * 1024

N_GROUPS = SUBLANES
HEAD_PAD = LANES
BF16_SUBLANES = 2 * SUBLANES
VT_ROWS = -(-(V_DIM + 1) // BF16_SUBLANES) * BF16_SUBLANES
N_PAIRS = N_HEADS // 2
SSM_LANE_BLOCKS = SSM_W // LANES
GROUPS_PER_BLOCK = LANES // SSM_GROUP
STATES_PER_BLOCK = GROUPS_PER_BLOCK * SSM_STATE

PRE_TT = 128
PERM_TT = 32
ATT_BQ = 512
ATT_BK = 256
ATT_UNROLL = 10
Q_SCALE = math.log2(math.e) / math.sqrt(QK_NOPE + QK_ROPE)
SSM_TT = 128
POST_TM = 512
FF_CHUNK = 1024


def _vmem_limit(nbytes):
    return int(min(VMEM_LIMIT_CAP, nbytes))


def _rms(x, g):
    return x * lax.rsqrt(jnp.mean(x * x, axis=-1, keepdims=True) + EPS) * g


def _dot(a, b):
    return jnp.dot(a, b, preferred_element_type=F32)


def _pre_kernel(x_ref, cos_ref, sin_ref, n1_ref, wa_ref, wu_ref, qg_ref, wqa_ref, wqb_ref,
                kvg_ref, wkv_ref, perm_ref, qt_ref, k_ref, vt_ref, u_ref, *, tt):
    rows = N_GROUPS * tt
    x = x_ref[...].reshape(rows, D_MODEL)
    h = _rms(x, n1_ref[...]).astype(BF16)
    pa = _dot(h, wa_ref[...])
    u = _dot(h, wu_ref[...])
    cqn = _rms(pa[:, :Q_RANK], qg_ref[...]).astype(BF16)
    ckvn = _rms(pa[:, Q_RANK:Q_RANK + KV_RANK], kvg_ref[...]).astype(BF16)
    cos = jnp.broadcast_to(cos_ref[...], (N_GROUPS, tt, HEAD_PAD)).reshape(rows, HEAD_PAD)
    sin = jnp.broadcast_to(sin_ref[...], (N_GROUPS, tt, HEAD_PAD)).reshape(rows, HEAD_PAD)
    o = Q_RANK + KV_RANK
    kr = pa[:, o:o + HEAD_PAD]
    k_rope = kr * cos + pltpu.roll(kr, HEAD_PAD - QK_ROPE, 1) * sin
    qa = _dot(cqn, wqa_ref[...])
    qb = _dot(cqn, wqb_ref[...])
    kv = _dot(ckvn, wkv_ref[...])
    nope_lanes = lax.broadcasted_iota(jnp.int32, (1, HEAD_PAD), 1) < QK_NOPE
    vt_tail = (lax.broadcasted_iota(jnp.int32, (VT_ROWS - V_DIM, tt), 0) == 0).astype(F32)
    for hd in range(N_HEADS):
        sl = slice(hd * HEAD_PAD, (hd + 1) * HEAD_PAD)
        qht = ((qa[:, sl] * cos + qb[:, sl] * sin) * Q_SCALE).T
        kvt = kv[:, sl].T
        for b in range(N_GROUPS):
            cols = slice(b * tt, (b + 1) * tt)
            qt_ref[b, hd] = qht[:, cols].astype(BF16)
            vt_ref[b, hd, 0] = jnp.concatenate([kvt[QK_NOPE:, cols], vt_tail], axis=0).astype(BF16)
        kh = jnp.where(nope_lanes, kv[:, sl], 0.0) + k_rope
        k_ref[:, hd] = kh.astype(BF16).reshape(N_GROUPS, tt, HEAD_PAD)
    u_hi = u.astype(BF16)
    u_lo = (u - u_hi.astype(F32)).astype(BF16)
    perm = perm_ref[...]
    sub = N_GROUPS * PERM_TT
    for s in range(tt // PERM_TT):
        pick = lambda a: jnp.concatenate(
            [a[b * tt + s * PERM_TT:b * tt + (s + 1) * PERM_TT] for b in range(N_GROUPS)], axis=0)
        u_ref[s * sub:(s + 1) * sub, :] = _dot(perm, pick(u_hi)) + _dot(perm, pick(u_lo))


def _pre_call(x, cos_t, sin_t, w, perm):
    nb, lr, _ = x.shape
    tt = PRE_TT
    rows = nb * tt
    n_t = lr // tt
    tab_nb = cos_t.shape[0]
    kt_sub = ATT_BK // tt
    full = lambda a: pl.BlockSpec(a.shape, lambda j: (0,) * a.ndim)
    weights = [w["n1"], w["wa"], w["wu"], w["qg"], w["wqa"], w["wqb"], w["kvg"], w["wkv"], perm]
    in_specs = [
        pl.BlockSpec((nb, tt, D_MODEL), lambda j: (0, j, 0)),
        pl.BlockSpec((tab_nb, tt, HEAD_PAD), lambda j: (0, j, 0)),
        pl.BlockSpec((tab_nb, tt, HEAD_PAD), lambda j: (0, j, 0)),
    ] + [full(a) for a in weights]
    out_shape = [
        jax.ShapeDtypeStruct((nb, N_HEADS, HEAD_PAD, lr), BF16),
        jax.ShapeDtypeStruct((nb, N_HEADS, lr, HEAD_PAD), BF16),
        jax.ShapeDtypeStruct((nb, N_HEADS, lr // ATT_BK, VT_ROWS, ATT_BK), BF16),
        jax.ShapeDtypeStruct((lr * nb, SSM_W), F32),
    ]
    out_specs = [
        pl.BlockSpec((nb, N_HEADS, HEAD_PAD, tt), lambda j: (0, 0, 0, j)),
        pl.BlockSpec((nb, N_HEADS, tt, HEAD_PAD), lambda j: (0, 0, j, 0)),
        pl.BlockSpec((nb, N_HEADS, 1, VT_ROWS, tt), lambda j: (0, 0, j // kt_sub, 0, j % kt_sub)),
        pl.BlockSpec((rows, SSM_W), lambda j: (j, 0)),
    ]
    return pl.pallas_call(
        functools.partial(_pre_kernel, tt=tt),
        grid=(n_t,),
        in_specs=in_specs,
        out_specs=out_specs,
        out_shape=out_shape,
        compiler_params=pltpu.CompilerParams(
            dimension_semantics=("arbitrary",), vmem_limit_bytes=_vmem_limit(56 << 20)),
        name="pre",
    )(x, cos_t, sin_t, *weights)


def _attn_kernel(qt_ref, k_ref, vt_ref, o_ref, s_ref, p_ref, alpha_ref, m_ref, acc_ref, *, n_seg, n_kb):
    bq, bk = ATT_BQ, ATT_BK
    m_ref[...] = jnp.full(m_ref.shape, -0.5 * float(np.finfo(np.float32).max), F32)
    acc_ref[...] = jnp.zeros(acc_ref.shape, F32)

    n = n_seg * n_kb

    def scores(c, slot):
        r0 = (c % n_kb) * bk
        if not isinstance(c, int):
            r0 = pl.multiple_of(r0, bk)
        for j in range(2):
            s_ref[slot, j, :, :bq] = _dot(k_ref[0, c // n_kb, j, pl.ds(r0, bk), :], qt_ref[0, 0, j])

    def softmax(slot):
        for j in range(2):
            for lt in range(bq // LANES):
                lanes = slice(lt * LANES, (lt + 1) * LANES)
                blk = s_ref[slot, j, :, lanes]
                m_old = m_ref[j, :, lanes]
                m_new = jnp.maximum(m_old, jnp.max(blk, axis=0, keepdims=True))
                alpha_ref[slot, j, :, lanes] = jnp.exp2(m_old - m_new)
                m_ref[j, :, lanes] = m_new
                p_ref[slot, j, :, lanes] = jnp.exp2(blk - m_new).astype(BF16)

    def weighted_values(c, slot):
        for j in range(2):
            pv = _dot(vt_ref[0, c // n_kb, j, c % n_kb], p_ref[slot, j, :, :bq])
            acc_ref[j] = acc_ref[j] * alpha_ref[slot, j] + pv

    steady = max(n - 2, 0)
    unroll = min(ATT_UNROLL, max(steady - steady % 2, 2))

    def stage(t, parity, do_scores=True, do_softmax=True, do_values=True):
        if do_values:
            weighted_values(t - 2, parity)
        if do_softmax:
            softmax(1 - parity)
        if do_scores:
            scores(t, parity)

    for t in range(2):
        stage(t, t % 2, t < n, 1 <= t <= n, False)
    n_iter = steady // unroll
    if n_iter:
        def body(k, _):
            for i in range(unroll):
                stage(2 + unroll * k + i, i % 2)
            return 0
        lax.fori_loop(0, n_iter, body, 0)
    for t in range(2 + unroll * n_iter, n + 2):
        stage(t, t % 2, t < n, t <= n, True)
    outs = [acc_ref[j][:V_DIM] / acc_ref[j][V_DIM:V_DIM + 1] for j in range(2)]
    o_ref[0] = jnp.concatenate(outs, axis=0).T


def _attn_call(qt, k, vt):
    n_seq, n_seg, _, lr, _ = k.shape
    n_kb = lr // ATT_BK
    n_qb = lr // ATT_BQ
    scratch = [
        pltpu.VMEM((2, 2, ATT_BK, ATT_BQ + LANES), F32),
        pltpu.VMEM((2, 2, ATT_BK, ATT_BQ + LANES), BF16),
        pltpu.VMEM((2, 2, 1, ATT_BQ), F32),
        pltpu.VMEM((2, 1, ATT_BQ), F32),
        pltpu.VMEM((2, VT_ROWS, ATT_BQ), F32),
    ]
    in_specs = [
        pl.BlockSpec((1, 1, 2, HEAD_PAD, ATT_BQ), lambda s, p, i: (s, i // n_qb, p, 0, i % n_qb)),
        pl.BlockSpec((1, n_seg, 2, lr, HEAD_PAD), lambda s, p, i: (s, 0, p, 0, 0)),
        pl.BlockSpec((1, n_seg, 2, n_kb, VT_ROWS, ATT_BK), lambda s, p, i: (s, 0, p, 0, 0, 0)),
    ]
    kv_bytes = 2 * n_seg * lr * (HEAD_PAD + VT_ROWS) * 2 * 2
    return pl.pallas_call(
        functools.partial(_attn_kernel, n_seg=n_seg, n_kb=n_kb),
        grid=(n_seq, N_PAIRS, n_seg * n_qb),
        in_specs=in_specs,
        out_specs=pl.BlockSpec((1, ATT_BQ, HEAD_PAD), lambda s, p, i: (s, i, p)),
        out_shape=jax.ShapeDtypeStruct((n_seq, n_seg * lr, N_HEADS * V_DIM), F32),
        scratch_shapes=scratch,
        compiler_params=pltpu.CompilerParams(
            dimension_semantics=("arbitrary", "arbitrary", "arbitrary"),
            vmem_limit_bytes=_vmem_limit(kv_bytes + (24 << 20))),
        name="attn",
    )(qt, k, vt)


def _cstep(xre, xim, are, aim, bre, bim):
    return are * xre - aim * xim + bre, are * xim + aim * xre + bim


def _ssm_kernel(uf_ref, ub_ref, bmat_ref, cmat_ref, a_ref, init_ref, *rest, tt, with_y):
    if with_y:
        yf_ref, yb_ref, fin_ref, state_ref, bu_ref, xs_ref = rest
    else:
        fin_ref, state_ref, bu_ref = rest
    j = pl.program_id(0)
    rows = N_GROUPS * tt
    n_pairs = tt // 2
    S = STATES_PER_BLOCK

    @pl.when(j == 0)
    def _():
        state_ref[...] = init_ref[...]

    for q in range(SSM_LANE_BLOCKS):
        lanes = slice(q * LANES, (q + 1) * LANES)
        slot = q % 2
        bu_ref[slot, 0] = _dot(uf_ref[:, lanes].astype(BF16), bmat_ref[0, q])
        bu_ref[slot, 1] = _dot(ub_ref[:, lanes].astype(BF16), bmat_ref[1, q])
        a = [jnp.broadcast_to(a_ref[d, q], (N_GROUPS, 2 * S)) for d in range(2)]
        are = [a[d][:, :S] for d in range(2)]
        aim = [a[d][:, S:] for d in range(2)]
        fwd = (state_ref[0, q, :, :S], state_ref[0, q, :, S:])
        bwd = (state_ref[1, q, :, :S], state_ref[1, q, :, S:])
        for k in range(n_pairs):
            r0 = k * 2 * N_GROUPS
            blk = bu_ref[slot, 0, r0:r0 + 2 * N_GROUPS, :]
            f1 = _cstep(*fwd, are[0], aim[0], blk[:N_GROUPS, :S], blk[:N_GROUPS, S:])
            fwd = _cstep(*f1, are[0], aim[0], blk[N_GROUPS:, :S], blk[N_GROUPS:, S:])
            r1 = rows - (k + 1) * 2 * N_GROUPS
            blk = bu_ref[slot, 1, r1:r1 + 2 * N_GROUPS, :]
            b1 = _cstep(*bwd, are[1], aim[1], blk[N_GROUPS:, :S], blk[N_GROUPS:, S:])
            bwd = _cstep(*b1, are[1], aim[1], blk[:N_GROUPS, :S], blk[:N_GROUPS, S:])
            if with_y:
                xs_ref[slot, 0, r0:r0 + 2 * N_GROUPS, :] = jnp.concatenate(
                    [jnp.concatenate(f1, axis=1), jnp.concatenate(fwd, axis=1)], axis=0).astype(BF16)
                xs_ref[slot, 1, r1:r1 + 2 * N_GROUPS, :] = jnp.concatenate(
                    [jnp.concatenate(bwd, axis=1), jnp.concatenate(b1, axis=1)], axis=0).astype(BF16)
        state_ref[0, q] = jnp.concatenate(fwd, axis=1)
        state_ref[1, q] = jnp.concatenate(bwd, axis=1)
        if with_y:
            yf_ref[:, lanes] = _dot(xs_ref[slot, 0], cmat_ref[0, q])
            yb_ref[:, lanes] = _dot(xs_ref[slot, 1], cmat_ref[1, q])

    @pl.when(j == pl.num_programs(0) - 1)
    def _():
        fin_ref[...] = state_ref[...]


def _ssm_call(u, bmat, cmat, a, init, with_y):
    n_rows = u.shape[0]
    tt = SSM_TT
    rows = N_GROUPS * tt
    n_t = n_rows // rows
    full = lambda arr: pl.BlockSpec(arr.shape, lambda j: (0,) * arr.ndim)
    in_specs = [
        pl.BlockSpec((rows, SSM_W), lambda j: (j, 0)),
        pl.BlockSpec((rows, SSM_W), lambda j: (n_t - 1 - j, 0)),
        full(bmat), full(cmat), full(a), full(init),
    ]
    state_shape = (2, SSM_LANE_BLOCKS, N_GROUPS, 2 * STATES_PER_BLOCK)
    out_shape = [jax.ShapeDtypeStruct(state_shape, F32)]
    out_specs = [pl.BlockSpec(state_shape, lambda j: (0, 0, 0, 0))]
    scratch = [pltpu.VMEM(state_shape, F32), pltpu.VMEM((2, 2, rows, 2 * STATES_PER_BLOCK), F32)]
    if with_y:
        out_shape = [jax.ShapeDtypeStruct((n_rows, SSM_W), F32)] * 2 + out_shape
        out_specs = [pl.BlockSpec((rows, SSM_W), lambda j: (j, 0)),
                     pl.BlockSpec((rows, SSM_W), lambda j: (n_t - 1 - j, 0))] + out_specs
        scratch.append(pltpu.VMEM((2, 2, rows, 2 * STATES_PER_BLOCK), BF16))
    return pl.pallas_call(
        functools.partial(_ssm_kernel, tt=tt, with_y=with_y),
        grid=(n_t,),
        in_specs=in_specs,
        out_specs=out_specs,
        out_shape=out_shape,
        scratch_shapes=scratch,
        compiler_params=pltpu.CompilerParams(
            dimension_semantics=("arbitrary",), vmem_limit_bytes=_vmem_limit(57 << 20)),
        name="ssm" if with_y else "ssm_ends",
    )(u, u, bmat, cmat, a, init)


def _carry_kernel(e_ref, as_ref, i_ref):
    S = STATES_PER_BLOCK
    row = lax.broadcasted_iota(jnp.int32, (N_GROUPS, S), 0)
    for d in range(2):
        first = 0 if d == 0 else N_GROUPS - 1
        shift = 1 if d == 0 else N_GROUPS - 1
        for q in range(SSM_LANE_BLOCKS):
            e = e_ref[d, q]
            a = jnp.broadcast_to(as_ref[d, q], (N_GROUPS, 2 * S))
            ere, eim, are, aim = e[:, :S], e[:, S:], a[:, :S], a[:, S:]
            ire = jnp.zeros((N_GROUPS, S), F32)
            iim = jnp.zeros((N_GROUPS, S), F32)
            for _ in range(N_GROUPS - 1):
                tre, tim = _cstep(ire, iim, are, aim, ere, eim)
                ire = jnp.where(row == first, 0.0, pltpu.roll(tre, shift, 0))
                iim = jnp.where(row == first, 0.0, pltpu.roll(tim, shift, 0))
            i_ref[d, q] = jnp.concatenate([ire, iim], axis=1)


def _carry_call(ends, a_seg):
    return pl.pallas_call(
        _carry_kernel,
        out_shape=jax.ShapeDtypeStruct(ends.shape, F32),
        name="ssm_carry",
    )(ends, a_seg)


def _ssm_post_kernel(yf_ref, yb_ref, u_ref, dsk_ref, wglu_ref, bglu_ref, sg_ref, permt_ref, o_ref, *, tt):
    y = yf_ref[...] + yb_ref[...] + dsk_ref[...] * u_ref[...]
    g = jax.nn.gelu(y)
    z = _dot(g.astype(BF16), wglu_ref[...]) + bglu_ref[...]
    s = g * jax.nn.sigmoid(z)
    sn = _rms(s, sg_ref[...]).astype(BF16)
    permt = permt_ref[...]
    sub = N_GROUPS * PERM_TT
    for k in range(tt // PERM_TT):
        blk = _dot(permt, sn[k * sub:(k + 1) * sub]).astype(BF16)
        o_ref[:, k * PERM_TT:(k + 1) * PERM_TT, :] = blk.reshape(N_GROUPS, PERM_TT, SSM_W)


def _ssm_post_call(yf, yb, u, w, permt):
    n_rows = u.shape[0]
    tt = PRE_TT
    rows = N_GROUPS * tt
    n_t = n_rows // rows
    lr = n_rows // N_GROUPS
    full = lambda a: pl.BlockSpec(a.shape, lambda j: (0,) * a.ndim)
    row_spec = pl.BlockSpec((rows, SSM_W), lambda j: (j, 0))
    weights = [w["dsk"], w["wglu"], w["bglu"], w["sg"], permt]
    return pl.pallas_call(
        functools.partial(_ssm_post_kernel, tt=tt),
        grid=(n_t,),
        in_specs=[row_spec, row_spec, row_spec] + [full(a) for a in weights],
        out_specs=pl.BlockSpec((N_GROUPS, tt, SSM_W), lambda j: (0, j, 0)),
        out_shape=jax.ShapeDtypeStruct((N_GROUPS, lr, SSM_W), BF16),
        compiler_params=pltpu.CompilerParams(
            dimension_semantics=("arbitrary",), vmem_limit_bytes=_vmem_limit(40 << 20)),
        name="ssm_post",
    )(yf, yb, u, *weights)


def _post_kernel(x_ref, a_ref, sn_ref, ag_ref, wo_ref, n2_ref, w1_ref, w2_ref, fg_ref, o_ref):
    an = _rms(a_ref[0], ag_ref[...]).astype(BF16)
    mixed = jnp.concatenate([an, sn_ref[0]], axis=-1)
    x1 = x_ref[0] + _dot(mixed, wo_ref[...])
    h2 = _rms(x1, n2_ref[...]).astype(BF16)
    acc = jnp.zeros_like(x1)
    for c in range(D_FF // FF_CHUNK):
        hid = _dot(h2, w1_ref[:, c * FF_CHUNK:(c + 1) * FF_CHUNK])
        hid = jnp.square(jnp.maximum(hid, 0.0)).astype(BF16)
        acc = acc + _dot(hid, w2_ref[c * FF_CHUNK:(c + 1) * FF_CHUNK, :])
    o_ref[0] = _rms(x1 + acc, fg_ref[...])


def _post_call(x, a, sn, w):
    nb, lr, _ = x.shape
    tm = POST_TM
    const = lambda arr: pl.BlockSpec(arr.shape, lambda b, i: (0,) * arr.ndim, pipeline_mode=pl.Buffered(1))
    weights = [w["ag"], w["wo"], w["n2"], w["w1"], w["w2"], w["fg"]]
    return pl.pallas_call(
        _post_kernel,
        grid=(nb, lr // tm),
        in_specs=[
            pl.BlockSpec((1, tm, D_MODEL), lambda b, i: (b, i, 0)),
            pl.BlockSpec((1, tm, N_HEADS * V_DIM), lambda b, i: (b, i, 0)),
            pl.BlockSpec((1, tm, SSM_W), lambda b, i: (b, i, 0)),
        ] + [const(arr) for arr in weights],
        out_specs=pl.BlockSpec((1, tm, D_MODEL), lambda b, i: (b, i, 0)),
        out_shape=jax.ShapeDtypeStruct((nb, lr, D_MODEL), F32),
        compiler_params=pltpu.CompilerParams(
            dimension_semantics=("arbitrary", "arbitrary"), vmem_limit_bytes=_vmem_limit(48 << 20)),
        name="post",
    )(x, a, sn, *weights)


def _rope_tables(length):
    inv = ROPE_THETA ** (-jnp.arange(0, QK_ROPE, 2, dtype=F32) / QK_ROPE)
    zeros = lambda n: jnp.zeros((n,), F32)
    inv_lanes = jnp.concatenate([zeros(QK_NOPE), inv, inv, zeros(HEAD_PAD - QK_NOPE - QK_ROPE)])
    n_hi = -(-length // ROPE_BLOCK)
    a_hi = (jnp.arange(n_hi, dtype=F32) * ROPE_BLOCK)[:, None] * inv_lanes
    a_lo = jnp.arange(ROPE_BLOCK, dtype=F32)[:, None] * inv_lanes
    ch, sh = jnp.cos(a_hi)[:, None], jnp.sin(a_hi)[:, None]
    cl, sl = jnp.cos(a_lo)[None], jnp.sin(a_lo)[None]
    used = (jnp.arange(HEAD_PAD) < QK_NOPE + QK_ROPE).astype(F32)
    cos_t = ((ch * cl - sh * sl) * used).reshape(n_hi * ROPE_BLOCK, HEAD_PAD)[:length]
    sin_t = (sh * cl + ch * sl).reshape(n_hi * ROPE_BLOCK, HEAD_PAD)[:length]
    return cos_t, sin_t


def _rot_half_cols(w):
    half = QK_ROPE // 2
    return jnp.concatenate([-w[..., half:], w[..., :half]], axis=-1)


def _pack_weights(norm1_g, w_in, q_norm_g, w_uq, kv_norm_g, w_ukv, d_skip, w_glu, b_glu,
                  attn_out_g, ssm_out_g, w_out, norm2_g, w_mlp1, w_mlp2, final_g):
    row = lambda g: g.reshape(1, -1).astype(F32)
    o = Q_RANK + KV_RANK
    w_kr = w_in[:, o:o + QK_ROPE]
    zk = jnp.zeros((D_MODEL, QK_NOPE), F32)
    wa = jnp.concatenate([w_in[:, :o], zk, w_kr, _rot_half_cols(w_kr)], axis=1)
    wu = w_in[:, o + QK_ROPE:]
    wq = w_uq.reshape(Q_RANK, N_HEADS, QK_NOPE + QK_ROPE)
    zq = jnp.zeros((Q_RANK, N_HEADS, HEAD_PAD - QK_NOPE - QK_ROPE), F32)
    wqa = jnp.concatenate([wq, zq], axis=-1).reshape(Q_RANK, N_HEADS * HEAD_PAD)
    wqb = jnp.concatenate([jnp.zeros((Q_RANK, N_HEADS, QK_NOPE), F32), _rot_half_cols(wq[..., QK_NOPE:]), zq],
                          axis=-1).reshape(Q_RANK, N_HEADS * HEAD_PAD)
    assert QK_NOPE + V_DIM == HEAD_PAD
    bf = lambda a: a.astype(BF16)
    return dict(
        n1=row(norm1_g), wa=bf(wa), wu=bf(wu), qg=row(q_norm_g), wqa=bf(wqa), wqb=bf(wqb),
        kvg=row(kv_norm_g), wkv=bf(w_ukv),
        dsk=row(d_skip), wglu=bf(w_glu), bglu=row(b_glu), sg=row(ssm_out_g),
        ag=row(attn_out_g), wo=bf(w_out), n2=row(norm2_g), w1=bf(w_mlp1), w2=bf(w_mlp2), fg=row(final_g))


def _pack_ssm(lam_re, lam_im, log_dt, b_re, b_im, c_re, c_im, seg_len):
    cmul = lambda xr, xi, yr, yi: (xr * yr - xi * yi, xr * yi + xi * yr)
    lam_re, lam_im = lam_re.astype(F32), lam_im.astype(F32)
    dt = jnp.exp(log_dt.astype(F32))[..., None]
    mag = jnp.exp(lam_re * dt)
    a_re, a_im = mag * jnp.cos(lam_im * dt), mag * jnp.sin(lam_im * dt)
    den = lam_re * lam_re + lam_im * lam_im
    k_re = ((a_re - 1.0) * lam_re + a_im * lam_im) / den
    k_im = (a_im * lam_re - (a_re - 1.0) * lam_im) / den
    bb_re, bb_im = cmul(k_re[..., None], k_im[..., None], b_re.astype(F32), b_im.astype(F32))
    p_re, p_im = a_re, a_im
    s_re, s_im = jnp.ones_like(a_re), jnp.zeros_like(a_re)
    n = seg_len
    while n:
        if n & 1:
            s_re, s_im = cmul(s_re, s_im, p_re, p_im)
        p_re, p_im = cmul(p_re, p_im, p_re, p_im)
        n >>= 1
    eye = jnp.eye(GROUPS_PER_BLOCK, dtype=F32)
    nq, gb = SSM_LANE_BLOCKS, GROUPS_PER_BLOCK

    def b_block(part):
        p = part.reshape(2, nq, gb, SSM_STATE, SSM_GROUP)
        return jnp.einsum("dqgnh,gk->dqghkn", p, eye).reshape(2, nq, LANES, STATES_PER_BLOCK)

    def c_block(part):
        p = part.reshape(2, nq, gb, SSM_GROUP, SSM_STATE)
        return jnp.einsum("dqghn,gk->dqgnkh", p, eye).reshape(2, nq, STATES_PER_BLOCK, LANES)

    def a_block(zr, zi):
        shape = (2, nq, 1, STATES_PER_BLOCK)
        return jnp.concatenate([zr.reshape(shape), zi.reshape(shape)], axis=-1)

    bmat = jnp.concatenate([b_block(bb_re), b_block(bb_im)], axis=-1).astype(BF16)
    cmat = jnp.concatenate([c_block(c_re.astype(F32)), -c_block(c_im.astype(F32))], axis=-2).astype(BF16)
    return bmat, cmat, a_block(a_re, a_im), a_block(s_re, s_im)


def _perm_matrix(tt):
    rows = N_GROUPS * tt
    dst = np.arange(rows)
    src = (dst % N_GROUPS) * tt + dst // N_GROUPS
    p = np.zeros((rows, rows), np.float32)
    p[dst, src] = 1.0
    return jnp.asarray(p, BF16)


def _trunk(x, rope, segmented, w, ssm):
    nb, lr, _ = x.shape
    bmat, cmat, a_blk, a_seg = ssm
    cos_t, sin_t = rope
    perm = _perm_matrix(PERM_TT)
    qt, k, vt, u = _pre_call(x, cos_t, sin_t, w, perm)
    seq_seg = (lambda a: a[None]) if segmented else (lambda a: a[:, None])
    att = _attn_call(seq_seg(qt), seq_seg(k), seq_seg(vt)).reshape(nb, lr, N_HEADS * V_DIM)
    init = jnp.zeros((2, SSM_LANE_BLOCKS, N_GROUPS, 2 * STATES_PER_BLOCK), F32)
    if segmented:
        (ends,) = _ssm_call(u, bmat, cmat, a_blk, init, with_y=False)
        init = _carry_call(ends, a_seg)
    yf, yb, _ = _ssm_call(u, bmat, cmat, a_blk, init, with_y=True)
    sn = _ssm_post_call(yf, yb, u, w, perm.T)
    return _post_call(x, att, sn, w)


def kernel(x_prompt, x_sample, norm1_g, w_in, q_norm_g, w_uq, kv_norm_g, w_ukv, lam_re, lam_im, log_dt,
           b_re, b_im, c_re, c_im, d_skip, w_glu, b_glu, attn_out_g, ssm_out_g, w_out, norm2_g, w_mlp1,
           w_mlp2, final_g):
    assert norm1_g.shape[0] == 1, "single-layer trunk"
    w = _pack_weights(norm1_g[0], w_in[0], q_norm_g[0], w_uq[0], kv_norm_g[0], w_ukv[0], d_skip[0], w_glu[0],
                      b_glu[0], attn_out_g[0], ssm_out_g[0], w_out[0], norm2_g[0], w_mlp1[0], w_mlp2[0], final_g)
    bp, lp, _ = x_prompt.shape
    bs, ls, _ = x_sample.shape
    assert bp == N_GROUPS and bs == 1 and ls % N_GROUPS == 0
    seg = ls // N_GROUPS
    ssm = _pack_ssm(lam_re[0], lam_im[0], log_dt[0], b_re[0], b_im[0], c_re[0], c_im[0], seg)
    cos_t, sin_t = _rope_tables(max(lp, ls))
    rope_p = (cos_t[None, :lp], sin_t[None, :lp])
    rope_s = (cos_t[:ls].reshape(N_GROUPS, seg, HEAD_PAD), sin_t[:ls].reshape(N_GROUPS, seg, HEAD_PAD))
    y_prompt = _trunk(x_prompt, rope_p, False, w, ssm)
    y_sample = _trunk(x_sample.reshape(N_GROUPS, seg, D_MODEL), rope_s, True, w, ssm)
    return y_prompt, y_sample.reshape(bs, ls, D_MODEL)
```

```python
import functools
import math

import jax
import jax.numpy as jnp
import numpy as np
from jax import lax
from jax.experimental import pallas as pl
from jax.experimental.pallas import tpu as pltpu

F32 = jnp.float32
BF16 = jnp.bfloat16

D_MODEL = 1024
N_HEADS = 8
QK_NOPE = 64
QK_ROPE = 32
V_DIM = 64
Q_RANK = 256
KV_RANK = 128
SSM_W = 512
SSM_GROUP = 16
SSM_GROUPS = 32
SSM_STATE = 64
D_FF = 4096
EPS = 1e-6
ROPE_THETA = 10000.0
ROPE_BLOCK = 128

LANES = 128
SUBLANES = 8
VMEM_LIMIT_CAP = 60000 * 1024

N_GROUPS = SUBLANES
HEAD_PAD = LANES
BF16_SUBLANES = 2 * SUBLANES
VT_ROWS = -(-(V_DIM + 1) // BF16_SUBLANES) * BF16_SUBLANES
N_PAIRS = N_HEADS // 2
SSM_LANE_BLOCKS = SSM_W // LANES
GROUPS_PER_BLOCK = LANES // SSM_GROUP
STATES_PER_BLOCK = GROUPS_PER_BLOCK * SSM_STATE

PRE_TT = 128
PERM_TT = 32
ATT_BQ = 512
ATT_BK = 256
ATT_UNROLL = 10
Q_SCALE = math.log2(math.e) / math.sqrt(QK_NOPE + QK_ROPE)
SSM_TT = 128
POST_TM = 1024
FF_CHUNK = 512


def _vmem_limit(nbytes):
    return int(min(VMEM_LIMIT_CAP, nbytes))


def _rms(x, g):
    return x * lax.rsqrt(jnp.mean(x * x, axis=-1, keepdims=True) + EPS) * g


def _dot(a, b):
    return jnp.dot(a, b, preferred_element_type=F32)


def _pre_kernel(x_ref, cos_ref, sin_ref, n1_ref, wa_ref, wu_ref, qg_ref, wqa_ref, wqb_ref,
                kvg_ref, wkv_ref, perm_ref, qt_ref, k_ref, vt_ref, u_ref, *, tt):
    rows = N_GROUPS * tt
    x = x_ref[...].reshape(rows, D_MODEL)
    h = _rms(x, n1_ref[...]).astype(BF16)
    pa = _dot(h, wa_ref[...])
    u = _dot(h, wu_ref[...])
    cqn = _rms(pa[:, :Q_RANK], qg_ref[...]).astype(BF16)
    ckvn = _rms(pa[:, Q_RANK:Q_RANK + KV_RANK], kvg_ref[...]).astype(BF16)
    cos = jnp.broadcast_to(cos_ref[...], (N_GROUPS, tt, HEAD_PAD)).reshape(rows, HEAD_PAD)
    sin = jnp.broadcast_to(sin_ref[...], (N_GROUPS, tt, HEAD_PAD)).reshape(rows, HEAD_PAD)
    o = Q_RANK + KV_RANK
    kr = pa[:, o:o + HEAD_PAD]
    k_rope = kr * cos + pltpu.roll(kr, HEAD_PAD - QK_ROPE, 1) * sin
    qa = _dot(cqn, wqa_ref[...])
    qb = _dot(cqn, wqb_ref[...])
    kv = _dot(ckvn, wkv_ref[...])
    nope_lanes = lax.broadcasted_iota(jnp.int32, (1, HEAD_PAD), 1) < QK_NOPE
    vt_tail = (lax.broadcasted_iota(jnp.int32, (VT_ROWS - V_DIM, tt), 0) == 0).astype(F32)
    for hd in range(N_HEADS):
        sl = slice(hd * HEAD_PAD, (hd + 1) * HEAD_PAD)
        qht = ((qa[:, sl] * cos + qb[:, sl] * sin) * Q_SCALE).T
        kvt = kv[:, sl].T
        for b in range(N_GROUPS):
            cols = slice(b * tt, (b + 1) * tt)
            qt_ref[b, hd] = qht[:, cols].astype(BF16)
            vt_ref[b, hd, 0] = jnp.concatenate([kvt[QK_NOPE:, cols], vt_tail], axis=0).astype(BF16)
        kh = jnp.where(nope_lanes, kv[:, sl], 0.0) + k_rope
        k_ref[:, hd] = kh.astype(BF16).reshape(N_GROUPS, tt, HEAD_PAD)
    u_hi = u.astype(BF16)
    u_lo = (u - u_hi.astype(F32)).astype(BF16)
    perm = perm_ref[...]
    sub = N_GROUPS * PERM_TT
    for s in range(tt // PERM_TT):
        pick = lambda a: jnp.concatenate(
            [a[b * tt + s * PERM_TT:b * tt + (s + 1) * PERM_TT] for b in range(N_GROUPS)], axis=0)
        u_ref[s * sub:(s + 1) * sub, :] = _dot(perm, pick(u_hi)) + _dot(perm, pick(u_lo))


def _pre_call(x, cos_t, sin_t, w, perm):
    nb, lr, _ = x.shape
    tt = PRE_TT
    rows = nb * tt
    n_t = lr // tt
    tab_nb = cos_t.shape[0]
    kt_sub = ATT_BK // tt
    full = lambda a: pl.BlockSpec(a.shape, lambda j: (0,) * a.ndim)
    weights = [w["n1"], w["wa"], w["wu"], w["qg"], w["wqa"], w["wqb"], w["kvg"], w["wkv"], perm]
    in_specs = [
        pl.BlockSpec((nb, tt, D_MODEL), lambda j: (0, j, 0)),
        pl.BlockSpec((tab_nb, tt, HEAD_PAD), lambda j: (0, j, 0)),
        pl.BlockSpec((tab_nb, tt, HEAD_PAD), lambda j: (0, j, 0)),
    ] + [full(a) for a in weights]
    out_shape = [
        jax.ShapeDtypeStruct((nb, N_HEADS, HEAD_PAD, lr), BF16),
        jax.ShapeDtypeStruct((nb, N_HEADS, lr, HEAD_PAD), BF16),
        jax.ShapeDtypeStruct((nb, N_HEADS, lr // ATT_BK, VT_ROWS, ATT_BK), BF16),
        jax.ShapeDtypeStruct((lr * nb, SSM_W), F32),
    ]
    out_specs = [
        pl.BlockSpec((nb, N_HEADS, HEAD_PAD, tt), lambda j: (0, 0, 0, j)),
        pl.BlockSpec((nb, N_HEADS, tt, HEAD_PAD), lambda j: (0, 0, j, 0)),
        pl.BlockSpec((nb, N_HEADS, 1, VT_ROWS, tt), lambda j: (0, 0, j // kt_sub, 0, j % kt_sub)),
        pl.BlockSpec((rows, SSM_W), lambda j: (j, 0)),
    ]
    return pl.pallas_call(
        functools.partial(_pre_kernel, tt=tt),
        grid=(n_t,),
        in_specs=in_specs,
        out_specs=out_specs,
        out_shape=out_shape,
        compiler_params=pltpu.CompilerParams(
            dimension_semantics=("arbitrary",), vmem_limit_bytes=_vmem_limit(56 << 20)),
        name="pre",
    )(x, cos_t, sin_t, *weights)


def _attn_kernel(qt_ref, k_ref, vt_ref, o_ref, s_ref, p_ref, alpha_ref, m_ref, acc_ref, *, n_seg, n_kb):
    bq, bk = ATT_BQ, ATT_BK
    m_ref[...] = jnp.full(m_ref.shape, -0.5 * float(np.finfo(np.float32).max), F32)
    acc_ref[...] = jnp.zeros(acc_ref.shape, F32)

    n = n_seg * n_kb

    def scores(c, slot):
        r0 = (c % n_kb) * bk
        if not isinstance(c, int):
            r0 = pl.multiple_of(r0, bk)
        for j in range(2):
            s_ref[slot, j, :, :bq] = _dot(k_ref[0, c // n_kb, j, pl.ds(r0, bk), :], qt_ref[0, 0, j])

    def softmax(slot):
        for j in range(2):
            for lt in range(bq // LANES):
                lanes = slice(lt * LANES, (lt + 1) * LANES)
                blk = s_ref[slot, j, :, lanes]
                m_old = m_ref[j, :, lanes]
                m_new = jnp.maximum(m_old, jnp.max(blk, axis=0, keepdims=True))
                alpha_ref[slot, j, :, lanes] = jnp.exp2(m_old - m_new)
                m_ref[j, :, lanes] = m_new
                p_ref[slot, j, :, lanes] = jnp.exp2(blk - m_new).astype(BF16)

    def weighted_values(c, slot):
        for j in range(2):
            pv = _dot(vt_ref[0, c // n_kb, j, c % n_kb], p_ref[slot, j, :, :bq])
            acc_ref[j] = acc_ref[j] * alpha_ref[slot, j] + pv

    steady = max(n - 2, 0)
    unroll = min(ATT_UNROLL, max(steady - steady % 2, 2))

    def stage(t, parity, do_scores=True, do_softmax=True, do_values=True):
        if do_values:
            weighted_values(t - 2, parity)
        if do_softmax:
            softmax(1 - parity)
        if do_scores:
            scores(t, parity)

    for t in range(2):
        stage(t, t % 2, t < n, 1 <= t <= n, False)
    n_iter = steady // unroll
    if n_iter:
        def body(k, _):
            for i in range(unroll):
                stage(2 + unroll * k + i, i % 2)
            return 0
        lax.fori_loop(0, n_iter, body, 0)
    for t in range(2 + unroll * n_iter, n + 2):
        stage(t, t % 2, t < n, t <= n, True)
    outs = [acc_ref[j][:V_DIM] / acc_ref[j][V_DIM:V_DIM + 1] for j in range(2)]
    o_ref[0] = jnp.concatenate(outs, axis=0).T


def _attn_call(qt, k, vt):
    n_seq, n_seg, _, lr, _ = k.shape
    n_kb = lr // ATT_BK
    n_qb = lr // ATT_BQ
    scratch = [
        pltpu.VMEM((2, 2, ATT_BK, ATT_BQ + LANES), F32),
        pltpu.VMEM((2, 2, ATT_BK, ATT_BQ + LANES), BF16),
        pltpu.VMEM((2, 2, 1, ATT_BQ), F32),
        pltpu.VMEM((2, 1, ATT_BQ), F32),
        pltpu.VMEM((2, VT_ROWS, ATT_BQ), F32),
    ]
    in_specs = [
        pl.BlockSpec((1, 1, 2, HEAD_PAD, ATT_BQ), lambda s, p, i: (s, i // n_qb, p, 0, i % n_qb)),
        pl.BlockSpec((1, n_seg, 2, lr, HEAD_PAD), lambda s, p, i: (s, 0, p, 0, 0)),
        pl.BlockSpec((1, n_seg, 2, n_kb, VT_ROWS, ATT_BK), lambda s, p, i: (s, 0, p, 0, 0, 0)),
    ]
    kv_bytes = 2 * n_seg * lr * (HEAD_PAD + VT_ROWS) * 2 * 2
    return pl.pallas_call(
        functools.partial(_attn_kernel, n_seg=n_seg, n_kb=n_kb),
        grid=(n_seq, N_PAIRS, n_seg * n_qb),
        in_specs=in_specs,
        out_specs=pl.BlockSpec((1, ATT_BQ, HEAD_PAD), lambda s, p, i: (s, i, p)),
        out_shape=jax.ShapeDtypeStruct((n_seq, n_seg * lr, N_HEADS * V_DIM), F32),
        scratch_shapes=scratch,
        compiler_params=pltpu.CompilerParams(
            dimension_semantics=("arbitrary", "arbitrary", "arbitrary"),
            vmem_limit_bytes=_vmem_limit(kv_bytes + (24 << 20))),
        name="attn",
    )(qt, k, vt)


def _cstep(xre, xim, are, aim, bre, bim):
    return are * xre - aim * xim + bre, are * xim + aim * xre + bim


def _ssm_kernel(uf_ref, ub_ref, bmat_ref, cmat_ref, a_ref, init_ref, *rest, tt, with_y):
    if with_y:
        yf_ref, yb_ref, fin_ref, state_ref, bu_ref, xs_ref = rest
    else:
        fin_ref, state_ref, bu_ref = rest
    j = pl.program_id(0)
    rows = N_GROUPS * tt
    n_pairs = tt // 2
    S = STATES_PER_BLOCK

    @pl.when(j == 0)
    def _():
        state_ref[...] = init_ref[...]

    for q in range(SSM_LANE_BLOCKS):
        lanes = slice(q * LANES, (q + 1) * LANES)
        slot = q % 2
        bu_ref[slot, 0] = _dot(uf_ref[:, lanes].astype(BF16), bmat_ref[0, q])
        bu_ref[slot, 1] = _dot(ub_ref[:, lanes].astype(BF16), bmat_ref[1, q])
        a = [jnp.broadcast_to(a_ref[d, q], (N_GROUPS, 2 * S)) for d in range(2)]
        are = [a[d][:, :S] for d in range(2)]
        aim = [a[d][:, S:] for d in range(2)]
        fwd = (state_ref[0, q, :, :S], state_ref[0, q, :, S:])
        bwd = (state_ref[1, q, :, :S], state_ref[1, q, :, S:])
        for k in range(n_pairs):
            r0 = k * 2 * N_GROUPS
            blk = bu_ref[slot, 0, r0:r0 + 2 * N_GROUPS, :]
            f1 = _cstep(*fwd, are[0], aim[0], blk[:N_GROUPS, :S], blk[:N_GROUPS, S:])
            fwd = _cstep(*f1, are[0], aim[0], blk[N_GROUPS:, :S], blk[N_GROUPS:, S:])
            r1 = rows - (k + 1) * 2 * N_GROUPS
            blk = bu_ref[slot, 1, r1:r1 + 2 * N_GROUPS, :]
            b1 = _cstep(*bwd, are[1], aim[1], blk[N_GROUPS:, :S], blk[N_GROUPS:, S:])
            bwd = _cstep(*b1, are[1], aim[1], blk[:N_GROUPS, :S], blk[:N_GROUPS, S:])
            if with_y:
                xs_ref[slot, 0, r0:r0 + 2 * N_GROUPS, :] = jnp.concatenate(
                    [jnp.concatenate(f1, axis=1), jnp.concatenate(fwd, axis=1)], axis=0).astype(BF16)
                xs_ref[slot, 1, r1:r1 + 2 * N_GROUPS, :] = jnp.concatenate(
                    [jnp.concatenate(bwd, axis=1), jnp.concatenate(b1, axis=1)], axis=0).astype(BF16)
        state_ref[0, q] = jnp.concatenate(fwd, axis=1)
        state_ref[1, q] = jnp.concatenate(bwd, axis=1)
        if with_y:
            yf_ref[:, lanes] = _dot(xs_ref[slot, 0], cmat_ref[0, q])
            yb_ref[:, lanes] = _dot(xs_ref[slot, 1], cmat_ref[1, q])

    @pl.when(j == pl.num_programs(0) - 1)
    def _():
        fin_ref[...] = state_ref[...]


def _ssm_call(u, bmat, cmat, a, init, with_y):
    n_rows = u.shape[0]
    tt = SSM_TT
    rows = N_GROUPS * tt
    n_t = n_rows // rows
    full = lambda arr: pl.BlockSpec(arr.shape, lambda j: (0,) * arr.ndim)
    in_specs = [
        pl.BlockSpec((rows, SSM_W), lambda j: (j, 0)),
        pl.BlockSpec((rows, SSM_W), lambda j: (n_t - 1 - j, 0)),
        full(bmat), full(cmat), full(a), full(init),
    ]
    state_shape = (2, SSM_LANE_BLOCKS, N_GROUPS, 2 * STATES_PER_BLOCK)
    out_shape = [jax.ShapeDtypeStruct(state_shape, F32)]
    out_specs = [pl.BlockSpec(state_shape, lambda j: (0, 0, 0, 0))]
    scratch = [pltpu.VMEM(state_shape, F32), pltpu.VMEM((2, 2, rows, 2 * STATES_PER_BLOCK), F32)]
    if with_y:
        out_shape = [jax.ShapeDtypeStruct((n_rows, SSM_W), F32)] * 2 + out_shape
        out_specs = [pl.BlockSpec((rows, SSM_W), lambda j: (j, 0)),
                     pl.BlockSpec((rows, SSM_W), lambda j: (n_t - 1 - j, 0))] + out_specs
        scratch.append(pltpu.VMEM((2, 2, rows, 2 * STATES_PER_BLOCK), BF16))
    return pl.pallas_call(
        functools.partial(_ssm_kernel, tt=tt, with_y=with_y),
        grid=(n_t,),
        in_specs=in_specs,
        out_specs=out_specs,
        out_shape=out_shape,
        scratch_shapes=scratch,
        compiler_params=pltpu.CompilerParams(
            dimension_semantics=("arbitrary",), vmem_limit_bytes=_vmem_limit(57 << 20)),
        name="ssm" if with_y else "ssm_ends",
    )(u, u, bmat, cmat, a, init)


def _carry_kernel(e_ref, as_ref, i_ref):
    S = STATES_PER_BLOCK
    row = lax.broadcasted_iota(jnp.int32, (N_GROUPS, S), 0)
    for d in range(2):
        first = 0 if d == 0 else N_GROUPS - 1
        shift = 1 if d == 0 else N_GROUPS - 1
        for q in range(SSM_LANE_BLOCKS):
            e = e_ref[d, q]
            a = jnp.broadcast_to(as_ref[d, q], (N_GROUPS, 2 * S))
            ere, eim, are, aim = e[:, :S], e[:, S:], a[:, :S], a[:, S:]
            ire = jnp.zeros((N_GROUPS, S), F32)
            iim = jnp.zeros((N_GROUPS, S), F32)
            for _ in range(N_GROUPS - 1):
                tre, tim = _cstep(ire, iim, are, aim, ere, eim)
                ire = jnp.where(row == first, 0.0, pltpu.roll(tre, shift, 0))
                iim = jnp.where(row == first, 0.0, pltpu.roll(tim, shift, 0))
            i_ref[d, q] = jnp.concatenate([ire, iim], axis=1)


def _carry_call(ends, a_seg):
    return pl.pallas_call(
        _carry_kernel,
        out_shape=jax.ShapeDtypeStruct(ends.shape, F32),
        name="ssm_carry",
    )(ends, a_seg)


def _ssm_post_kernel(yf_ref, yb_ref, u_ref, dsk_ref, wglu_ref, bglu_ref, sg_ref, permt_ref, o_ref, *, tt):
    y = yf_ref[...] + yb_ref[...] + dsk_ref[...] * u_ref[...]
    g = jax.nn.gelu(y)
    z = _dot(g.astype(BF16), wglu_ref[...]) + bglu_ref[...]
    s = g * jax.nn.sigmoid(z)
    sn = _rms(s, sg_ref[...]).astype(BF16)
    permt = permt_ref[...]
    sub = N_GROUPS * PERM_TT
    for k in range(tt // PERM_TT):
        blk = _dot(permt, sn[k * sub:(k + 1) * sub]).astype(BF16)
        o_ref[:, k * PERM_TT:(k + 1) * PERM_TT, :] = blk.reshape(N_GROUPS, PERM_TT, SSM_W)


def _ssm_post_call(yf, yb, u, w, permt):
    n_rows = u.shape[0]
    tt = PRE_TT
    rows = N_GROUPS * tt
    n_t = n_rows // rows
    lr = n_rows // N_GROUPS
    full = lambda a: pl.BlockSpec(a.shape, lambda j: (0,) * a.ndim)
    row_spec = pl.BlockSpec((rows, SSM_W), lambda j: (j, 0))
    weights = [w["dsk"], w["wglu"], w["bglu"], w["sg"], permt]
    return pl.pallas_call(
        functools.partial(_ssm_post_kernel, tt=tt),
        grid=(n_t,),
        in_specs=[row_spec, row_spec, row_spec] + [full(a) for a in weights],
        out_specs=pl.BlockSpec((N_GROUPS, tt, SSM_W), lambda j: (0, j, 0)),
        out_shape=jax.ShapeDtypeStruct((N_GROUPS, lr, SSM_W), BF16),
        compiler_params=pltpu.CompilerParams(
            dimension_semantics=("arbitrary",), vmem_limit_bytes=_vmem_limit(40 << 20)),
        name="ssm_post",
    )(yf, yb, u, *weights)


def _post_kernel(x_ref, a_ref, sn_ref, ag_ref, wo_ref, n2_ref, w1_ref, w2_ref, fg_ref, o_ref):
    an = _rms(a_ref[0], ag_ref[...]).astype(BF16)
    mixed = jnp.concatenate([an, sn_ref[0]], axis=-1)
    x1 = x_ref[0] + _dot(mixed, wo_ref[...])
    h2 = _rms(x1, n2_ref[...]).astype(BF16)
    acc = jnp.zeros_like(x1)
    for c in range(D_FF // FF_CHUNK):
        hid = _dot(h2, w1_ref[:, c * FF_CHUNK:(c + 1) * FF_CHUNK])
        hid = jnp.square(jnp.maximum(hid, 0.0)).astype(BF16)
        acc = acc + _dot(hid, w2_ref[c * FF_CHUNK:(c + 1) * FF_CHUNK, :])
    o_ref[0] = _rms(x1 + acc, fg_ref[...])


def _post_call(x, a, sn, w):
    nb, lr, _ = x.shape
    tm = POST_TM
    const = lambda arr: pl.BlockSpec(arr.shape, lambda b, i: (0,) * arr.ndim, pipeline_mode=pl.Buffered(1))
    weights = [w["ag"], w["wo"], w["n2"], w["w1"], w["w2"], w["fg"]]
    return pl.pallas_call(
        _post_kernel,
        grid=(nb, lr // tm),
        in_specs=[
            pl.BlockSpec((1, tm, D_MODEL), lambda b, i: (b, i, 0)),
            pl.BlockSpec((1, tm, N_HEADS * V_DIM), lambda b, i: (b, i, 0)),
            pl.BlockSpec((1, tm, SSM_W), lambda b, i: (b, i, 0)),
        ] + [const(arr) for arr in weights],
        out_specs=pl.BlockSpec((1, tm, D_MODEL), lambda b, i: (b, i, 0)),
        out_shape=jax.ShapeDtypeStruct((nb, lr, D_MODEL), F32),
        compiler_params=pltpu.CompilerParams(
            dimension_semantics=("arbitrary", "arbitrary"), vmem_limit_bytes=_vmem_limit(57 << 20)),
        name="post",
    )(x, a, sn, *weights)


def _rope_tables(length):
    inv = ROPE_THETA ** (-jnp.arange(0, QK_ROPE, 2, dtype=F32) / QK_ROPE)
    zeros = lambda n: jnp.zeros((n,), F32)
    inv_lanes = jnp.concatenate([zeros(QK_NOPE), inv, inv, zeros(HEAD_PAD - QK_NOPE - QK_ROPE)])
    n_hi = -(-length // ROPE_BLOCK)
    a_hi = (jnp.arange(n_hi, dtype=F32) * ROPE_BLOCK)[:, None] * inv_lanes
    a_lo = jnp.arange(ROPE_BLOCK, dtype=F32)[:, None] * inv_lanes
    ch, sh = jnp.cos(a_hi)[:, None], jnp.sin(a_hi)[:, None]
    cl, sl = jnp.cos(a_lo)[None], jnp.sin(a_lo)[None]
    used = (jnp.arange(HEAD_PAD) < QK_NOPE + QK_ROPE).astype(F32)
    cos_t = ((ch * cl - sh * sl) * used).reshape(n_hi * ROPE_BLOCK, HEAD_PAD)[:length]
    sin_t = (sh * cl + ch * sl).reshape(n_hi * ROPE_BLOCK, HEAD_PAD)[:length]
    return cos_t, sin_t


def _rot_half_cols(w):
    half = QK_ROPE // 2
    return jnp.concatenate([-w[..., half:], w[..., :half]], axis=-1)


def _pack_weights(norm1_g, w_in, q_norm_g, w_uq, kv_norm_g, w_ukv, d_skip, w_glu, b_glu,
                  attn_out_g, ssm_out_g, w_out, norm2_g, w_mlp1, w_mlp2, final_g):
    row = lambda g: g.reshape(1, -1).astype(F32)
    o = Q_RANK + KV_RANK
    w_kr = w_in[:, o:o + QK_ROPE]
    zk = jnp.zeros((D_MODEL, QK_NOPE), F32)
    wa = jnp.concatenate([w_in[:, :o], zk, w_kr, _rot_half_cols(w_kr)], axis=1)
    wu = w_in[:, o + QK_ROPE:]
    wq = w_uq.reshape(Q_RANK, N_HEADS, QK_NOPE + QK_ROPE)
    zq = jnp.zeros((Q_RANK, N_HEADS, HEAD_PAD - QK_NOPE - QK_ROPE), F32)
    wqa = jnp.concatenate([wq, zq], axis=-1).reshape(Q_RANK, N_HEADS * HEAD_PAD)
    wqb = jnp.concatenate([jnp.zeros((Q_RANK, N_HEADS, QK_NOPE), F32), _rot_half_cols(wq[..., QK_NOPE:]), zq],
                          axis=-1).reshape(Q_RANK, N_HEADS * HEAD_PAD)
    assert QK_NOPE + V_DIM == HEAD_PAD
    bf = lambda a: a.astype(BF16)
    return dict(
        n1=row(norm1_g), wa=bf(wa), wu=bf(wu), qg=row(q_norm_g), wqa=bf(wqa), wqb=bf(wqb),
        kvg=row(kv_norm_g), wkv=bf(w_ukv),
        dsk=row(d_skip), wglu=bf(w_glu), bglu=row(b_glu), sg=row(ssm_out_g),
        ag=row(attn_out_g), wo=bf(w_out), n2=row(norm2_g), w1=bf(w_mlp1), w2=bf(w_mlp2), fg=row(final_g))


def _pack_ssm(lam_re, lam_im, log_dt, b_re, b_im, c_re, c_im, seg_len):
    cmul = lambda xr, xi, yr, yi: (xr * yr - xi * yi, xr * yi + xi * yr)
    lam_re, lam_im = lam_re.astype(F32), lam_im.astype(F32)
    dt = jnp.exp(log_dt.astype(F32))[..., None]
    mag = jnp.exp(lam_re * dt)
    a_re, a_im = mag * jnp.cos(lam_im * dt), mag * jnp.sin(lam_im * dt)
    den = lam_re * lam_re + lam_im * lam_im
    k_re = ((a_re - 1.0) * lam_re + a_im * lam_im) / den
    k_im = (a_im * lam_re - (a_re - 1.0) * lam_im) / den
    bb_re, bb_im = cmul(k_re[..., None], k_im[..., None], b_re.astype(F32), b_im.astype(F32))
    p_re, p_im = a_re, a_im
    s_re, s_im = jnp.ones_like(a_re), jnp.zeros_like(a_re)
    n = seg_len
    while n:
        if n & 1:
            s_re, s_im = cmul(s_re, s_im, p_re, p_im)
        p_re, p_im = cmul(p_re, p_im, p_re, p_im)
        n >>= 1
    eye = jnp.eye(GROUPS_PER_BLOCK, dtype=F32)
    nq, gb = SSM_LANE_BLOCKS, GROUPS_PER_BLOCK

    def b_block(part):
        p = part.reshape(2, nq, gb, SSM_STATE, SSM_GROUP)
        return jnp.einsum("dqgnh,gk->dqghkn", p, eye).reshape(2, nq, LANES, STATES_PER_BLOCK)

    def c_block(part):
        p = part.reshape(2, nq, gb, SSM_GROUP, SSM_STATE)
        return jnp.einsum("dqghn,gk->dqgnkh", p, eye).reshape(2, nq, STATES_PER_BLOCK, LANES)

    def a_block(zr, zi):
        shape = (2, nq, 1, STATES_PER_BLOCK)
        return jnp.concatenate([zr.reshape(shape), zi.reshape(shape)], axis=-1)

    bmat = jnp.concatenate([b_block(bb_re), b_block(bb_im)], axis=-1).astype(BF16)
    cmat = jnp.concatenate([c_block(c_re.astype(F32)), -c_block(c_im.astype(F32))], axis=-2).astype(BF16)
    return bmat, cmat, a_block(a_re, a_im), a_block(s_re, s_im)


def _perm_matrix(tt):
    rows = N_GROUPS * tt
    dst = np.arange(rows)
    src = (dst % N_GROUPS) * tt + dst // N_GROUPS
    p = np.zeros((rows, rows), np.float32)
    p[dst, src] = 1.0
    return jnp.asarray(p, BF16)


def _trunk(x, rope, segmented, w, ssm):
    nb, lr, _ = x.shape
    bmat, cmat, a_blk, a_seg = ssm
    cos_t, sin_t = rope
    perm = _perm_matrix(PERM_TT)
    qt, k, vt, u = _pre_call(x, cos_t, sin_t, w, perm)
    seq_seg = (lambda a: a[None]) if segmented else (lambda a: a[:, None])
    att = _attn_call(seq_seg(qt), seq_seg(k), seq_seg(vt)).reshape(nb, lr, N_HEADS * V_DIM)
    init = jnp.zeros((2, SSM_LANE_BLOCKS, N_GROUPS, 2 * STATES_PER_BLOCK), F32)
    if segmented:
        (ends,) = _ssm_call(u, bmat, cmat, a_blk, init, with_y=False)
        init = _carry_call(ends, a_seg)
    yf, yb, _ = _ssm_call(u, bmat, cmat, a_blk, init, with_y=True)
    sn = _ssm_post_call(yf, yb, u, w, perm.T)
    return _post_call(x, att, sn, w)


def kernel(x_prompt, x_sample, norm1_g, w_in, q_norm_g, w_uq, kv_norm_g, w_ukv, lam_re, lam_im, log_dt,
           b_re, b_im, c_re, c_im, d_skip, w_glu, b_glu, attn_out_g, ssm_out_g, w_out, norm2_g, w_mlp1,
           w_mlp2, final_g):
    assert norm1_g.shape[0] == 1, "single-layer trunk"
    w = _pack_weights(norm1_g[0], w_in[0], q_norm_g[0], w_uq[0], kv_norm_g[0], w_ukv[0], d_skip[0], w_glu[0],
                      b_glu[0], attn_out_g[0], ssm_out_g[0], w_out[0], norm2_g[0], w_mlp1[0], w_mlp2[0], final_g)
    bp, lp, _ = x_prompt.shape
    bs, ls, _ = x_sample.shape
    assert bp == N_GROUPS and bs == 1 and ls % N_GROUPS == 0
    seg = ls // N_GROUPS
    ssm = _pack_ssm(lam_re[0], lam_im[0], log_dt[0], b_re[0], b_im[0], c_re[0], c_im[0], seg)
    cos_t, sin_t = _rope_tables(max(lp, ls))
    rope_p = (cos_t[None, :lp], sin_t[None, :lp])
    rope_s = (cos_t[:ls].reshape(N_GROUPS, seg, HEAD_PAD), sin_t[:ls].reshape(N_GROUPS, seg, HEAD_PAD))
    y_prompt = _trunk(x_prompt, rope_p, False, w, ssm)
    y_sample = _trunk(x_sample.reshape(N_GROUPS, seg, D_MODEL), rope_s, True, w, ssm)
    return y_prompt, y_sample.reshape(bs, ls, D_MODEL)
```

```python
import functools
import math

import jax
import jax.numpy as jnp
import numpy as np
from jax import lax
from jax.experimental import pallas as pl
from jax.experimental.pallas import tpu as pltpu

F32 = jnp.float32
BF16 = jnp.bfloat16

D_MODEL = 1024
N_HEADS = 8
QK_NOPE = 64
QK_ROPE = 32
V_DIM = 64
Q_RANK = 256
KV_RANK = 128
SSM_W = 512
SSM_GROUP = 16
SSM_GROUPS = 32
SSM_STATE = 64
D_FF = 4096
EPS = 1e-6
ROPE_THETA = 10000.0
ROPE_BLOCK = 128

LANES = 128
SUBLANES = 8
VMEM_LIMIT_CAP = 60000 * 1024

N_GROUPS = SUBLANES
HEAD_PAD = LANES
BF16_SUBLANES = 2 * SUBLANES
VT_ROWS = -(-(V_DIM + 1) // BF16_SUBLANES) * BF16_SUBLANES
N_PAIRS = N_HEADS // 2
SSM_LANE_BLOCKS = SSM_W // LANES
GROUPS_PER_BLOCK = LANES // SSM_GROUP
STATES_PER_BLOCK = GROUPS_PER_BLOCK * SSM_STATE

PRE_TT = 128
PERM_TT = 32
ATT_BQ = 512
ATT_BK = 256
ATT_UNROLL = 20
Q_SCALE = math.log2(math.e) / math.sqrt(QK_NOPE + QK_ROPE)
SSM_TT = 128
POST_TM = 512
FF_CHUNK = 1024


def _vmem_limit(nbytes):
    return int(min(VMEM_LIMIT_CAP, nbytes))


def _rms(x, g):
    return x * lax.rsqrt(jnp.mean(x * x, axis=-1, keepdims=True) + EPS) * g


def _dot(a, b):
    return jnp.dot(a, b, preferred_element_type=F32)


def _pre_kernel(x_ref, cos_ref, sin_ref, n1_ref, wa_ref, wu_ref, qg_ref, wqa_ref, wqb_ref,
                kvg_ref, wkv_ref, perm_ref, qt_ref, k_ref, vt_ref, u_ref, *, tt):
    rows = N_GROUPS * tt
    x = x_ref[...].reshape(rows, D_MODEL)
    h = _rms(x, n1_ref[...]).astype(BF16)
    pa = _dot(h, wa_ref[...])
    u = _dot(h, wu_ref[...])
    cqn = _rms(pa[:, :Q_RANK], qg_ref[...]).astype(BF16)
    ckvn = _rms(pa[:, Q_RANK:Q_RANK + KV_RANK], kvg_ref[...]).astype(BF16)
    cos = jnp.broadcast_to(cos_ref[...], (N_GROUPS, tt, HEAD_PAD)).reshape(rows, HEAD_PAD)
    sin = jnp.broadcast_to(sin_ref[...], (N_GROUPS, tt, HEAD_PAD)).reshape(rows, HEAD_PAD)
    o = Q_RANK + KV_RANK
    kr = pa[:, o:o + HEAD_PAD]
    k_rope = kr * cos + pltpu.roll(kr, HEAD_PAD - QK_ROPE, 1) * sin
    qa = _dot(cqn, wqa_ref[...])
    qb = _dot(cqn, wqb_ref[...])
    kv = _dot(ckvn, wkv_ref[...])
    nope_lanes = lax.broadcasted_iota(jnp.int32, (1, HEAD_PAD), 1) < QK_NOPE
    vt_tail = (lax.broadcasted_iota(jnp.int32, (VT_ROWS - V_DIM, tt), 0) == 0).astype(F32)
    for hd in range(N_HEADS):
        sl = slice(hd * HEAD_PAD, (hd + 1) * HEAD_PAD)
        qht = ((qa[:, sl] * cos + qb[:, sl] * sin) * Q_SCALE).T
        kvt = kv[:, sl].T
        for b in range(N_GROUPS):
            cols = slice(b * tt, (b + 1) * tt)
            qt_ref[b, hd] = qht[:, cols].astype(BF16)
            vt_ref[b, hd, 0] = jnp.concatenate([kvt[QK_NOPE:, cols], vt_tail], axis=0).astype(BF16)
        kh = jnp.where(nope_lanes, kv[:, sl], 0.0) + k_rope
        k_ref[:, hd] = kh.astype(BF16).reshape(N_GROUPS, tt, HEAD_PAD)
    u_hi = u.astype(BF16)
    u_lo = (u - u_hi.astype(F32)).astype(BF16)
    perm = perm_ref[...]
    sub = N_GROUPS * PERM_TT
    for s in range(tt // PERM_TT):
        pick = lambda a: jnp.concatenate(
            [a[b * tt + s * PERM_TT:b * tt + (s + 1) * PERM_TT] for b in range(N_GROUPS)], axis=0)
        u_ref[s * sub:(s + 1) * sub, :] = _dot(perm, pick(u_hi)) + _dot(perm, pick(u_lo))


def _pre_call(x, cos_t, sin_t, w, perm):
    nb, lr, _ = x.shape
    tt = PRE_TT
    rows = nb * tt
    n_t = lr // tt
    tab_nb = cos_t.shape[0]
    kt_sub = ATT_BK // tt
    full = lambda a: pl.BlockSpec(a.shape, lambda j: (0,) * a.ndim)
    weights = [w["n1"], w["wa"], w["wu"], w["qg"], w["wqa"], w["wqb"], w["kvg"], w["wkv"], perm]
    in_specs = [
        pl.BlockSpec((nb, tt, D_MODEL), lambda j: (0, j, 0)),
        pl.BlockSpec((tab_nb, tt, HEAD_PAD), lambda j: (0, j, 0)),
        pl.BlockSpec((tab_nb, tt, HEAD_PAD), lambda j: (0, j, 0)),
    ] + [full(a) for a in weights]
    out_shape = [
        jax.ShapeDtypeStruct((nb, N_HEADS, HEAD_PAD, lr), BF16),
        jax.ShapeDtypeStruct((nb, N_HEADS, lr, HEAD_PAD), BF16),
        jax.ShapeDtypeStruct((nb, N_HEADS, lr // ATT_BK, VT_ROWS, ATT_BK), BF16),
        jax.ShapeDtypeStruct((lr * nb, SSM_W), F32),
    ]
    out_specs = [
        pl.BlockSpec((nb, N_HEADS, HEAD_PAD, tt), lambda j: (0, 0, 0, j)),
        pl.BlockSpec((nb, N_HEADS, tt, HEAD_PAD), lambda j: (0, 0, j, 0)),
        pl.BlockSpec((nb, N_HEADS, 1, VT_ROWS, tt), lambda j: (0, 0, j // kt_sub, 0, j % kt_sub)),
        pl.BlockSpec((rows, SSM_W), lambda j: (j, 0)),
    ]
    return pl.pallas_call(
        functools.partial(_pre_kernel, tt=tt),
        grid=(n_t,),
        in_specs=in_specs,
        out_specs=out_specs,
        out_shape=out_shape,
        compiler_params=pltpu.CompilerParams(
            dimension_semantics=("arbitrary",), vmem_limit_bytes=_vmem_limit(56 << 20)),
        name="pre",
    )(x, cos_t, sin_t, *weights)


def _attn_kernel(qt_ref, k_ref, vt_ref, o_ref, s_ref, p_ref, alpha_ref, m_ref, acc_ref, *, n_seg, n_kb):
    bq, bk = ATT_BQ, ATT_BK
    m_ref[...] = jnp.full(m_ref.shape, -0.5 * float(np.finfo(np.float32).max), F32)
    acc_ref[...] = jnp.zeros(acc_ref.shape, F32)

    n = n_seg * n_kb

    def scores(c, slot):
        r0 = (c % n_kb) * bk
        if not isinstance(c, int):
            r0 = pl.multiple_of(r0, bk)
        for j in range(2):
            s_ref[slot, j, :, :bq] = _dot(k_ref[0, c // n_kb, j, pl.ds(r0, bk), :], qt_ref[0, 0, j])

    def softmax(slot):
        for j in range(2):
            for lt in range(bq // LANES):
                lanes = slice(lt * LANES, (lt + 1) * LANES)
                blk = s_ref[slot, j, :, lanes]
                m_old = m_ref[j, :, lanes]
                m_new = jnp.maximum(m_old, jnp.max(blk, axis=0, keepdims=True))
                alpha_ref[slot, j, :, lanes] = jnp.exp2(m_old - m_new)
                m_ref[j, :, lanes] = m_new
                p_ref[slot, j, :, lanes] = jnp.exp2(blk - m_new).astype(BF16)

    def weighted_values(c, slot):
        for j in range(2):
            pv = _dot(vt_ref[0, c // n_kb, j, c % n_kb], p_ref[slot, j, :, :bq])
            acc_ref[j] = acc_ref[j] * alpha_ref[slot, j] + pv

    steady = max(n - 2, 0)
    unroll = min(ATT_UNROLL, max(steady - steady % 2, 2))

    def stage(t, parity, do_scores=True, do_softmax=True, do_values=True):
        if do_values:
            weighted_values(t - 2, parity)
        if do_softmax:
            softmax(1 - parity)
        if do_scores:
            scores(t, parity)

    for t in range(2):
        stage(t, t % 2, t < n, 1 <= t <= n, False)
    n_iter = steady // unroll
    if n_iter:
        def body(k, _):
            for i in range(unroll):
                stage(2 + unroll * k + i, i % 2)
            return 0
        lax.fori_loop(0, n_iter, body, 0)
    for t in range(2 + unroll * n_iter, n + 2):
        stage(t, t % 2, t < n, t <= n, True)
    outs = [acc_ref[j][:V_DIM] / acc_ref[j][V_DIM:V_DIM + 1] for j in range(2)]
    o_ref[0] = jnp.concatenate(outs, axis=0).T


def _attn_call(qt, k, vt):
    n_seq, n_seg, _, lr, _ = k.shape
    n_kb = lr // ATT_BK
    n_qb = lr // ATT_BQ
    scratch = [
        pltpu.VMEM((2, 2, ATT_BK, ATT_BQ + LANES), F32),
        pltpu.VMEM((2, 2, ATT_BK, ATT_BQ + LANES), BF16),
        pltpu.VMEM((2, 2, 1, ATT_BQ), F32),
        pltpu.VMEM((2, 1, ATT_BQ), F32),
        pltpu.VMEM((2, VT_ROWS, ATT_BQ), F32),
    ]
    in_specs = [
        pl.BlockSpec((1, 1, 2, HEAD_PAD, ATT_BQ), lambda s, p, i: (s, i // n_qb, p, 0, i % n_qb)),
        pl.BlockSpec((1, n_seg, 2, lr, HEAD_PAD), lambda s, p, i: (s, 0, p, 0, 0)),
        pl.BlockSpec((1, n_seg, 2, n_kb, VT_ROWS, ATT_BK), lambda s, p, i: (s, 0, p, 0, 0, 0)),
    ]
    kv_bytes = 2 * n_seg * lr * (HEAD_PAD + VT_ROWS) * 2 * 2
    return pl.pallas_call(
        functools.partial(_attn_kernel, n_seg=n_seg, n_kb=n_kb),
        grid=(n_seq, N_PAIRS, n_seg * n_qb),
        in_specs=in_specs,
        out_specs=pl.BlockSpec((1, ATT_BQ, HEAD_PAD), lambda s, p, i: (s, i, p)),
        out_shape=jax.ShapeDtypeStruct((n_seq, n_seg * lr, N_HEADS * V_DIM), F32),
        scratch_shapes=scratch,
        compiler_params=pltpu.CompilerParams(
            dimension_semantics=("arbitrary", "arbitrary", "arbitrary"),
            vmem_limit_bytes=_vmem_limit(kv_bytes + (24 << 20))),
        name="attn",
    )(qt, k, vt)


def _cstep(xre, xim, are, aim, bre, bim):
    return are * xre - aim * xim + bre, are * xim + aim * xre + bim


def _ssm_kernel(uf_ref, ub_ref, bmat_ref, cmat_ref, a_ref, init_ref, *rest, tt, with_y):
    if with_y:
        yf_ref, yb_ref, fin_ref, state_ref, bu_ref, xs_ref = rest
    else:
        fin_ref, state_ref, bu_ref = rest
    j = pl.program_id(0)
    rows = N_GROUPS * tt
    n_pairs = tt // 2
    S = STATES_PER_BLOCK

    @pl.when(j == 0)
    def _():
        state_ref[...] = init_ref[...]

    for q in range(SSM_LANE_BLOCKS):
        lanes = slice(q * LANES, (q + 1) * LANES)
        slot = q % 2
        bu_ref[slot, 0] = _dot(uf_ref[:, lanes].astype(BF16), bmat_ref[0, q])
        bu_ref[slot, 1] = _dot(ub_ref[:, lanes].astype(BF16), bmat_ref[1, q])
        a = [jnp.broadcast_to(a_ref[d, q], (N_GROUPS, 2 * S)) for d in range(2)]
        are = [a[d][:, :S] for d in range(2)]
        aim = [a[d][:, S:] for d in range(2)]
        fwd = (state_ref[0, q, :, :S], state_ref[0, q, :, S:])
        bwd = (state_ref[1, q, :, :S], state_ref[1, q, :, S:])
        for k in range(n_pairs):
            r0 = k * 2 * N_GROUPS
            blk = bu_ref[slot, 0, r0:r0 + 2 * N_GROUPS, :]
            f1 = _cstep(*fwd, are[0], aim[0], blk[:N_GROUPS, :S], blk[:N_GROUPS, S:])
            fwd = _cstep(*f1, are[0], aim[0], blk[N_GROUPS:, :S], blk[N_GROUPS:, S:])
            r1 = rows - (k + 1) * 2 * N_GROUPS
            blk = bu_ref[slot, 1, r1:r1 + 2 * N_GROUPS, :]
            b1 = _cstep(*bwd, are[1], aim[1], blk[N_GROUPS:, :S], blk[N_GROUPS:, S:])
            bwd = _cstep(*b1, are[1], aim[1], blk[:N_GROUPS, :S], blk[:N_GROUPS, S:])
            if with_y:
                xs_ref[slot, 0, r0:r0 + 2 * N_GROUPS, :] = jnp.concatenate(
                    [jnp.concatenate(f1, axis=1), jnp.concatenate(fwd, axis=1)], axis=0).astype(BF16)
                xs_ref[slot, 1, r1:r1 + 2 * N_GROUPS, :] = jnp.concatenate(
                    [jnp.concatenate(bwd, axis=1), jnp.concatenate(b1, axis=1)], axis=0).astype(BF16)
        state_ref[0, q] = jnp.concatenate(fwd, axis=1)
        state_ref[1, q] = jnp.concatenate(bwd, axis=1)
        if with_y:
            yf_ref[:, lanes] = _dot(xs_ref[slot, 0], cmat_ref[0, q])
            yb_ref[:, lanes] = _dot(xs_ref[slot, 1], cmat_ref[1, q])

    @pl.when(j == pl.num_programs(0) - 1)
    def _():
        fin_ref[...] = state_ref[...]


def _ssm_call(u, bmat, cmat, a, init, with_y):
    n_rows = u.shape[0]
    tt = SSM_TT
    rows = N_GROUPS * tt
    n_t = n_rows // rows
    full = lambda arr: pl.BlockSpec(arr.shape, lambda j: (0,) * arr.ndim)
    in_specs = [
        pl.BlockSpec((rows, SSM_W), lambda j: (j, 0)),
        pl.BlockSpec((rows, SSM_W), lambda j: (n_t - 1 - j, 0)),
        full(bmat), full(cmat), full(a), full(init),
    ]
    state_shape = (2, SSM_LANE_BLOCKS, N_GROUPS, 2 * STATES_PER_BLOCK)
    out_shape = [jax.ShapeDtypeStruct(state_shape, F32)]
    out_specs = [pl.BlockSpec(state_shape, lambda j: (0, 0, 0, 0))]
    scratch = [pltpu.VMEM(state_shape, F32), pltpu.VMEM((2, 2, rows, 2 * STATES_PER_BLOCK), F32)]
    if with_y:
        out_shape = [jax.ShapeDtypeStruct((n_rows, SSM_W), F32)] * 2 + out_shape
        out_specs = [pl.BlockSpec((rows, SSM_W), lambda j: (j, 0)),
                     pl.BlockSpec((rows, SSM_W), lambda j: (n_t - 1 - j, 0))] + out_specs
        scratch.append(pltpu.VMEM((2, 2, rows, 2 * STATES_PER_BLOCK), BF16))
    return pl.pallas_call(
        functools.partial(_ssm_kernel, tt=tt, with_y=with_y),
        grid=(n_t,),
        in_specs=in_specs,
        out_specs=out_specs,
        out_shape=out_shape,
        scratch_shapes=scratch,
        compiler_params=pltpu.CompilerParams(
            dimension_semantics=("arbitrary",), vmem_limit_bytes=_vmem_limit(57 << 20)),
        name="ssm" if with_y else "ssm_ends",
    )(u, u, bmat, cmat, a, init)


def _carry_kernel(e_ref, as_ref, i_ref):
    S = STATES_PER_BLOCK
    row = lax.broadcasted_iota(jnp.int32, (N_GROUPS, S), 0)
    for d in range(2):
        first = 0 if d == 0 else N_GROUPS - 1
        shift = 1 if d == 0 else N_GROUPS - 1
        for q in range(SSM_LANE_BLOCKS):
            e = e_ref[d, q]
            a = jnp.broadcast_to(as_ref[d, q], (N_GROUPS, 2 * S))
            ere, eim, are, aim = e[:, :S], e[:, S:], a[:, :S], a[:, S:]
            ire = jnp.zeros((N_GROUPS, S), F32)
            iim = jnp.zeros((N_GROUPS, S), F32)
            for _ in range(N_GROUPS - 1):
                tre, tim = _cstep(ire, iim, are, aim, ere, eim)
                ire = jnp.where(row == first, 0.0, pltpu.roll(tre, shift, 0))
                iim = jnp.where(row == first, 0.0, pltpu.roll(tim, shift, 0))
            i_ref[d, q] = jnp.concatenate([ire, iim], axis=1)


def _carry_call(ends, a_seg):
    return pl.pallas_call(
        _carry_kernel,
        out_shape=jax.ShapeDtypeStruct(ends.shape, F32),
        name="ssm_carry",
    )(ends, a_seg)


def _ssm_post_kernel(yf_ref, yb_ref, u_ref, dsk_ref, wglu_ref, bglu_ref, sg_ref, permt_ref, o_ref, *, tt):
    y = yf_ref[...] + yb_ref[...] + dsk_ref[...] * u_ref[...]
    g = jax.nn.gelu(y)
    z = _dot(g.astype(BF16), wglu_ref[...]) + bglu_ref[...]
    s = g * jax.nn.sigmoid(z)
    sn = _rms(s, sg_ref[...]).astype(BF16)
    permt = permt_ref[...]
    sub = N_GROUPS * PERM_TT
    for k in range(tt // PERM_TT):
        blk = _dot(permt, sn[k * sub:(k + 1) * sub]).astype(BF16)
        o_ref[:, k * PERM_TT:(k + 1) * PERM_TT, :] = blk.reshape(N_GROUPS, PERM_TT, SSM_W)


def _ssm_post_call(yf, yb, u, w, permt):
    n_rows = u.shape[0]
    tt = PRE_TT
    rows = N_GROUPS * tt
    n_t = n_rows // rows
    lr = n_rows // N_GROUPS
    full = lambda a: pl.BlockSpec(a.shape, lambda j: (0,) * a.ndim)
    row_spec = pl.BlockSpec((rows, SSM_W), lambda j: (j, 0))
    weights = [w["dsk"], w["wglu"], w["bglu"], w["sg"], permt]
    return pl.pallas_call(
        functools.partial(_ssm_post_kernel, tt=tt),
        grid=(n_t,),
        in_specs=[row_spec, row_spec, row_spec] + [full(a) for a in weights],
        out_specs=pl.BlockSpec((N_GROUPS, tt, SSM_W), lambda j: (0, j, 0)),
        out_shape=jax.ShapeDtypeStruct((N_GROUPS, lr, SSM_W), BF16),
        compiler_params=pltpu.CompilerParams(
            dimension_semantics=("arbitrary",), vmem_limit_bytes=_vmem_limit(40 << 20)),
        name="ssm_post",
    )(yf, yb, u, *weights)


def _post_kernel(x_ref, a_ref, sn_ref, ag_ref, wo_ref, n2_ref, w1_ref, w2_ref, fg_ref, o_ref):
    an = _rms(a_ref[0], ag_ref[...]).astype(BF16)
    mixed = jnp.concatenate([an, sn_ref[0]], axis=-1)
    x1 = x_ref[0] + _dot(mixed, wo_ref[...])
    h2 = _rms(x1, n2_ref[...]).astype(BF16)
    acc = jnp.zeros_like(x1)
    for c in range(D_FF // FF_CHUNK):
        hid = _dot(h2, w1_ref[:, c * FF_CHUNK:(c + 1) * FF_CHUNK])
        hid = jnp.square(jnp.maximum(hid, 0.0)).astype(BF16)
        acc = acc + _dot(hid, w2_ref[c * FF_CHUNK:(c + 1) * FF_CHUNK, :])
    o_ref[0] = _rms(x1 + acc, fg_ref[...])


def _post_call(x, a, sn, w):
    nb, lr, _ = x.shape
    tm = POST_TM
    const = lambda arr: pl.BlockSpec(arr.shape, lambda b, i: (0,) * arr.ndim, pipeline_mode=pl.Buffered(1))
    weights = [w["ag"], w["wo"], w["n2"], w["w1"], w["w2"], w["fg"]]
    return pl.pallas_call(
        _post_kernel,
        grid=(nb, lr // tm),
        in_specs=[
            pl.BlockSpec((1, tm, D_MODEL), lambda b, i: (b, i, 0)),
            pl.BlockSpec((1, tm, N_HEADS * V_DIM), lambda b, i: (b, i, 0)),
            pl.BlockSpec((1, tm, SSM_W), lambda b, i: (b, i, 0)),
        ] + [const(arr) for arr in weights],
        out_specs=pl.BlockSpec((1, tm, D_MODEL), lambda b, i: (b, i, 0)),
        out_shape=jax.ShapeDtypeStruct((nb, lr, D_MODEL), F32),
        compiler_params=pltpu.CompilerParams(
            dimension_semantics=("arbitrary", "arbitrary"), vmem_limit_bytes=_vmem_limit(48 << 20)),
        name="post",
    )(x, a, sn, *weights)


def _rope_tables(length):
    inv = ROPE_THETA ** (-jnp.arange(0, QK_ROPE, 2, dtype=F32) / QK_ROPE)
    zeros = lambda n: jnp.zeros((n,), F32)
    inv_lanes = jnp.concatenate([zeros(QK_NOPE), inv, inv, zeros(HEAD_PAD - QK_NOPE - QK_ROPE)])
    n_hi = -(-length // ROPE_BLOCK)
    a_hi = (jnp.arange(n_hi, dtype=F32) * ROPE_BLOCK)[:, None] * inv_lanes
    a_lo = jnp.arange(ROPE_BLOCK, dtype=F32)[:, None] * inv_lanes
    ch, sh = jnp.cos(a_hi)[:, None], jnp.sin(a_hi)[:, None]
    cl, sl = jnp.cos(a_lo)[None], jnp.sin(a_lo)[None]
    used = (jnp.arange(HEAD_PAD) < QK_NOPE + QK_ROPE).astype(F32)
    cos_t = ((ch * cl - sh * sl) * used).reshape(n_hi * ROPE_BLOCK, HEAD_PAD)[:length]
    sin_t = (sh * cl + ch * sl).reshape(n_hi * ROPE_BLOCK, HEAD_PAD)[:length]
    return cos_t, sin_t


def _rot_half_cols(w):
    half = QK_ROPE // 2
    return jnp.concatenate([-w[..., half:], w[..., :half]], axis=-1)


def _pack_weights(norm1_g, w_in, q_norm_g, w_uq, kv_norm_g, w_ukv, d_skip, w_glu, b_glu,
                  attn_out_g, ssm_out_g, w_out, norm2_g, w_mlp1, w_mlp2, final_g):
    row = lambda g: g.reshape(1, -1).astype(F32)
    o = Q_RANK + KV_RANK
    w_kr = w_in[:, o:o + QK_ROPE]
    zk = jnp.zeros((D_MODEL, QK_NOPE), F32)
    wa = jnp.concatenate([w_in[:, :o], zk, w_kr, _rot_half_cols(w_kr)], axis=1)
    wu = w_in[:, o + QK_ROPE:]
    wq = w_uq.reshape(Q_RANK, N_HEADS, QK_NOPE + QK_ROPE)
    zq = jnp.zeros((Q_RANK, N_HEADS, HEAD_PAD - QK_NOPE - QK_ROPE), F32)
    wqa = jnp.concatenate([wq, zq], axis=-1).reshape(Q_RANK, N_HEADS * HEAD_PAD)
    wqb = jnp.concatenate([jnp.zeros((Q_RANK, N_HEADS, QK_NOPE), F32), _rot_half_cols(wq[..., QK_NOPE:]), zq],
                          axis=-1).reshape(Q_RANK, N_HEADS * HEAD_PAD)
    assert QK_NOPE + V_DIM == HEAD_PAD
    bf = lambda a: a.astype(BF16)
    return dict(
        n1=row(norm1_g), wa=bf(wa), wu=bf(wu), qg=row(q_norm_g), wqa=bf(wqa), wqb=bf(wqb),
        kvg=row(kv_norm_g), wkv=bf(w_ukv),
        dsk=row(d_skip), wglu=bf(w_glu), bglu=row(b_glu), sg=row(ssm_out_g),
        ag=row(attn_out_g), wo=bf(w_out), n2=row(norm2_g), w1=bf(w_mlp1), w2=bf(w_mlp2), fg=row(final_g))


def _pack_ssm(lam_re, lam_im, log_dt, b_re, b_im, c_re, c_im, seg_len):
    cmul = lambda xr, xi, yr, yi: (xr * yr - xi * yi, xr * yi + xi * yr)
    lam_re, lam_im = lam_re.astype(F32), lam_im.astype(F32)
    dt = jnp.exp(log_dt.astype(F32))[..., None]
    mag = jnp.exp(lam_re * dt)
    a_re, a_im = mag * jnp.cos(lam_im * dt), mag * jnp.sin(lam_im * dt)
    den = lam_re * lam_re + lam_im * lam_im
    k_re = ((a_re - 1.0) * lam_re + a_im * lam_im) / den
    k_im = (a_im * lam_re - (a_re - 1.0) * lam_im) / den
    bb_re, bb_im = cmul(k_re[..., None], k_im[..., None], b_re.astype(F32), b_im.astype(F32))
    p_re, p_im = a_re, a_im
    s_re, s_im = jnp.ones_like(a_re), jnp.zeros_like(a_re)
    n = seg_len
    while n:
        if n & 1:
            s_re, s_im = cmul(s_re, s_im, p_re, p_im)
        p_re, p_im = cmul(p_re, p_im, p_re, p_im)
        n >>= 1
    eye = jnp.eye(GROUPS_PER_BLOCK, dtype=F32)
    nq, gb = SSM_LANE_BLOCKS, GROUPS_PER_BLOCK

    def b_block(part):
        p = part.reshape(2, nq, gb, SSM_STATE, SSM_GROUP)
        return jnp.einsum("dqgnh,gk->dqghkn", p, eye).reshape(2, nq, LANES, STATES_PER_BLOCK)

    def c_block(part):
        p = part.reshape(2, nq, gb, SSM_GROUP, SSM_STATE)
        return jnp.einsum("dqghn,gk->dqgnkh", p, eye).reshape(2, nq, STATES_PER_BLOCK, LANES)

    def a_block(zr, zi):
        shape = (2, nq, 1, STATES_PER_BLOCK)
        return jnp.concatenate([zr.reshape(shape), zi.reshape(shape)], axis=-1)

    bmat = jnp.concatenate([b_block(bb_re), b_block(bb_im)], axis=-1).astype(BF16)
    cmat = jnp.concatenate([c_block(c_re.astype(F32)), -c_block(c_im.astype(F32))], axis=-2).astype(BF16)
    return bmat, cmat, a_block(a_re, a_im), a_block(s_re, s_im)


def _perm_matrix(tt):
    rows = N_GROUPS * tt
    dst = np.arange(rows)
    src = (dst % N_GROUPS) * tt + dst // N_GROUPS
    p = np.zeros((rows, rows), np.float32)
    p[dst, src] = 1.0
    return jnp.asarray(p, BF16)


def _trunk(x, rope, segmented, w, ssm):
    nb, lr, _ = x.shape
    bmat, cmat, a_blk, a_seg = ssm
    cos_t, sin_t = rope
    perm = _perm_matrix(PERM_TT)
    qt, k, vt, u = _pre_call(x, cos_t, sin_t, w, perm)
    seq_seg = (lambda a: a[None]) if segmented else (lambda a: a[:, None])
    att = _attn_call(seq_seg(qt), seq_seg(k), seq_seg(vt)).reshape(nb, lr, N_HEADS * V_DIM)
    init = jnp.zeros((2, SSM_LANE_BLOCKS, N_GROUPS, 2 * STATES_PER_BLOCK), F32)
    if segmented:
        (ends,) = _ssm_call(u, bmat, cmat, a_blk, init, with_y=False)
        init = _carry_call(ends, a_seg)
    yf, yb, _ = _ssm_call(u, bmat, cmat, a_blk, init, with_y=True)
    sn = _ssm_post_call(yf, yb, u, w, perm.T)
    return _post_call(x, att, sn, w)


def kernel(x_prompt, x_sample, norm1_g, w_in, q_norm_g, w_uq, kv_norm_g, w_ukv, lam_re, lam_im, log_dt,
           b_re, b_im, c_re, c_im, d_skip, w_glu, b_glu, attn_out_g, ssm_out_g, w_out, norm2_g, w_mlp1,
           w_mlp2, final_g):
    assert norm1_g.shape[0] == 1, "single-layer trunk"
    w = _pack_weights(norm1_g[0], w_in[0], q_norm_g[0], w_uq[0], kv_norm_g[0], w_ukv[0], d_skip[0], w_glu[0],
                      b_glu[0], attn_out_g[0], ssm_out_g[0], w_out[0], norm2_g[0], w_mlp1[0], w_mlp2[0], final_g)
    bp, lp, _ = x_prompt.shape
    bs, ls, _ = x_sample.shape
    assert bp == N_GROUPS and bs == 1 and ls % N_GROUPS == 0
    seg = ls // N_GROUPS
    ssm = _pack_ssm(lam_re[0], lam_im[0], log_dt[0], b_re[0], b_im[0], c_re[0], c_im[0], seg)
    cos_t, sin_t = _rope_tables(max(lp, ls))
    rope_p = (cos_t[None, :lp], sin_t[None, :lp])
    rope_s = (cos_t[:ls].reshape(N_GROUPS, seg, HEAD_PAD), sin_t[:ls].reshape(N_GROUPS, seg, HEAD_PAD))
    y_prompt = _trunk(x_prompt, rope_p, False, w, ssm)
    y_sample = _trunk(x_sample.reshape(N_GROUPS, seg, D_MODEL), rope_s, True, w, ssm)
    return y_prompt, y_sample.reshape(bs, ls, D_MODEL)
```

```python
import functools
import math

import jax
import jax.numpy as jnp
import numpy as np
from jax import lax
from jax.experimental import pallas as pl
from jax.experimental.pallas import tpu as pltpu

F32 = jnp.float32
BF16 = jnp.bfloat16

D_MODEL = 1024
N_HEADS = 8
QK_NOPE = 64
QK_ROPE = 32
V_DIM = 64
Q_RANK = 256
KV_RANK = 128
SSM_W = 512
SSM_GROUP = 16
SSM_GROUPS = 32
SSM_STATE = 64
D_FF = 4096
EPS = 1e-6
ROPE_THETA = 10000.0
ROPE_BLOCK = 128

LANES = 128
SUBLANES = 8
VMEM_LIMIT_CAP = 60000 * 1024

N_GROUPS = SUBLANES
HEAD_PAD = LANES
BF16_SUBLANES = 2 * SUBLANES
VT_ROWS = -(-(V_DIM + 1) // BF16_SUBLANES) * BF16_SUBLANES
N_PAIRS = N_HEADS // 2
SSM_LANE_BLOCKS = SSM_W // LANES
GROUPS_PER_BLOCK = LANES // SSM_GROUP
STATES_PER_BLOCK = GROUPS_PER_BLOCK * SSM_STATE

PRE_TT = 128
PERM_TT = 32
ATT_BQ = 512
ATT_BK = 256
ATT_UNROLL = 30
Q_SCALE = math.log2(math.e) / math.sqrt(QK_NOPE + QK_ROPE)
SSM_TT = 128
POST_TM = 512
FF_CHUNK = 1024


def _vmem_limit(nbytes):
    return int(min(VMEM_LIMIT_CAP, nbytes))


def _rms(x, g):
    return x * lax.rsqrt(jnp.mean(x * x, axis=-1, keepdims=True) + EPS) * g


def _dot(a, b):
    return jnp.dot(a, b, preferred_element_type=F32)


def _pre_kernel(x_ref, cos_ref, sin_ref, n1_ref, wa_ref, wu_ref, qg_ref, wqa_ref, wqb_ref,
                kvg_ref, wkv_ref, perm_ref, qt_ref, k_ref, vt_ref, u_ref, *, tt):
    rows = N_GROUPS * tt
    x = x_ref[...].reshape(rows, D_MODEL)
    h = _rms(x, n1_ref[...]).astype(BF16)
    pa = _dot(h, wa_ref[...])
    u = _dot(h, wu_ref[...])
    cqn = _rms(pa[:, :Q_RANK], qg_ref[...]).astype(BF16)
    ckvn = _rms(pa[:, Q_RANK:Q_RANK + KV_RANK], kvg_ref[...]).astype(BF16)
    cos = jnp.broadcast_to(cos_ref[...], (N_GROUPS, tt, HEAD_PAD)).reshape(rows, HEAD_PAD)
    sin = jnp.broadcast_to(sin_ref[...], (N_GROUPS, tt, HEAD_PAD)).reshape(rows, HEAD_PAD)
    o = Q_RANK + KV_RANK
    kr = pa[:, o:o + HEAD_PAD]
    k_rope = kr * cos + pltpu.roll(kr, HEAD_PAD - QK_ROPE, 1) * sin
    qa = _dot(cqn, wqa_ref[...])
    qb = _dot(cqn, wqb_ref[...])
    kv = _dot(ckvn, wkv_ref[...])
    nope_lanes = lax.broadcasted_iota(jnp.int32, (1, HEAD_PAD), 1) < QK_NOPE
    vt_tail = (lax.broadcasted_iota(jnp.int32, (VT_ROWS - V_DIM, tt), 0) == 0).astype(F32)
    for hd in range(N_HEADS):
        sl = slice(hd * HEAD_PAD, (hd + 1) * HEAD_PAD)
        qht = ((qa[:, sl] * cos + qb[:, sl] * sin) * Q_SCALE).T
        kvt = kv[:, sl].T
        for b in range(N_GROUPS):
            cols = slice(b * tt, (b + 1) * tt)
            qt_ref[b, hd] = qht[:, cols].astype(BF16)
            vt_ref[b, hd, 0] = jnp.concatenate([kvt[QK_NOPE:, cols], vt_tail], axis=0).astype(BF16)
        kh = jnp.where(nope_lanes, kv[:, sl], 0.0) + k_rope
        k_ref[:, hd] = kh.astype(BF16).reshape(N_GROUPS, tt, HEAD_PAD)
    u_hi = u.astype(BF16)
    u_lo = (u - u_hi.astype(F32)).astype(BF16)
    perm = perm_ref[...]
    sub = N_GROUPS * PERM_TT
    for s in range(tt // PERM_TT):
        pick = lambda a: jnp.concatenate(
            [a[b * tt + s * PERM_TT:b * tt + (s + 1) * PERM_TT] for b in range(N_GROUPS)], axis=0)
        u_ref[s * sub:(s + 1) * sub, :] = _dot(perm, pick(u_hi)) + _dot(perm, pick(u_lo))


def _pre_call(x, cos_t, sin_t, w, perm):
    nb, lr, _ = x.shape
    tt = PRE_TT
    rows = nb * tt
    n_t = lr // tt
    tab_nb = cos_t.shape[0]
    kt_sub = ATT_BK // tt
    full = lambda a: pl.BlockSpec(a.shape, lambda j: (0,) * a.ndim)
    weights = [w["n1"], w["wa"], w["wu"], w["qg"], w["wqa"], w["wqb"], w["kvg"], w["wkv"], perm]
    in_specs = [
        pl.BlockSpec((nb, tt, D_MODEL), lambda j: (0, j, 0)),
        pl.BlockSpec((tab_nb, tt, HEAD_PAD), lambda j: (0, j, 0)),
        pl.BlockSpec((tab_nb, tt, HEAD_PAD), lambda j: (0, j, 0)),
    ] + [full(a) for a in weights]
    out_shape = [
        jax.ShapeDtypeStruct((nb, N_HEADS, HEAD_PAD, lr), BF16),
        jax.ShapeDtypeStruct((nb, N_HEADS, lr, HEAD_PAD), BF16),
        jax.ShapeDtypeStruct((nb, N_HEADS, lr // ATT_BK, VT_ROWS, ATT_BK), BF16),
        jax.ShapeDtypeStruct((lr * nb, SSM_W), F32),
    ]
    out_specs = [
        pl.BlockSpec((nb, N_HEADS, HEAD_PAD, tt), lambda j: (0, 0, 0, j)),
        pl.BlockSpec((nb, N_HEADS, tt, HEAD_PAD), lambda j: (0, 0, j, 0)),
        pl.BlockSpec((nb, N_HEADS, 1, VT_ROWS, tt), lambda j: (0, 0, j // kt_sub, 0, j % kt_sub)),
        pl.BlockSpec((rows, SSM_W), lambda j: (j, 0)),
    ]
    return pl.pallas_call(
        functools.partial(_pre_kernel, tt=tt),
        grid=(n_t,),
        in_specs=in_specs,
        out_specs=out_specs,
        out_shape=out_shape,
        compiler_params=pltpu.CompilerParams(
            dimension_semantics=("arbitrary",), vmem_limit_bytes=_vmem_limit(56 << 20)),
        name="pre",
    )(x, cos_t, sin_t, *weights)


def _attn_kernel(qt_ref, k_ref, vt_ref, o_ref, s_ref, p_ref, alpha_ref, m_ref, acc_ref, *, n_seg, n_kb):
    bq, bk = ATT_BQ, ATT_BK
    m_ref[...] = jnp.full(m_ref.shape, -0.5 * float(np.finfo(np.float32).max), F32)
    acc_ref[...] = jnp.zeros(acc_ref.shape, F32)

    n = n_seg * n_kb

    def scores(c, slot):
        r0 = (c % n_kb) * bk
        if not isinstance(c, int):
            r0 = pl.multiple_of(r0, bk)
        for j in range(2):
            s_ref[slot, j, :, :bq] = _dot(k_ref[0, c // n_kb, j, pl.ds(r0, bk), :], qt_ref[0, 0, j])

    def softmax(slot):
        for j in range(2):
            for lt in range(bq // LANES):
                lanes = slice(lt * LANES, (lt + 1) * LANES)
                blk = s_ref[slot, j, :, lanes]
                m_old = m_ref[j, :, lanes]
                m_new = jnp.maximum(m_old, jnp.max(blk, axis=0, keepdims=True))
                alpha_ref[slot, j, :, lanes] = jnp.exp2(m_old - m_new)
                m_ref[j, :, lanes] = m_new
                p_ref[slot, j, :, lanes] = jnp.exp2(blk - m_new).astype(BF16)

    def weighted_values(c, slot):
        for j in range(2):
            pv = _dot(vt_ref[0, c // n_kb, j, c % n_kb], p_ref[slot, j, :, :bq])
            acc_ref[j] = acc_ref[j] * alpha_ref[slot, j] + pv

    steady = max(n - 2, 0)
    unroll = min(ATT_UNROLL, max(steady - steady % 2, 2))

    def stage(t, parity, do_scores=True, do_softmax=True, do_values=True):
        if do_values:
            weighted_values(t - 2, parity)
        if do_softmax:
            softmax(1 - parity)
        if do_scores:
            scores(t, parity)

    for t in range(2):
        stage(t, t % 2, t < n, 1 <= t <= n, False)
    n_iter = steady // unroll
    if n_iter:
        def body(k, _):
            for i in range(unroll):
                stage(2 + unroll * k + i, i % 2)
            return 0
        lax.fori_loop(0, n_iter, body, 0)
    for t in range(2 + unroll * n_iter, n + 2):
        stage(t, t % 2, t < n, t <= n, True)
    outs = [acc_ref[j][:V_DIM] / acc_ref[j][V_DIM:V_DIM + 1] for j in range(2)]
    o_ref[0] = jnp.concatenate(outs, axis=0).T


def _attn_call(qt, k, vt):
    n_seq, n_seg, _, lr, _ = k.shape
    n_kb = lr // ATT_BK
    n_qb = lr // ATT_BQ
    scratch = [
        pltpu.VMEM((2, 2, ATT_BK, ATT_BQ + LANES), F32),
        pltpu.VMEM((2, 2, ATT_BK, ATT_BQ + LANES), BF16),
        pltpu.VMEM((2, 2, 1, ATT_BQ), F32),
        pltpu.VMEM((2, 1, ATT_BQ), F32),
        pltpu.VMEM((2, VT_ROWS, ATT_BQ), F32),
    ]
    in_specs = [
        pl.BlockSpec((1, 1, 2, HEAD_PAD, ATT_BQ), lambda s, p, i: (s, i // n_qb, p, 0, i % n_qb)),
        pl.BlockSpec((1, n_seg, 2, lr, HEAD_PAD), lambda s, p, i: (s, 0, p, 0, 0)),
        pl.BlockSpec((1, n_seg, 2, n_kb, VT_ROWS, ATT_BK), lambda s, p, i: (s, 0, p, 0, 0, 0)),
    ]
    kv_bytes = 2 * n_seg * lr * (HEAD_PAD + VT_ROWS) * 2 * 2
    return pl.pallas_call(
        functools.partial(_attn_kernel, n_seg=n_seg, n_kb=n_kb),
        grid=(n_seq, N_PAIRS, n_seg * n_qb),
        in_specs=in_specs,
        out_specs=pl.BlockSpec((1, ATT_BQ, HEAD_PAD), lambda s, p, i: (s, i, p)),
        out_shape=jax.ShapeDtypeStruct((n_seq, n_seg * lr, N_HEADS * V_DIM), F32),
        scratch_shapes=scratch,
        compiler_params=pltpu.CompilerParams(
            dimension_semantics=("arbitrary", "arbitrary", "arbitrary"),
            vmem_limit_bytes=_vmem_limit(kv_bytes + (24 << 20))),
        name="attn",
    )(qt, k, vt)


def _cstep(xre, xim, are, aim, bre, bim):
    return are * xre - aim * xim + bre, are * xim + aim * xre + bim


def _ssm_kernel(uf_ref, ub_ref, bmat_ref, cmat_ref, a_ref, init_ref, *rest, tt, with_y):
    if with_y:
        yf_ref, yb_ref, fin_ref, state_ref, bu_ref, xs_ref = rest
    else:
        fin_ref, state_ref, bu_ref = rest
    j = pl.program_id(0)
    rows = N_GROUPS * tt
    n_pairs = tt // 2
    S = STATES_PER_BLOCK

    @pl.when(j == 0)
    def _():
        state_ref[...] = init_ref[...]

    for q in range(SSM_LANE_BLOCKS):
        lanes = slice(q * LANES, (q + 1) * LANES)
        slot = q % 2
        bu_ref[slot, 0] = _dot(uf_ref[:, lanes].astype(BF16), bmat_ref[0, q])
        bu_ref[slot, 1] = _dot(ub_ref[:, lanes].astype(BF16), bmat_ref[1, q])
        a = [jnp.broadcast_to(a_ref[d, q], (N_GROUPS, 2 * S)) for d in range(2)]
        are = [a[d][:, :S] for d in range(2)]
        aim = [a[d][:, S:] for d in range(2)]
        fwd = (state_ref[0, q, :, :S], state_ref[0, q, :, S:])
        bwd = (state_ref[1, q, :, :S], state_ref[1, q, :, S:])
        for k in range(n_pairs):
            r0 = k * 2 * N_GROUPS
            blk = bu_ref[slot, 0, r0:r0 + 2 * N_GROUPS, :]
            f1 = _cstep(*fwd, are[0], aim[0], blk[:N_GROUPS, :S], blk[:N_GROUPS, S:])
            fwd = _cstep(*f1, are[0], aim[0], blk[N_GROUPS:, :S], blk[N_GROUPS:, S:])
            r1 = rows - (k + 1) * 2 * N_GROUPS
            blk = bu_ref[slot, 1, r1:r1 + 2 * N_GROUPS, :]
            b1 = _cstep(*bwd, are[1], aim[1], blk[N_GROUPS:, :S], blk[N_GROUPS:, S:])
            bwd = _cstep(*b1, are[1], aim[1], blk[:N_GROUPS, :S], blk[:N_GROUPS, S:])
            if with_y:
                xs_ref[slot, 0, r0:r0 + 2 * N_GROUPS, :] = jnp.concatenate(
                    [jnp.concatenate(f1, axis=1), jnp.concatenate(fwd, axis=1)], axis=0).astype(BF16)
                xs_ref[slot, 1, r1:r1 + 2 * N_GROUPS, :] = jnp.concatenate(
                    [jnp.concatenate(bwd, axis=1), jnp.concatenate(b1, axis=1)], axis=0).astype(BF16)
        state_ref[0, q] = jnp.concatenate(fwd, axis=1)
        state_ref[1, q] = jnp.concatenate(bwd, axis=1)
        if with_y:
            yf_ref[:, lanes] = _dot(xs_ref[slot, 0], cmat_ref[0, q])
            yb_ref[:, lanes] = _dot(xs_ref[slot, 1], cmat_ref[1, q])

    @pl.when(j == pl.num_programs(0) - 1)
    def _():
        fin_ref[...] = state_ref[...]


def _ssm_call(u, bmat, cmat, a, init, with_y):
    n_rows = u.shape[0]
    tt = SSM_TT
    rows = N_GROUPS * tt
    n_t = n_rows // rows
    full = lambda arr: pl.BlockSpec(arr.shape, lambda j: (0,) * arr.ndim)
    in_specs = [
        pl.BlockSpec((rows, SSM_W), lambda j: (j, 0)),
        pl.BlockSpec((rows, SSM_W), lambda j: (n_t - 1 - j, 0)),
        full(bmat), full(cmat), full(a), full(init),
    ]
    state_shape = (2, SSM_LANE_BLOCKS, N_GROUPS, 2 * STATES_PER_BLOCK)
    out_shape = [jax.ShapeDtypeStruct(state_shape, F32)]
    out_specs = [pl.BlockSpec(state_shape, lambda j: (0, 0, 0, 0))]
    scratch = [pltpu.VMEM(state_shape, F32), pltpu.VMEM((2, 2, rows, 2 * STATES_PER_BLOCK), F32)]
    if with_y:
        out_shape = [jax.ShapeDtypeStruct((n_rows, SSM_W), F32)] * 2 + out_shape
        out_specs = [pl.BlockSpec((rows, SSM_W), lambda j: (j, 0)),
                     pl.BlockSpec((rows, SSM_W), lambda j: (n_t - 1 - j, 0))] + out_specs
        scratch.append(pltpu.VMEM((2, 2, rows, 2 * STATES_PER_BLOCK), BF16))
    return pl.pallas_call(
        functools.partial(_ssm_kernel, tt=tt, with_y=with_y),
        grid=(n_t,),
        in_specs=in_specs,
        out_specs=out_specs,
        out_shape=out_shape,
        scratch_shapes=scratch,
        compiler_params=pltpu.CompilerParams(
            dimension_semantics=("arbitrary",), vmem_limit_bytes=_vmem_limit(57 << 20)),
        name="ssm" if with_y else "ssm_ends",
    )(u, u, bmat, cmat, a, init)


def _carry_kernel(e_ref, as_ref, i_ref):
    S = STATES_PER_BLOCK
    row = lax.broadcasted_iota(jnp.int32, (N_GROUPS, S), 0)
    for d in range(2):
        first = 0 if d == 0 else N_GROUPS - 1
        shift = 1 if d == 0 else N_GROUPS - 1
        for q in range(SSM_LANE_BLOCKS):
            e = e_ref[d, q]
            a = jnp.broadcast_to(as_ref[d, q], (N_GROUPS, 2 * S))
            ere, eim, are, aim = e[:, :S], e[:, S:], a[:, :S], a[:, S:]
            ire = jnp.zeros((N_GROUPS, S), F32)
            iim = jnp.zeros((N_GROUPS, S), F32)
            for _ in range(N_GROUPS - 1):
                tre, tim = _cstep(ire, iim, are, aim, ere, eim)
                ire = jnp.where(row == first, 0.0, pltpu.roll(tre, shift, 0))
                iim = jnp.where(row == first, 0.0, pltpu.roll(tim, shift, 0))
            i_ref[d, q] = jnp.concatenate([ire, iim], axis=1)


def _carry_call(ends, a_seg):
    return pl.pallas_call(
        _carry_kernel,
        out_shape=jax.ShapeDtypeStruct(ends.shape, F32),
        name="ssm_carry",
    )(ends, a_seg)


def _ssm_post_kernel(yf_ref, yb_ref, u_ref, dsk_ref, wglu_ref, bglu_ref, sg_ref, permt_ref, o_ref, *, tt):
    y = yf_ref[...] + yb_ref[...] + dsk_ref[...] * u_ref[...]
    g = jax.nn.gelu(y)
    z = _dot(g.astype(BF16), wglu_ref[...]) + bglu_ref[...]
    s = g * jax.nn.sigmoid(z)
    sn = _rms(s, sg_ref[...]).astype(BF16)
    permt = permt_ref[...]
    sub = N_GROUPS * PERM_TT
    for k in range(tt // PERM_TT):
        blk = _dot(permt, sn[k * sub:(k + 1) * sub]).astype(BF16)
        o_ref[:, k * PERM_TT:(k + 1) * PERM_TT, :] = blk.reshape(N_GROUPS, PERM_TT, SSM_W)


def _ssm_post_call(yf, yb, u, w, permt):
    n_rows = u.shape[0]
    tt = PRE_TT
    rows = N_GROUPS * tt
    n_t = n_rows // rows
    lr = n_rows // N_GROUPS
    full = lambda a: pl.BlockSpec(a.shape, lambda j: (0,) * a.ndim)
    row_spec = pl.BlockSpec((rows, SSM_W), lambda j: (j, 0))
    weights = [w["dsk"], w["wglu"], w["bglu"], w["sg"], permt]
    return pl.pallas_call(
        functools.partial(_ssm_post_kernel, tt=tt),
        grid=(n_t,),
        in_specs=[row_spec, row_spec, row_spec] + [full(a) for a in weights],
        out_specs=pl.BlockSpec((N_GROUPS, tt, SSM_W), lambda j: (0, j, 0)),
        out_shape=jax.ShapeDtypeStruct((N_GROUPS, lr, SSM_W), BF16),
        compiler_params=pltpu.CompilerParams(
            dimension_semantics=("arbitrary",), vmem_limit_bytes=_vmem_limit(40 << 20)),
        name="ssm_post",
    )(yf, yb, u, *weights)


def _post_kernel(x_ref, a_ref, sn_ref, ag_ref, wo_ref, n2_ref, w1_ref, w2_ref, fg_ref, o_ref):
    an = _rms(a_ref[0], ag_ref[...]).astype(BF16)
    mixed = jnp.concatenate([an, sn_ref[0]], axis=-1)
    x1 = x_ref[0] + _dot(mixed, wo_ref[...])
    h2 = _rms(x1, n2_ref[...]).astype(BF16)
    acc = jnp.zeros_like(x1)
    for c in range(D_FF // FF_CHUNK):
        hid = _dot(h2, w1_ref[:, c * FF_CHUNK:(c + 1) * FF_CHUNK])
        hid = jnp.square(jnp.maximum(hid, 0.0)).astype(BF16)
        acc = acc + _dot(hid, w2_ref[c * FF_CHUNK:(c + 1) * FF_CHUNK, :])
    o_ref[0] = _rms(x1 + acc, fg_ref[...])


def _post_call(x, a, sn, w):
    nb, lr, _ = x.shape
    tm = POST_TM
    const = lambda arr: pl.BlockSpec(arr.shape, lambda b, i: (0,) * arr.ndim, pipeline_mode=pl.Buffered(1))
    weights = [w["ag"], w["wo"], w["n2"], w["w1"], w["w2"], w["fg"]]
    return pl.pallas_call(
        _post_kernel,
        grid=(nb, lr // tm),
        in_specs=[
            pl.BlockSpec((1, tm, D_MODEL), lambda b, i: (b, i, 0)),
            pl.BlockSpec((1, tm, N_HEADS * V_DIM), lambda b, i: (b, i, 0)),
            pl.BlockSpec((1, tm, SSM_W), lambda b, i: (b, i, 0)),
        ] + [const(arr) for arr in weights],
        out_specs=pl.BlockSpec((1, tm, D_MODEL), lambda b, i: (b, i, 0)),
        out_shape=jax.ShapeDtypeStruct((nb, lr, D_MODEL), F32),
        compiler_params=pltpu.CompilerParams(
            dimension_semantics=("arbitrary", "arbitrary"), vmem_limit_bytes=_vmem_limit(48 << 20)),
        name="post",
    )(x, a, sn, *weights)


def _rope_tables(length):
    inv = ROPE_THETA ** (-jnp.arange(0, QK_ROPE, 2, dtype=F32) / QK_ROPE)
    zeros = lambda n: jnp.zeros((n,), F32)
    inv_lanes = jnp.concatenate([zeros(QK_NOPE), inv, inv, zeros(HEAD_PAD - QK_NOPE - QK_ROPE)])
    n_hi = -(-length // ROPE_BLOCK)
    a_hi = (jnp.arange(n_hi, dtype=F32) * ROPE_BLOCK)[:, None] * inv_lanes
    a_lo = jnp.arange(ROPE_BLOCK, dtype=F32)[:, None] * inv_lanes
    ch, sh = jnp.cos(a_hi)[:, None], jnp.sin(a_hi)[:, None]
    cl, sl = jnp.cos(a_lo)[None], jnp.sin(a_lo)[None]
    used = (jnp.arange(HEAD_PAD) < QK_NOPE + QK_ROPE).astype(F32)
    cos_t = ((ch * cl - sh * sl) * used).reshape(n_hi * ROPE_BLOCK, HEAD_PAD)[:length]
    sin_t = (sh * cl + ch * sl).reshape(n_hi * ROPE_BLOCK, HEAD_PAD)[:length]
    return cos_t, sin_t


def _rot_half_cols(w):
    half = QK_ROPE // 2
    return jnp.concatenate([-w[..., half:], w[..., :half]], axis=-1)


def _pack_weights(norm1_g, w_in, q_norm_g, w_uq, kv_norm_g, w_ukv, d_skip, w_glu, b_glu,
                  attn_out_g, ssm_out_g, w_out, norm2_g, w_mlp1, w_mlp2, final_g):
    row = lambda g: g.reshape(1, -1).astype(F32)
    o = Q_RANK + KV_RANK
    w_kr = w_in[:, o:o + QK_ROPE]
    zk = jnp.zeros((D_MODEL, QK_NOPE), F32)
    wa = jnp.concatenate([w_in[:, :o], zk, w_kr, _rot_half_cols(w_kr)], axis=1)
    wu = w_in[:, o + QK_ROPE:]
    wq = w_uq.reshape(Q_RANK, N_HEADS, QK_NOPE + QK_ROPE)
    zq = jnp.zeros((Q_RANK, N_HEADS, HEAD_PAD - QK_NOPE - QK_ROPE), F32)
    wqa = jnp.concatenate([wq, zq], axis=-1).reshape(Q_RANK, N_HEADS * HEAD_PAD)
    wqb = jnp.concatenate([jnp.zeros((Q_RANK, N_HEADS, QK_NOPE), F32), _rot_half_cols(wq[..., QK_NOPE:]), zq],
                          axis=-1).reshape(Q_RANK, N_HEADS * HEAD_PAD)
    assert QK_NOPE + V_DIM == HEAD_PAD
    bf = lambda a: a.astype(BF16)
    return dict(
        n1=row(norm1_g), wa=bf(wa), wu=bf(wu), qg=row(q_norm_g), wqa=bf(wqa), wqb=bf(wqb),
        kvg=row(kv_norm_g), wkv=bf(w_ukv),
        dsk=row(d_skip), wglu=bf(w_glu), bglu=row(b_glu), sg=row(ssm_out_g),
        ag=row(attn_out_g), wo=bf(w_out), n2=row(norm2_g), w1=bf(w_mlp1), w2=bf(w_mlp2), fg=row(final_g))


def _pack_ssm(lam_re, lam_im, log_dt, b_re, b_im, c_re, c_im, seg_len):
    cmul = lambda xr, xi, yr, yi: (xr * yr - xi * yi, xr * yi + xi * yr)
    lam_re, lam_im = lam_re.astype(F32), lam_im.astype(F32)
    dt = jnp.exp(log_dt.astype(F32))[..., None]
    mag = jnp.exp(lam_re * dt)
    a_re, a_im = mag * jnp.cos(lam_im * dt), mag * jnp.sin(lam_im * dt)
    den = lam_re * lam_re + lam_im * lam_im
    k_re = ((a_re - 1.0) * lam_re + a_im * lam_im) / den
    k_im = (a_im * lam_re - (a_re - 1.0) * lam_im) / den
    bb_re, bb_im = cmul(k_re[..., None], k_im[..., None], b_re.astype(F32), b_im.astype(F32))
    p_re, p_im = a_re, a_im
    s_re, s_im = jnp.ones_like(a_re), jnp.zeros_like(a_re)
    n = seg_len
    while n:
        if n & 1:
            s_re, s_im = cmul(s_re, s_im, p_re, p_im)
        p_re, p_im = cmul(p_re, p_im, p_re, p_im)
        n >>= 1
    eye = jnp.eye(GROUPS_PER_BLOCK, dtype=F32)
    nq, gb = SSM_LANE_BLOCKS, GROUPS_PER_BLOCK

    def b_block(part):
        p = part.reshape(2, nq, gb, SSM_STATE, SSM_GROUP)
        return jnp.einsum("dqgnh,gk->dqghkn", p, eye).reshape(2, nq, LANES, STATES_PER_BLOCK)

    def c_block(part):
        p = part.reshape(2, nq, gb, SSM_GROUP, SSM_STATE)
        return jnp.einsum("dqghn,gk->dqgnkh", p, eye).reshape(2, nq, STATES_PER_BLOCK, LANES)

    def a_block(zr, zi):
        shape = (2, nq, 1, STATES_PER_BLOCK)
        return jnp.concatenate([zr.reshape(shape), zi.reshape(shape)], axis=-1)

    bmat = jnp.concatenate([b_block(bb_re), b_block(bb_im)], axis=-1).astype(BF16)
    cmat = jnp.concatenate([c_block(c_re.astype(F32)), -c_block(c_im.astype(F32))], axis=-2).astype(BF16)
    return bmat, cmat, a_block(a_re, a_im), a_block(s_re, s_im)


def _perm_matrix(tt):
    rows = N_GROUPS * tt
    dst = np.arange(rows)
    src = (dst % N_GROUPS) * tt + dst // N_GROUPS
    p = np.zeros((rows, rows), np.float32)
    p[dst, src] = 1.0
    return jnp.asarray(p, BF16)


def _trunk(x, rope, segmented, w, ssm):
    nb, lr, _ = x.shape
    bmat, cmat, a_blk, a_seg = ssm
    cos_t, sin_t = rope
    perm = _perm_matrix(PERM_TT)
    qt, k, vt, u = _pre_call(x, cos_t, sin_t, w, perm)
    seq_seg = (lambda a: a[None]) if segmented else (lambda a: a[:, None])
    att = _attn_call(seq_seg(qt), seq_seg(k), seq_seg(vt)).reshape(nb, lr, N_HEADS * V_DIM)
    init = jnp.zeros((2, SSM_LANE_BLOCKS, N_GROUPS, 2 * STATES_PER_BLOCK), F32)
    if segmented:
        (ends,) = _ssm_call(u, bmat, cmat, a_blk, init, with_y=False)
        init = _carry_call(ends, a_seg)
    yf, yb, _ = _ssm_call(u, bmat, cmat, a_blk, init, with_y=True)
    sn = _ssm_post_call(yf, yb, u, w, perm.T)
    return _post_call(x, att, sn, w)


def kernel(x_prompt, x_sample, norm1_g, w_in, q_norm_g, w_uq, kv_norm_g, w_ukv, lam_re, lam_im, log_dt,
           b_re, b_im, c_re, c_im, d_skip, w_glu, b_glu, attn_out_g, ssm_out_g, w_out, norm2_g, w_mlp1,
           w_mlp2, final_g):
    assert norm1_g.shape[0] == 1, "single-layer trunk"
    w = _pack_weights(norm1_g[0], w_in[0], q_norm_g[0], w_uq[0], kv_norm_g[0], w_ukv[0], d_skip[0], w_glu[0],
                      b_glu[0], attn_out_g[0], ssm_out_g[0], w_out[0], norm2_g[0], w_mlp1[0], w_mlp2[0], final_g)
    bp, lp, _ = x_prompt.shape
    bs, ls, _ = x_sample.shape
    assert bp == N_GROUPS and bs == 1 and ls % N_GROUPS == 0
    seg = ls // N_GROUPS
    ssm = _pack_ssm(lam_re[0], lam_im[0], log_dt[0], b_re[0], b_im[0], c_re[0], c_im[0], seg)
    cos_t, sin_t = _rope_tables(max(lp, ls))
    rope_p = (cos_t[None, :lp], sin_t[None, :lp])
    rope_s = (cos_t[:ls].reshape(N_GROUPS, seg, HEAD_PAD), sin_t[:ls].reshape(N_GROUPS, seg, HEAD_PAD))
    y_prompt = _trunk(x_prompt, rope_p, False, w, ssm)
    y_sample = _trunk(x_sample.reshape(N_GROUPS, seg, D_MODEL), rope_s, True, w, ssm)
    return y_prompt, y_sample.reshape(bs, ls, D_MODEL)
```

```python
import functools
import math

import jax
import jax.numpy as jnp
import numpy as np
from jax import lax
from jax.experimental import pallas as pl
from jax.experimental.pallas import tpu as pltpu

F32 = jnp.float32
BF16 = jnp.bfloat16

D_MODEL = 1024
N_HEADS = 8
QK_NOPE = 64
QK_ROPE = 32
V_DIM = 64
Q_RANK = 256
KV_RANK = 128
SSM_W = 512
SSM_GROUP = 16
SSM_GROUPS = 32
SSM_STATE = 64
D_FF = 4096
EPS = 1e-6
ROPE_THETA = 10000.0
ROPE_BLOCK = 128

LANES = 128
SUBLANES = 8
VMEM_LIMIT_CAP = 60000 * 1024

N_GROUPS = SUBLANES
HEAD_PAD = LANES
BF16_SUBLANES = 2 * SUBLANES
VT_ROWS = -(-(V_DIM + 1) // BF16_SUBLANES) * BF16_SUBLANES
N_PAIRS = N_HEADS // 2
SSM_LANE_BLOCKS = SSM_W // LANES
GROUPS_PER_BLOCK = LANES // SSM_GROUP
STATES_PER_BLOCK = GROUPS_PER_BLOCK * SSM_STATE

PRE_TT = 128
PERM_TT = 32
ATT_BQ = 512
ATT_BK = 256
ATT_UNROLL = 30
Q_SCALE = math.log2(math.e) / math.sqrt(QK_NOPE + QK_ROPE)
SSM_TT = 128
POST_TM = 512
FF_CHUNK = 1024


def _vmem_limit(nbytes):
    return int(min(VMEM_LIMIT_CAP, nbytes))


def _rms(x, g):
    return x * lax.rsqrt(jnp.mean(x * x, axis=-1, keepdims=True) + EPS) * g


def _dot(a, b):
    return jnp.dot(a, b, preferred_element_type=F32)


def _pre_kernel(x_ref, cos_ref, sin_ref, n1_ref, wa_ref, wu_ref, qg_ref, wqa_ref, wqb_ref,
                kvg_ref, wkv_ref, perm_ref, qt_ref, k_ref, vt_ref, u_ref, *, tt):
    rows = N_GROUPS * tt
    x = x_ref[...].reshape(rows, D_MODEL)
    h = _rms(x, n1_ref[...]).astype(BF16)
    pa = _dot(h, wa_ref[...])
    u = _dot(h, wu_ref[...])
    cqn = _rms(pa[:, :Q_RANK], qg_ref[...]).astype(BF16)
    ckvn = _rms(pa[:, Q_RANK:Q_RANK + KV_RANK], kvg_ref[...]).astype(BF16)
    cos = jnp.broadcast_to(cos_ref[...], (N_GROUPS, tt, HEAD_PAD)).reshape(rows, HEAD_PAD)
    sin = jnp.broadcast_to(sin_ref[...], (N_GROUPS, tt, HEAD_PAD)).reshape(rows, HEAD_PAD)
    o = Q_RANK + KV_RANK
    kr = pa[:, o:o + HEAD_PAD]
    k_rope = kr * cos + pltpu.roll(kr, HEAD_PAD - QK_ROPE, 1) * sin
    qa = _dot(cqn, wqa_ref[...])
    qb = _dot(cqn, wqb_ref[...])
    kv = _dot(ckvn, wkv_ref[...])
    nope_lanes = lax.broadcasted_iota(jnp.int32, (1, HEAD_PAD), 1) < QK_NOPE
    vt_tail = (lax.broadcasted_iota(jnp.int32, (VT_ROWS - V_DIM, tt), 0) == 0).astype(F32)
    for hd in range(N_HEADS):
        sl = slice(hd * HEAD_PAD, (hd + 1) * HEAD_PAD)
        qht = ((qa[:, sl] * cos + qb[:, sl] * sin) * Q_SCALE).T
        kvt = kv[:, sl].T
        for b in range(N_GROUPS):
            cols = slice(b * tt, (b + 1) * tt)
            qt_ref[b, hd] = qht[:, cols].astype(BF16)
            vt_ref[b, hd, 0] = jnp.concatenate([kvt[QK_NOPE:, cols], vt_tail], axis=0).astype(BF16)
        kh = jnp.where(nope_lanes, kv[:, sl], 0.0) + k_rope
        k_ref[:, hd] = kh.astype(BF16).reshape(N_GROUPS, tt, HEAD_PAD)
    u_hi = u.astype(BF16)
    u_lo = (u - u_hi.astype(F32)).astype(BF16)
    perm = perm_ref[...]
    sub = N_GROUPS * PERM_TT
    for s in range(tt // PERM_TT):
        pick = lambda a: jnp.concatenate(
            [a[b * tt + s * PERM_TT:b * tt + (s + 1) * PERM_TT] for b in range(N_GROUPS)], axis=0)
        u_ref[s * sub:(s + 1) * sub, :] = _dot(perm, pick(u_hi)) + _dot(perm, pick(u_lo))


def _pre_call(x, cos_t, sin_t, w, perm):
    nb, lr, _ = x.shape
    tt = PRE_TT
    rows = nb * tt
    n_t = lr // tt
    tab_nb = cos_t.shape[0]
    kt_sub = ATT_BK // tt
    full = lambda a: pl.BlockSpec(a.shape, lambda j: (0,) * a.ndim)
    weights = [w["n1"], w["wa"], w["wu"], w["qg"], w["wqa"], w["wqb"], w["kvg"], w["wkv"], perm]
    in_specs = [
        pl.BlockSpec((nb, tt, D_MODEL), lambda j: (0, j, 0)),
        pl.BlockSpec((tab_nb, tt, HEAD_PAD), lambda j: (0, j, 0)),
        pl.BlockSpec((tab_nb, tt, HEAD_PAD), lambda j: (0, j, 0)),
    ] + [full(a) for a in weights]
    out_shape = [
        jax.ShapeDtypeStruct((nb, N_HEADS, HEAD_PAD, lr), BF16),
        jax.ShapeDtypeStruct((nb, N_HEADS, lr, HEAD_PAD), BF16),
        jax.ShapeDtypeStruct((nb, N_HEADS, lr // ATT_BK, VT_ROWS, ATT_BK), BF16),
        jax.ShapeDtypeStruct((lr * nb, SSM_W), F32),
    ]
    out_specs = [
        pl.BlockSpec((nb, N_HEADS, HEAD_PAD, tt), lambda j: (0, 0, 0, j)),
        pl.BlockSpec((nb, N_HEADS, tt, HEAD_PAD), lambda j: (0, 0, j, 0)),
        pl.BlockSpec((nb, N_HEADS, 1, VT_ROWS, tt), lambda j: (0, 0, j // kt_sub, 0, j % kt_sub)),
        pl.BlockSpec((rows, SSM_W), lambda j: (j, 0)),
    ]
    return pl.pallas_call(
        functools.partial(_pre_kernel, tt=tt),
        grid=(n_t,),
        in_specs=in_specs,
        out_specs=out_specs,
        out_shape=out_shape,
        compiler_params=pltpu.CompilerParams(
            dimension_semantics=("arbitrary",), vmem_limit_bytes=_vmem_limit(56 << 20)),
        name="pre",
    )(x, cos_t, sin_t, *weights)


def _attn_kernel(*refs, n_seg, n_kb, n_q):
    for qb in range(n_q):
        _attn_q_block(qb, *refs, n_seg=n_seg, n_kb=n_kb)


def _attn_q_block(qb, qt_ref, k_ref, vt_ref, o_ref, s_ref, p_ref, alpha_ref, m_ref, acc_ref, *, n_seg, n_kb):
    bq, bk = ATT_BQ, ATT_BK
    m_ref[...] = jnp.full(m_ref.shape, -0.5 * float(np.finfo(np.float32).max), F32)
    acc_ref[...] = jnp.zeros(acc_ref.shape, F32)

    n = n_seg * n_kb

    def scores(c, slot):
        r0 = (c % n_kb) * bk
        if not isinstance(c, int):
            r0 = pl.multiple_of(r0, bk)
        for j in range(2):
            s_ref[slot, j, :, :bq] = _dot(k_ref[0, c // n_kb, j, pl.ds(r0, bk), :],
                                          qt_ref[0, 0, j, :, qb * bq:(qb + 1) * bq])

    def softmax(slot):
        for j in range(2):
            for lt in range(bq // LANES):
                lanes = slice(lt * LANES, (lt + 1) * LANES)
                blk = s_ref[slot, j, :, lanes]
                m_old = m_ref[j, :, lanes]
                m_new = jnp.maximum(m_old, jnp.max(blk, axis=0, keepdims=True))
                alpha_ref[slot, j, :, lanes] = jnp.exp2(m_old - m_new)
                m_ref[j, :, lanes] = m_new
                p_ref[slot, j, :, lanes] = jnp.exp2(blk - m_new).astype(BF16)

    def weighted_values(c, slot):
        for j in range(2):
            pv = _dot(vt_ref[0, c // n_kb, j, c % n_kb], p_ref[slot, j, :, :bq])
            acc_ref[j] = acc_ref[j] * alpha_ref[slot, j] + pv

    steady = max(n - 2, 0)
    unroll = min(ATT_UNROLL, max(steady - steady % 2, 2))

    def stage(t, parity, do_scores=True, do_softmax=True, do_values=True):
        if do_values:
            weighted_values(t - 2, parity)
        if do_softmax:
            softmax(1 - parity)
        if do_scores:
            scores(t, parity)

    for t in range(2):
        stage(t, t % 2, t < n, 1 <= t <= n, False)
    n_iter = steady // unroll
    if n_iter:
        def body(k, _):
            for i in range(unroll):
                stage(2 + unroll * k + i, i % 2)
            return 0
        lax.fori_loop(0, n_iter, body, 0)
    for t in range(2 + unroll * n_iter, n + 2):
        stage(t, t % 2, t < n, t <= n, True)
    outs = [acc_ref[j][:V_DIM] / acc_ref[j][V_DIM:V_DIM + 1] for j in range(2)]
    o_ref[0, qb * bq:(qb + 1) * bq, :] = jnp.concatenate(outs, axis=0).T


def _attn_call(qt, k, vt):
    n_seq, n_seg, _, lr, _ = k.shape
    n_kb = lr // ATT_BK
    n_q = 2 if n_seg == 1 and lr % (2 * ATT_BQ) == 0 else 1
    step_q = n_q * ATT_BQ
    n_qb = lr // step_q
    scratch = [
        pltpu.VMEM((2, 2, ATT_BK, ATT_BQ + LANES), F32),
        pltpu.VMEM((2, 2, ATT_BK, ATT_BQ + LANES), BF16),
        pltpu.VMEM((2, 2, 1, ATT_BQ), F32),
        pltpu.VMEM((2, 1, ATT_BQ), F32),
        pltpu.VMEM((2, VT_ROWS, ATT_BQ), F32),
    ]
    in_specs = [
        pl.BlockSpec((1, 1, 2, HEAD_PAD, step_q), lambda s, p, i: (s, i // n_qb, p, 0, i % n_qb)),
        pl.BlockSpec((1, n_seg, 2, lr, HEAD_PAD), lambda s, p, i: (s, 0, p, 0, 0)),
        pl.BlockSpec((1, n_seg, 2, n_kb, VT_ROWS, ATT_BK), lambda s, p, i: (s, 0, p, 0, 0, 0)),
    ]
    kv_bytes = 2 * n_seg * lr * (HEAD_PAD + VT_ROWS) * 2 * 2
    return pl.pallas_call(
        functools.partial(_attn_kernel, n_seg=n_seg, n_kb=n_kb, n_q=n_q),
        grid=(n_seq, N_PAIRS, n_seg * n_qb),
        in_specs=in_specs,
        out_specs=pl.BlockSpec((1, step_q, HEAD_PAD), lambda s, p, i: (s, i, p)),
        out_shape=jax.ShapeDtypeStruct((n_seq, n_seg * lr, N_HEADS * V_DIM), F32),
        scratch_shapes=scratch,
        compiler_params=pltpu.CompilerParams(
            dimension_semantics=("arbitrary", "arbitrary", "arbitrary"),
            vmem_limit_bytes=_vmem_limit(kv_bytes + (24 << 20))),
        name="attn",
    )(qt, k, vt)


def _cstep(xre, xim, are, aim, bre, bim):
    return are * xre - aim * xim + bre, are * xim + aim * xre + bim


def _ssm_kernel(uf_ref, ub_ref, bmat_ref, cmat_ref, a_ref, init_ref, *rest, tt, with_y):
    if with_y:
        yf_ref, yb_ref, fin_ref, state_ref, bu_ref, xs_ref = rest
    else:
        fin_ref, state_ref, bu_ref = rest
    j = pl.program_id(0)
    rows = N_GROUPS * tt
    n_pairs = tt // 2
    S = STATES_PER_BLOCK

    @pl.when(j == 0)
    def _():
        state_ref[...] = init_ref[...]

    for q in range(SSM_LANE_BLOCKS):
        lanes = slice(q * LANES, (q + 1) * LANES)
        slot = q % 2
        bu_ref[slot, 0] = _dot(uf_ref[:, lanes].astype(BF16), bmat_ref[0, q])
        bu_ref[slot, 1] = _dot(ub_ref[:, lanes].astype(BF16), bmat_ref[1, q])
        a = [jnp.broadcast_to(a_ref[d, q], (N_GROUPS, 2 * S)) for d in range(2)]
        are = [a[d][:, :S] for d in range(2)]
        aim = [a[d][:, S:] for d in range(2)]
        fwd = (state_ref[0, q, :, :S], state_ref[0, q, :, S:])
        bwd = (state_ref[1, q, :, :S], state_ref[1, q, :, S:])
        for k in range(n_pairs):
            r0 = k * 2 * N_GROUPS
            blk = bu_ref[slot, 0, r0:r0 + 2 * N_GROUPS, :]
            f1 = _cstep(*fwd, are[0], aim[0], blk[:N_GROUPS, :S], blk[:N_GROUPS, S:])
            fwd = _cstep(*f1, are[0], aim[0], blk[N_GROUPS:, :S], blk[N_GROUPS:, S:])
            r1 = rows - (k + 1) * 2 * N_GROUPS
            blk = bu_ref[slot, 1, r1:r1 + 2 * N_GROUPS, :]
            b1 = _cstep(*bwd, are[1], aim[1], blk[N_GROUPS:, :S], blk[N_GROUPS:, S:])
            bwd = _cstep(*b1, are[1], aim[1], blk[:N_GROUPS, :S], blk[:N_GROUPS, S:])
            if with_y:
                xs_ref[slot, 0, r0:r0 + 2 * N_GROUPS, :] = jnp.concatenate(
                    [jnp.concatenate(f1, axis=1), jnp.concatenate(fwd, axis=1)], axis=0).astype(BF16)
                xs_ref[slot, 1, r1:r1 + 2 * N_GROUPS, :] = jnp.concatenate(
                    [jnp.concatenate(bwd, axis=1), jnp.concatenate(b1, axis=1)], axis=0).astype(BF16)
        state_ref[0, q] = jnp.concatenate(fwd, axis=1)
        state_ref[1, q] = jnp.concatenate(bwd, axis=1)
        if with_y:
            yf_ref[:, lanes] = _dot(xs_ref[slot, 0], cmat_ref[0, q])
            yb_ref[:, lanes] = _dot(xs_ref[slot, 1], cmat_ref[1, q])

    @pl.when(j == pl.num_programs(0) - 1)
    def _():
        fin_ref[...] = state_ref[...]


def _ssm_call(u, bmat, cmat, a, init, with_y):
    n_rows = u.shape[0]
    tt = SSM_TT
    rows = N_GROUPS * tt
    n_t = n_rows // rows
    full = lambda arr: pl.BlockSpec(arr.shape, lambda j: (0,) * arr.ndim)
    in_specs = [
        pl.BlockSpec((rows, SSM_W), lambda j: (j, 0)),
        pl.BlockSpec((rows, SSM_W), lambda j: (n_t - 1 - j, 0)),
        full(bmat), full(cmat), full(a), full(init),
    ]
    state_shape = (2, SSM_LANE_BLOCKS, N_GROUPS, 2 * STATES_PER_BLOCK)
    out_shape = [jax.ShapeDtypeStruct(state_shape, F32)]
    out_specs = [pl.BlockSpec(state_shape, lambda j: (0, 0, 0, 0))]
    scratch = [pltpu.VMEM(state_shape, F32), pltpu.VMEM((2, 2, rows, 2 * STATES_PER_BLOCK), F32)]
    if with_y:
        out_shape = [jax.ShapeDtypeStruct((n_rows, SSM_W), F32)] * 2 + out_shape
        out_specs = [pl.BlockSpec((rows, SSM_W), lambda j: (j, 0)),
                     pl.BlockSpec((rows, SSM_W), lambda j: (n_t - 1 - j, 0))] + out_specs
        scratch.append(pltpu.VMEM((2, 2, rows, 2 * STATES_PER_BLOCK), BF16))
    return pl.pallas_call(
        functools.partial(_ssm_kernel, tt=tt, with_y=with_y),
        grid=(n_t,),
        in_specs=in_specs,
        out_specs=out_specs,
        out_shape=out_shape,
        scratch_shapes=scratch,
        compiler_params=pltpu.CompilerParams(
            dimension_semantics=("arbitrary",), vmem_limit_bytes=_vmem_limit(57 << 20)),
        name="ssm" if with_y else "ssm_ends",
    )(u, u, bmat, cmat, a, init)


def _carry_kernel(e_ref, as_ref, i_ref):
    S = STATES_PER_BLOCK
    row = lax.broadcasted_iota(jnp.int32, (N_GROUPS, S), 0)
    for d in range(2):
        first = 0 if d == 0 else N_GROUPS - 1
        shift = 1 if d == 0 else N_GROUPS - 1
        for q in range(SSM_LANE_BLOCKS):
            e = e_ref[d, q]
            a = jnp.broadcast_to(as_ref[d, q], (N_GROUPS, 2 * S))
            ere, eim, are, aim = e[:, :S], e[:, S:], a[:, :S], a[:, S:]
            ire = jnp.zeros((N_GROUPS, S), F32)
            iim = jnp.zeros((N_GROUPS, S), F32)
            for _ in range(N_GROUPS - 1):
                tre, tim = _cstep(ire, iim, are, aim, ere, eim)
                ire = jnp.where(row == first, 0.0, pltpu.roll(tre, shift, 0))
                iim = jnp.where(row == first, 0.0, pltpu.roll(tim, shift, 0))
            i_ref[d, q] = jnp.concatenate([ire, iim], axis=1)


def _carry_call(ends, a_seg):
    return pl.pallas_call(
        _carry_kernel,
        out_shape=jax.ShapeDtypeStruct(ends.shape, F32),
        name="ssm_carry",
    )(ends, a_seg)


def _ssm_post_kernel(yf_ref, yb_ref, u_ref, dsk_ref, wglu_ref, bglu_ref, sg_ref, permt_ref, o_ref, *, tt):
    y = yf_ref[...] + yb_ref[...] + dsk_ref[...] * u_ref[...]
    g = jax.nn.gelu(y)
    z = _dot(g.astype(BF16), wglu_ref[...]) + bglu_ref[...]
    s = g * jax.nn.sigmoid(z)
    sn = _rms(s, sg_ref[...]).astype(BF16)
    permt = permt_ref[...]
    sub = N_GROUPS * PERM_TT
    for k in range(tt // PERM_TT):
        blk = _dot(permt, sn[k * sub:(k + 1) * sub]).astype(BF16)
        o_ref[:, k * PERM_TT:(k + 1) * PERM_TT, :] = blk.reshape(N_GROUPS, PERM_TT, SSM_W)


def _ssm_post_call(yf, yb, u, w, permt):
    n_rows = u.shape[0]
    tt = PRE_TT
    rows = N_GROUPS * tt
    n_t = n_rows // rows
    lr = n_rows // N_GROUPS
    full = lambda a: pl.BlockSpec(a.shape, lambda j: (0,) * a.ndim)
    row_spec = pl.BlockSpec((rows, SSM_W), lambda j: (j, 0))
    weights = [w["dsk"], w["wglu"], w["bglu"], w["sg"], permt]
    return pl.pallas_call(
        functools.partial(_ssm_post_kernel, tt=tt),
        grid=(n_t,),
        in_specs=[row_spec, row_spec, row_spec] + [full(a) for a in weights],
        out_specs=pl.BlockSpec((N_GROUPS, tt, SSM_W), lambda j: (0, j, 0)),
        out_shape=jax.ShapeDtypeStruct((N_GROUPS, lr, SSM_W), BF16),
        compiler_params=pltpu.CompilerParams(
            dimension_semantics=("arbitrary",), vmem_limit_bytes=_vmem_limit(40 << 20)),
        name="ssm_post",
    )(yf, yb, u, *weights)


def _post_kernel(x_ref, a_ref, sn_ref, ag_ref, wo_ref, n2_ref, w1_ref, w2_ref, fg_ref, o_ref):
    an = _rms(a_ref[0], ag_ref[...]).astype(BF16)
    mixed = jnp.concatenate([an, sn_ref[0]], axis=-1)
    x1 = x_ref[0] + _dot(mixed, wo_ref[...])
    h2 = _rms(x1, n2_ref[...]).astype(BF16)
    acc = jnp.zeros_like(x1)
    for c in range(D_FF // FF_CHUNK):
        hid = _dot(h2, w1_ref[:, c * FF_CHUNK:(c + 1) * FF_CHUNK])
        hid = jnp.square(jnp.maximum(hid, 0.0)).astype(BF16)
        acc = acc + _dot(hid, w2_ref[c * FF_CHUNK:(c + 1) * FF_CHUNK, :])
    o_ref[0] = _rms(x1 + acc, fg_ref[...])


def _post_call(x, a, sn, w):
    nb, lr, _ = x.shape
    tm = POST_TM
    const = lambda arr: pl.BlockSpec(arr.shape, lambda b, i: (0,) * arr.ndim, pipeline_mode=pl.Buffered(1))
    weights = [w["ag"], w["wo"], w["n2"], w["w1"], w["w2"], w["fg"]]
    return pl.pallas_call(
        _post_kernel,
        grid=(nb, lr // tm),
        in_specs=[
            pl.BlockSpec((1, tm, D_MODEL), lambda b, i: (b, i, 0)),
            pl.BlockSpec((1, tm, N_HEADS * V_DIM), lambda b, i: (b, i, 0)),
            pl.BlockSpec((1, tm, SSM_W), lambda b, i: (b, i, 0)),
        ] + [const(arr) for arr in weights],
        out_specs=pl.BlockSpec((1, tm, D_MODEL), lambda b, i: (b, i, 0)),
        out_shape=jax.ShapeDtypeStruct((nb, lr, D_MODEL), F32),
        compiler_params=pltpu.CompilerParams(
            dimension_semantics=("arbitrary", "arbitrary"), vmem_limit_bytes=_vmem_limit(48 << 20)),
        name="post",
    )(x, a, sn, *weights)


def _rope_tables(length):
    inv = ROPE_THETA ** (-jnp.arange(0, QK_ROPE, 2, dtype=F32) / QK_ROPE)
    zeros = lambda n: jnp.zeros((n,), F32)
    inv_lanes = jnp.concatenate([zeros(QK_NOPE), inv, inv, zeros(HEAD_PAD - QK_NOPE - QK_ROPE)])
    n_hi = -(-length // ROPE_BLOCK)
    a_hi = (jnp.arange(n_hi, dtype=F32) * ROPE_BLOCK)[:, None] * inv_lanes
    a_lo = jnp.arange(ROPE_BLOCK, dtype=F32)[:, None] * inv_lanes
    ch, sh = jnp.cos(a_hi)[:, None], jnp.sin(a_hi)[:, None]
    cl, sl = jnp.cos(a_lo)[None], jnp.sin(a_lo)[None]
    used = (jnp.arange(HEAD_PAD) < QK_NOPE + QK_ROPE).astype(F32)
    cos_t = ((ch * cl - sh * sl) * used).reshape(n_hi * ROPE_BLOCK, HEAD_PAD)[:length]
    sin_t = (sh * cl + ch * sl).reshape(n_hi * ROPE_BLOCK, HEAD_PAD)[:length]
    return cos_t, sin_t


def _rot_half_cols(w):
    half = QK_ROPE // 2
    return jnp.concatenate([-w[..., half:], w[..., :half]], axis=-1)


def _pack_weights(norm1_g, w_in, q_norm_g, w_uq, kv_norm_g, w_ukv, d_skip, w_glu, b_glu,
                  attn_out_g, ssm_out_g, w_out, norm2_g, w_mlp1, w_mlp2, final_g):
    row = lambda g: g.reshape(1, -1).astype(F32)
    o = Q_RANK + KV_RANK
    w_kr = w_in[:, o:o + QK_ROPE]
    zk = jnp.zeros((D_MODEL, QK_NOPE), F32)
    wa = jnp.concatenate([w_in[:, :o], zk, w_kr, _rot_half_cols(w_kr)], axis=1)
    wu = w_in[:, o + QK_ROPE:]
    wq = w_uq.reshape(Q_RANK, N_HEADS, QK_NOPE + QK_ROPE)
    zq = jnp.zeros((Q_RANK, N_HEADS, HEAD_PAD - QK_NOPE - QK_ROPE), F32)
    wqa = jnp.concatenate([wq, zq], axis=-1).reshape(Q_RANK, N_HEADS * HEAD_PAD)
    wqb = jnp.concatenate([jnp.zeros((Q_RANK, N_HEADS, QK_NOPE), F32), _rot_half_cols(wq[..., QK_NOPE:]), zq],
                          axis=-1).reshape(Q_RANK, N_HEADS * HEAD_PAD)
    assert QK_NOPE + V_DIM == HEAD_PAD
    bf = lambda a: a.astype(BF16)
    return dict(
        n1=row(norm1_g), wa=bf(wa), wu=bf(wu), qg=row(q_norm_g), wqa=bf(wqa), wqb=bf(wqb),
        kvg=row(kv_norm_g), wkv=bf(w_ukv),
        dsk=row(d_skip), wglu=bf(w_glu), bglu=row(b_glu), sg=row(ssm_out_g),
        ag=row(attn_out_g), wo=bf(w_out), n2=row(norm2_g), w1=bf(w_mlp1), w2=bf(w_mlp2), fg=row(final_g))


def _pack_ssm(lam_re, lam_im, log_dt, b_re, b_im, c_re, c_im, seg_len):
    cmul = lambda xr, xi, yr, yi: (xr * yr - xi * yi, xr * yi + xi * yr)
    lam_re, lam_im = lam_re.astype(F32), lam_im.astype(F32)
    dt = jnp.exp(log_dt.astype(F32))[..., None]
    mag = jnp.exp(lam_re * dt)
    a_re, a_im = mag * jnp.cos(lam_im * dt), mag * jnp.sin(lam_im * dt)
    den = lam_re * lam_re + lam_im * lam_im
    k_re = ((a_re - 1.0) * lam_re + a_im * lam_im) / den
    k_im = (a_im * lam_re - (a_re - 1.0) * lam_im) / den
    bb_re, bb_im = cmul(k_re[..., None], k_im[..., None], b_re.astype(F32), b_im.astype(F32))
    p_re, p_im = a_re, a_im
    s_re, s_im = jnp.ones_like(a_re), jnp.zeros_like(a_re)
    n = seg_len
    while n:
        if n & 1:
            s_re, s_im = cmul(s_re, s_im, p_re, p_im)
        p_re, p_im = cmul(p_re, p_im, p_re, p_im)
        n >>= 1
    eye = jnp.eye(GROUPS_PER_BLOCK, dtype=F32)
    nq, gb = SSM_LANE_BLOCKS, GROUPS_PER_BLOCK

    def b_block(part):
        p = part.reshape(2, nq, gb, SSM_STATE, SSM_GROUP)
        return jnp.einsum("dqgnh,gk->dqghkn", p, eye).reshape(2, nq, LANES, STATES_PER_BLOCK)

    def c_block(part):
        p = part.reshape(2, nq, gb, SSM_GROUP, SSM_STATE)
        return jnp.einsum("dqghn,gk->dqgnkh", p, eye).reshape(2, nq, STATES_PER_BLOCK, LANES)

    def a_block(zr, zi):
        shape = (2, nq, 1, STATES_PER_BLOCK)
        return jnp.concatenate([zr.reshape(shape), zi.reshape(shape)], axis=-1)

    bmat = jnp.concatenate([b_block(bb_re), b_block(bb_im)], axis=-1).astype(BF16)
    cmat = jnp.concatenate([c_block(c_re.astype(F32)), -c_block(c_im.astype(F32))], axis=-2).astype(BF16)
    return bmat, cmat, a_block(a_re, a_im), a_block(s_re, s_im)


def _perm_matrix(tt):
    rows = N_GROUPS * tt
    dst = np.arange(rows)
    src = (dst % N_GROUPS) * tt + dst // N_GROUPS
    p = np.zeros((rows, rows), np.float32)
    p[dst, src] = 1.0
    return jnp.asarray(p, BF16)


def _trunk(x, rope, segmented, w, ssm):
    nb, lr, _ = x.shape
    bmat, cmat, a_blk, a_seg = ssm
    cos_t, sin_t = rope
    perm = _perm_matrix(PERM_TT)
    qt, k, vt, u = _pre_call(x, cos_t, sin_t, w, perm)
    seq_seg = (lambda a: a[None]) if segmented else (lambda a: a[:, None])
    att = _attn_call(seq_seg(qt), seq_seg(k), seq_seg(vt)).reshape(nb, lr, N_HEADS * V_DIM)
    init = jnp.zeros((2, SSM_LANE_BLOCKS, N_GROUPS, 2 * STATES_PER_BLOCK), F32)
    if segmented:
        (ends,) = _ssm_call(u, bmat, cmat, a_blk, init, with_y=False)
        init = _carry_call(ends, a_seg)
    yf, yb, _ = _ssm_call(u, bmat, cmat, a_blk, init, with_y=True)
    sn = _ssm_post_call(yf, yb, u, w, perm.T)
    return _post_call(x, att, sn, w)


def kernel(x_prompt, x_sample, norm1_g, w_in, q_norm_g, w_uq, kv_norm_g, w_ukv, lam_re, lam_im, log_dt,
           b_re, b_im, c_re, c_im, d_skip, w_glu, b_glu, attn_out_g, ssm_out_g, w_out, norm2_g, w_mlp1,
           w_mlp2, final_g):
    assert norm1_g.shape[0] == 1, "single-layer trunk"
    w = _pack_weights(norm1_g[0], w_in[0], q_norm_g[0], w_uq[0], kv_norm_g[0], w_ukv[0], d_skip[0], w_glu[0],
                      b_glu[0], attn_out_g[0], ssm_out_g[0], w_out[0], norm2_g[0], w_mlp1[0], w_mlp2[0], final_g)
    bp, lp, _ = x_prompt.shape
    bs, ls, _ = x_sample.shape
    assert bp == N_GROUPS and bs == 1 and ls % N_GROUPS == 0
    seg = ls // N_GROUPS
    ssm = _pack_ssm(lam_re[0], lam_im[0], log_dt[0], b_re[0], b_im[0], c_re[0], c_im[0], seg)
    cos_t, sin_t = _rope_tables(max(lp, ls))
    rope_p = (cos_t[None, :lp], sin_t[None, :lp])
    rope_s = (cos_t[:ls].reshape(N_GROUPS, seg, HEAD_PAD), sin_t[:ls].reshape(N_GROUPS, seg, HEAD_PAD))
    y_prompt = _trunk(x_prompt, rope_p, False, w, ssm)
    y_sample = _trunk(x_sample.reshape(N_GROUPS, seg, D_MODEL), rope_s, True, w, ssm)
    return y_prompt, y_sample.reshape(bs, ls, D_MODEL)
```
